```python
import jax, jax.numpy as jnp
from jax import lax
import numpy as np

D_MODEL = 2048
BATCH = 16
SEQ = 256
DEPTH = 1
DEC_BATCH = 4
DEC_SEQ = 1024
PAST_LEN = 512

GRID_W = 64
N_HEADS = 16
N_KV_HEADS = 4
HEAD_DIM = 128
Q_GROUP = N_HEADS // N_KV_HEADS
ATTN_WIDTH = N_HEADS * HEAD_DIM
KV_WIDTH = N_KV_HEADS * HEAD_DIM
WINDOW = 128
BLOCK = 128
CONV_WIDTH = D_MODEL // 2
CONV_KERNEL = 31
N_EXPERTS = 16
D_EXPERT = D_MODEL // 2
EC_FACTOR = 2
ROPE_BASE = 10000.0
EPS = 1e-6
N_MOD = 6
NEG = -1e30
SPLITS = (2 * CONV_WIDTH,
          2 * CONV_WIDTH + ATTN_WIDTH,
          2 * CONV_WIDTH + ATTN_WIDTH + KV_WIDTH,
          2 * CONV_WIDTH + ATTN_WIDTH + 2 * KV_WIDTH,
          2 * CONV_WIDTH + ATTN_WIDTH + 2 * KV_WIDTH + D_MODEL)
IN_COLS = 2 * CONV_WIDTH + ATTN_WIDTH + 2 * KV_WIDTH + 2 * D_MODEL

kernel_name = "hybrid_diffusion_conformer_swa_ec_step"


def rmsnorm(x, g):
    xf = x.astype(jnp.float32)
    y = xf * lax.rsqrt(jnp.mean(xf * xf, axis=-1, keepdims=True) + EPS)
    return y.astype(x.dtype) * g


def modulate_params(cond, w_ada, b_ada):
    m = jax.nn.silu(cond) @ w_ada + b_ada
    return jnp.split(m[:, None, :], N_MOD, axis=-1)


def rope_2d(x):
    n = x.shape[1]
    rows = n // GRID_W
    row = jnp.broadcast_to(jnp.arange(rows)[:, None], (rows, GRID_W)).reshape(-1).astype(jnp.float32)
    col = jnp.broadcast_to(jnp.arange(GRID_W)[None, :], (rows, GRID_W)).reshape(-1).astype(jnp.float32)
    half = HEAD_DIM // 2
    quarter = half // 2
    inv = ROPE_BASE ** (-jnp.arange(quarter, dtype=jnp.float32) / quarter)

    def rot(xh, pos):
        ang = pos[:, None] * inv[None, :]
        cos = jnp.cos(ang)[None, :, None, :]
        sin = jnp.sin(ang)[None, :, None, :]
        x1 = xh[..., :quarter].astype(jnp.float32)
        x2 = xh[..., quarter:].astype(jnp.float32)
        return jnp.concatenate([x1 * cos - x2 * sin, x1 * sin + x2 * cos], axis=-1)

    out = jnp.concatenate([rot(x[..., :half], row), rot(x[..., half:], col)], axis=-1)
    return out.astype(x.dtype)


def conformer_conv(u, w_dw, b_dw, ln_g, ln_b, w_pw2):
    a, gt = jnp.split(u, 2, axis=-1)
    z = a * jax.nn.sigmoid(gt)
    pad = CONV_KERNEL // 2
    z = lax.conv_general_dilated(z, w_dw[:, None, :], window_strides=(1,),
                                 padding=[(pad, pad)],
                                 dimension_numbers=('NWC', 'WIO', 'NWC'),
                                 feature_group_count=CONV_WIDTH) + b_dw
    zf = z.astype(jnp.float32)
    mu = jnp.mean(zf, axis=-1, keepdims=True)
    var = jnp.mean(jnp.square(zf - mu), axis=-1, keepdims=True)
    z = ((zf - mu) * lax.rsqrt(var + EPS)).astype(z.dtype) * ln_g + ln_b
    return jax.nn.silu(z) @ w_pw2


def sink_softmax_av(sc, vals, sink):
    sk = sink.astype(jnp.float32).reshape(N_KV_HEADS, Q_GROUP)[None, :, :, None, None]
    m = jnp.maximum(jnp.max(sc, axis=-1, keepdims=True), sk)
    p = jnp.exp(sc - m)
    den = jnp.sum(p, axis=-1, keepdims=True) + jnp.exp(sk - m)
    w = (p / den).astype(vals.dtype)
    return jnp.einsum('bkgqs,bskd->bqkgd', w, vals)


def context_attention(q, k, v, sink):
    b, s = q.shape[:2]
    nb = s // BLOCK
    scale = HEAD_DIM ** -0.5
    qb = q.reshape(b, nb, BLOCK, N_KV_HEADS, Q_GROUP, HEAD_DIM).transpose(1, 0, 2, 3, 4, 5)

    def one(qblk):
        sc = jnp.einsum('bqkgd,bskd->bkgqs', qblk, k).astype(jnp.float32) * scale
        return sink_softmax_av(sc, v, sink)

    out = lax.map(one, qb)
    return out.transpose(1, 0, 2, 3, 4, 5).reshape(b, s, ATTN_WIDTH)


def latent_attention(q, k, v, ck, cv, sink):
    b, n = q.shape[:2]
    nb = n // BLOCK
    span = BLOCK + 2 * WINDOW
    scale = HEAD_DIM ** -0.5
    kp = jnp.pad(k, ((0, 0), (WINDOW, WINDOW), (0, 0), (0, 0)))
    vp = jnp.pad(v, ((0, 0), (WINDOW, WINDOW), (0, 0), (0, 0)))
    qb = q.reshape(b, nb, BLOCK, N_KV_HEADS, Q_GROUP, HEAD_DIM).transpose(1, 0, 2, 3, 4, 5)

    def one(args):
        j, qblk = args
        start = j * BLOCK
        kw = lax.dynamic_slice_in_dim(kp, start, span, axis=1)
        vw = lax.dynamic_slice_in_dim(vp, start, span, axis=1)
        qpos = start + jnp.arange(BLOCK)
        kpos = start - WINDOW + jnp.arange(span)
        valid = (jnp.abs(qpos[:, None] - kpos[None, :]) <= WINDOW) & (kpos >= 0)[None, :] & (kpos < n)[None, :]
        s_loc = jnp.einsum('bqkgd,bskd->bkgqs', qblk, kw).astype(jnp.float32) * scale
        s_loc = jnp.where(valid[None, None, None], s_loc, NEG)
        s_ctx = jnp.einsum('bqkgd,bskd->bkgqs', qblk, ck).astype(jnp.float32) * scale
        sc = jnp.concatenate([s_loc, s_ctx], axis=-1)
        vals = jnp.concatenate([vw, cv], axis=1)
        return sink_softmax_av(sc, vals, sink)

    out = lax.map(one, (jnp.arange(nb), qb))
    return out.transpose(1, 0, 2, 3, 4, 5).reshape(b, n, ATTN_WIDTH)


def expert_choice_ffn(h, w_r, b_r, w_g, w_u, w_d):
    shp = h.shape
    xt = h.reshape(-1, D_MODEL)
    n = xt.shape[0]
    cap = EC_FACTOR * n // N_EXPERTS
    logits = (xt @ w_r).astype(jnp.float32) + b_r.astype(jnp.float32)
    aff = jax.nn.softmax(logits, axis=-1)
    gate, idx = lax.top_k(aff.T, cap)
    xe = xt[idx]
    hid = jax.nn.silu(jnp.einsum('ecd,edf->ecf', xe, w_g)) * jnp.einsum('ecd,edf->ecf', xe, w_u)
    ye = jnp.einsum('ecf,efd->ecd', hid, w_d) * gate[..., None].astype(xe.dtype)
    out = jnp.zeros_like(xt).at[idx.reshape(-1)].add(ye.reshape(-1, D_MODEL))
    return out.reshape(shp)


def trunk_layer(x, mod, n1g, w_in, conv_dw, conv_dw_b, conv_ln_g, conv_ln_b, w_conv_out,
                sink, w_attn_out, w_out, n2g, w_router, b_router, w_eg, w_eu, w_ed,
                ctx_k=None, ctx_v=None):
    sh1, sc1, g1, sh2, sc2, g2 = mod
    b, n, _ = x.shape
    h = rmsnorm(x, n1g) * (1 + sc1) + sh1
    u = h @ w_in
    glu_in, q, k, v, ga, gb = jnp.split(u, SPLITS, axis=-1)
    conv_out = conformer_conv(glu_in, conv_dw, conv_dw_b, conv_ln_g, conv_ln_b, w_conv_out)
    q = q.reshape(b, n, N_HEADS, HEAD_DIM)
    k = k.reshape(b, n, N_KV_HEADS, HEAD_DIM)
    v = v.reshape(b, n, N_KV_HEADS, HEAD_DIM)
    if ctx_k is None:
        att = context_attention(q, k, v, sink)
    else:
        att = latent_attention(rope_2d(q), rope_2d(k), v, ctx_k, ctx_v, sink)
    attn_out = att @ w_attn_out
    mixed = (jax.nn.sigmoid(ga) * conv_out + jax.nn.sigmoid(gb) * attn_out) @ w_out
    x = x + g1 * mixed
    h2 = rmsnorm(x, n2g) * (1 + sc2) + sh2
    x = x + g2 * expert_choice_ffn(h2, w_router, b_router, w_eg, w_eu, w_ed)
    return x, k, v


def setup_inputs(seed: int = 0) -> dict:
    key = jax.random.key(seed)
    ks = jax.random.split(key, 32)
    f32 = jnp.float32
    nrm = lambda k, shp, s: jax.random.normal(k, shp, f32) * s
    D = D_MODEL
    return {
        "x_prompt": nrm(ks[0], (BATCH, SEQ, D), 1.0),
        "x_sample": nrm(ks[1], (DEC_BATCH, DEC_SEQ, D), 1.0),
        "c": nrm(ks[2], (DEC_BATCH, D), 1.0),
        "cache_k": nrm(ks[3], (DEC_BATCH, DEPTH, PAST_LEN, N_KV_HEADS, HEAD_DIM), 1.0),
        "cache_v": nrm(ks[4], (DEC_BATCH, DEPTH, PAST_LEN, N_KV_HEADS, HEAD_DIM), 1.0),
        "c_ctx": nrm(ks[5], (D,), 1.0),
        "w_ada": nrm(ks[6], (DEPTH, D, N_MOD * D), 0.5 * D ** -0.5),
        "b_ada": nrm(ks[7], (DEPTH, N_MOD * D), 0.01),
        "norm1_g": 1.0 + nrm(ks[8], (DEPTH, D), 0.01),
        "w_in": nrm(ks[9], (DEPTH, D, IN_COLS), D ** -0.5),
        "conv_dw": nrm(ks[10], (DEPTH, CONV_KERNEL, CONV_WIDTH), CONV_KERNEL ** -0.5),
        "conv_dw_b": nrm(ks[11], (DEPTH, CONV_WIDTH), 0.01),
        "conv_ln_g": 1.0 + nrm(ks[12], (DEPTH, CONV_WIDTH), 0.01),
        "conv_ln_b": nrm(ks[13], (DEPTH, CONV_WIDTH), 0.01),
        "w_conv_out": nrm(ks[14], (DEPTH, CONV_WIDTH, D), CONV_WIDTH ** -0.5),
        "attn_sink": nrm(ks[15], (DEPTH, N_HEADS), 0.5),
        "w_attn_out": nrm(ks[16], (DEPTH, ATTN_WIDTH, D), ATTN_WIDTH ** -0.5),
        "w_out": nrm(ks[17], (DEPTH, D, D), D ** -0.5),
        "norm2_g": 1.0 + nrm(ks[18], (DEPTH, D), 0.01),
        "w_router": nrm(ks[19], (DEPTH, D, N_EXPERTS), D ** -0.5),
        "b_router": nrm(ks[20], (DEPTH, N_EXPERTS), 0.01),
        "w_exp_gate": nrm(ks[21], (DEPTH, N_EXPERTS, D, D_EXPERT), D ** -0.5),
        "w_exp_up": nrm(ks[22], (DEPTH, N_EXPERTS, D, D_EXPERT), D ** -0.5),
        "w_exp_down": nrm(ks[23], (DEPTH, N_EXPERTS, D_EXPERT, D), D_EXPERT ** -0.5),
        "final_norm_g": 1.0 + nrm(ks[24], (D,), 0.01),
    }


def reference(x_prompt, x_sample, c, cache_k, cache_v, c_ctx, w_ada, b_ada, norm1_g, w_in,
              conv_dw, conv_dw_b, conv_ln_g, conv_ln_b, w_conv_out, attn_sink, w_attn_out,
              w_out, norm2_g, w_router, b_router, w_exp_gate, w_exp_up, w_exp_down,
              final_norm_g):
    xp = x_prompt
    xs = x_sample
    new_k = []
    new_v = []
    for l in range(DEPTH):
        lp = (norm1_g[l], w_in[l], conv_dw[l], conv_dw_b[l], conv_ln_g[l], conv_ln_b[l],
              w_conv_out[l], attn_sink[l], w_attn_out[l], w_out[l], norm2_g[l],
              w_router[l], b_router[l], w_exp_gate[l], w_exp_up[l], w_exp_down[l])
        mod_ctx = modulate_params(c_ctx[None, :], w_ada[l], b_ada[l])
        mod_lat = modulate_params(c, w_ada[l], b_ada[l])
        xp, k_ctx, v_ctx = trunk_layer(xp, mod_ctx, *lp)
        new_k.append(k_ctx)
        new_v.append(v_ctx)
        xs, _, _ = trunk_layer(xs, mod_lat, *lp, ctx_k=cache_k[:, l], ctx_v=cache_v[:, l])
    state_k = jnp.stack(new_k, axis=1)
    state_v = jnp.stack(new_v, axis=1)
    y_prompt = rmsnorm(xp, final_norm_g)
    y_sample = rmsnorm(xs, final_norm_g)
    return (y_prompt, y_sample, state_k, state_v)
```

```python
import functools

import jax
import jax.numpy as jnp
from jax import lax
from jax.experimental import pallas as pl
from jax.experimental.pallas import tpu as pltpu

D = 2048
N_CTX_SEQ, CTX_LEN = 16, 256
N_LAT_SEQ, LAT_LEN = 4, 1024
N_CTX = N_CTX_SEQ * CTX_LEN
N_LAT = N_LAT_SEQ * LAT_LEN
N_TOK = N_CTX + N_LAT
PAST = 512
GRID_W = 64
N_HEADS, N_KV, HD = 16, 4, 128
Q_GROUP = N_HEADS // N_KV
WINDOW = 128
ABLK = 128
CONV_W = 1024
CONV_K = 31
CONV_PAD = CONV_K // 2
N_EXP = 16
D_EXP = 1024
CAP = 2 * N_CTX // N_EXP
N_SETS = 2
ROPE_BASE = 10000.0
EPS = 1e-6
N_MOD = 6
NEG = -1e30
IN_COLS = 2 * CONV_W + N_HEADS * HD + 2 * N_KV * HD + 2 * D
COL_Q = 2 * CONV_W
COL_K = COL_Q + N_HEADS * HD
COL_V = COL_K + N_KV * HD
COL_GA = COL_V + N_KV * HD
COL_GB = COL_GA + D
MOD_ROWS = 8
CTX_MOD_ROW = N_LAT_SEQ

VMEM_LIMIT = 56 * 1024 * 1024
F32 = jnp.float32
BF16 = jnp.bfloat16


def _params(sem, vmem=VMEM_LIMIT):
    return pltpu.CompilerParams(dimension_semantics=sem, vmem_limit_bytes=vmem)


def _mod_row(tok_block, block_tokens):
    ctx_blocks = N_CTX // block_tokens
    per_seq = LAT_LEN // block_tokens
    return jnp.where(tok_block < ctx_blocks, CTX_MOD_ROW, (tok_block - ctx_blocks) // per_seq)


def _silu(x):
    return x * jax.nn.sigmoid(x)


def _rms(x, g):
    return x * lax.rsqrt(jnp.mean(x * x, axis=-1, keepdims=True) + EPS) * g


def _ada_kernel(cond_ref, w_ref, b_ref, o_ref):
    a = _silu(cond_ref[...]).astype(BF16)
    o_ref[...] = jnp.dot(a, w_ref[...].astype(BF16), preferred_element_type=F32) + b_ref[...]


def _ada(cond, w_ada, b_ada):
    tn = 1024
    n = N_MOD * D
    return pl.pallas_call(
        _ada_kernel,
        grid=(n // tn,),
        in_specs=[pl.BlockSpec((MOD_ROWS, D), lambda j: (0, 0)),
                  pl.BlockSpec((D, tn), lambda j: (0, j)),
                  pl.BlockSpec((1, tn), lambda j: (0, j))],
        out_specs=pl.BlockSpec((MOD_ROWS, tn), lambda j: (0, j)),
        out_shape=jax.ShapeDtypeStruct((MOD_ROWS, n), F32),
        compiler_params=_params(("arbitrary",)),
        name="ada",
    )(cond, w_ada, b_ada.reshape(1, n))


def _inproj_kernel(x_ref, mod_ref, g_ref, w_ref, o_ref, h_ref):
    @pl.when(pl.program_id(1) == 0)
    def _():
        m = mod_ref[0]
        h = _rms(x_ref[...], g_ref[...]) * (1.0 + m[1:2, :]) + m[0:1, :]
        h_ref[...] = h.astype(BF16)

    o_ref[...] = jnp.dot(h_ref[...], w_ref[...], preferred_element_type=F32)


def _inproj(x, mod, n1g, w_in_bf):
    tm, tn = 1024, 1024
    return pl.pallas_call(
        _inproj_kernel,
        grid=(N_TOK // tm, IN_COLS // tn),
        in_specs=[pl.BlockSpec((tm, D), lambda i, j: (i, 0)),
                  pl.BlockSpec((1, N_MOD, D), lambda i, j: (_mod_row(i, tm), 0, 0)),
                  pl.BlockSpec((1, D), lambda i, j: (0, 0)),
                  pl.BlockSpec((D, tn), lambda i, j: (0, j))],
        out_specs=pl.BlockSpec((tm, tn), lambda i, j: (i, j)),
        out_shape=jax.ShapeDtypeStruct((N_TOK, IN_COLS), F32),
        scratch_shapes=[pltpu.VMEM((tm, D), BF16)],
        compiler_params=_params(("arbitrary", "arbitrary")),
        name="inproj",
    )(x, mod, n1g, w_in_bf)


CONV_T = 256
CONV_HALO = 16
CONV_ROWS = 32


def _conv_kernel(main_ref, prev_ref, next_ref, w_ref, b_ref, lg_ref, lb_ref, o_ref, z_ref):
    i = pl.program_id(0)
    ctx_blocks = N_CTX // CONV_T
    per_seq = LAT_LEN // CONV_T
    pos = (i - ctx_blocks) % per_seq
    is_lat = i >= ctx_blocks
    has_prev = jnp.logical_and(is_lat, pos > 0)
    has_next = jnp.logical_and(is_lat, pos < per_seq - 1)

    def glu(ref):
        v = ref[...]
        return v[:, :CONV_W] * jax.nn.sigmoid(v[:, CONV_W:])

    z_ref[0:CONV_HALO, :] = jnp.where(has_prev, glu(prev_ref), 0.0)
    z_ref[CONV_HALO:CONV_HALO + CONV_T, :] = glu(main_ref)
    z_ref[CONV_HALO + CONV_T:, :] = jnp.where(has_next, glu(next_ref), 0.0)

    for r in range(CONV_T // CONV_ROWS):
        base = CONV_HALO - CONV_PAD + r * CONV_ROWS
        acc = jnp.zeros((CONV_ROWS, CONV_W), F32)
        for j in range(CONV_K):
            acc = acc + z_ref[base + j:base + j + CONV_ROWS, :] * w_ref[j:j + 1, :]
        acc = acc + b_ref[...]
        mu = jnp.mean(acc, axis=-1, keepdims=True)
        cen = acc - mu
        var = jnp.mean(cen * cen, axis=-1, keepdims=True)
        y = cen * lax.rsqrt(var + EPS) * lg_ref[...] + lb_ref[...]
        o_ref[r * CONV_ROWS:(r + 1) * CONV_ROWS, :] = _silu(y).astype(BF16)


def _conv(u, w_dw, b_dw, ln_g, ln_b):
    nblk = N_TOK // CONV_T
    hpb = CONV_T // CONV_HALO
    last = N_TOK // CONV_HALO - 1
    row = lambda i: (0, 0)
    return pl.pallas_call(
        _conv_kernel,
        grid=(nblk,),
        in_specs=[pl.BlockSpec((CONV_T, 2 * CONV_W), lambda i: (i, 0)),
                  pl.BlockSpec((CONV_HALO, 2 * CONV_W), lambda i: (jnp.maximum(i * hpb - 1, 0), 0)),
                  pl.BlockSpec((CONV_HALO, 2 * CONV_W), lambda i: (jnp.minimum((i + 1) * hpb, last), 0)),
                  pl.BlockSpec((CONV_K, CONV_W), row),
                  pl.BlockSpec((1, CONV_W), row),
                  pl.BlockSpec((1, CONV_W), row),
                  pl.BlockSpec((1, CONV_W), row)],
        out_specs=pl.BlockSpec((CONV_T, CONV_W), lambda i: (i, 0)),
        out_shape=jax.ShapeDtypeStruct((N_TOK, CONV_W), BF16),
        scratch_shapes=[pltpu.VMEM((CONV_T + 2 * CONV_HALO, CONV_W), F32)],
        compiler_params=_params(("arbitrary",)),
        name="conv",
    )(u, u, u, w_dw, b_dw, ln_g, ln_b)


def _stack_heads(q):
    return jnp.concatenate([q[:, g * HD:(g + 1) * HD] for g in range(Q_GROUP)], axis=0)


def _unstack_heads(o, rows):
    return jnp.concatenate([o[g * rows:(g + 1) * rows, :] for g in range(Q_GROUP)], axis=1)


def _sink_column(sink_ref, kv, rows):
    rid = lax.broadcasted_iota(jnp.int32, (Q_GROUP * rows, 1), 0)
    col = jnp.full((Q_GROUP * rows, 1), sink_ref[kv * Q_GROUP], F32)
    for g in range(1, Q_GROUP):
        col = jnp.where(rid >= g * rows, sink_ref[kv * Q_GROUP + g], col)
    return col


def _softmax_av(s, sk, vals):
    m = jnp.maximum(jnp.max(s, axis=-1, keepdims=True), sk)
    p = jnp.exp(s - m)
    den = jnp.sum(p, axis=-1, keepdims=True) + jnp.exp(sk - m)
    o = jnp.dot(p.astype(BF16), vals, preferred_element_type=F32)
    return o / den


def _ctx_attn_kernel(sink_ref, q_ref, k_ref, v_ref, o_ref):
    kv = pl.program_id(1)
    q = _stack_heads(q_ref[...]).astype(BF16)
    k = k_ref[...].astype(BF16)
    s = lax.dot_general(q, k, (((1,), (1,)), ((), ())), preferred_element_type=F32) * (HD ** -0.5)
    o = _softmax_av(s, _sink_column(sink_ref, kv, CTX_LEN), v_ref[...].astype(BF16))
    o_ref[...] = _unstack_heads(o, CTX_LEN).astype(BF16)


def _ctx_attn(u, sink):
    qw = Q_GROUP * HD
    return pl.pallas_call(
        _ctx_attn_kernel,
        grid_spec=pltpu.PrefetchScalarGridSpec(
            num_scalar_prefetch=1,
            grid=(N_CTX_SEQ, N_KV),
            in_specs=[pl.BlockSpec((CTX_LEN, qw), lambda b, kv, s: (b, COL_Q // qw + kv)),
                      pl.BlockSpec((CTX_LEN, HD), lambda b, kv, s: (b, COL_K // HD + kv)),
                      pl.BlockSpec((CTX_LEN, HD), lambda b, kv, s: (b, COL_V // HD + kv))],
            out_specs=pl.BlockSpec((CTX_LEN, qw), lambda b, kv, s: (b, kv))),
        out_shape=jax.ShapeDtypeStruct((N_CTX, N_HEADS * HD), BF16),
        compiler_params=_params(("arbitrary", "arbitrary")),
        name="ctx_attn",
    )(sink, u, u, u)


def _rope(x, cos, sin):
    lane = lax.broadcasted_iota(jnp.int32, x.shape, 1)
    fwd = pltpu.roll(x, HD - 32, axis=1)
    bwd = pltpu.roll(x, 32, axis=1)
    partner = jnp.where((lane & 32) == 0, fwd, bwd)
    return x * cos + partner * sin


def _lat_attn_kernel(sink_ref, q_ref, kp_ref, kc_ref, kn_ref, vp_ref, vc_ref, vn_ref,
                     ck_ref, cv_ref, cos_ref, sin_ref, o_ref):
    kv = pl.program_id(1)
    j = pl.program_id(2)
    nblk = LAT_LEN // ABLK
    jp = jnp.maximum(j - 1, 0)
    jn = jnp.minimum(j + 1, nblk - 1)

    def table(ref, blk):
        return ref[pl.ds(pl.multiple_of(blk * ABLK, ABLK), ABLK), :]

    cos_c, sin_c = table(cos_ref, j), table(sin_ref, j)
    q = q_ref[...]
    qs = jnp.concatenate(
        [_rope(q[:, g * HD:(g + 1) * HD], cos_c, sin_c) for g in range(Q_GROUP)], axis=0).astype(BF16)
    k_loc = jnp.concatenate(
        [_rope(kp_ref[...], table(cos_ref, jp), table(sin_ref, jp)),
         _rope(kc_ref[...], cos_c, sin_c),
         _rope(kn_ref[...], table(cos_ref, jn), table(sin_ref, jn))], axis=0).astype(BF16)
    keys = jnp.concatenate([k_loc, ck_ref[...].astype(BF16)], axis=0)
    vals = jnp.concatenate([vp_ref[...], vc_ref[...], vn_ref[...], cv_ref[...]], axis=0).astype(BF16)

    s = lax.dot_general(qs, keys, (((1,), (1,)), ((), ())), preferred_element_type=F32) * (HD ** -0.5)
    span = ABLK + 2 * WINDOW
    rows = Q_GROUP * ABLK
    r = lax.broadcasted_iota(jnp.int32, (rows, span + PAST), 0) & (ABLK - 1)
    c = lax.broadcasted_iota(jnp.int32, (rows, span + PAST), 1)
    in_band = jnp.logical_and(c >= r, c <= r + 2 * WINDOW)
    lo = jnp.where(j > 0, 0, WINDOW)
    hi = jnp.where(j < nblk - 1, span, WINDOW + ABLK)
    in_seq = jnp.logical_and(c >= lo, c < hi)
    valid = jnp.logical_or(c >= span, jnp.logical_and(in_band, in_seq))
    s = jnp.where(valid, s, NEG)
    o = _softmax_av(s, _sink_column(sink_ref, kv, ABLK), vals)
    o_ref[...] = _unstack_heads(o, ABLK).astype(BF16)


def _lat_attn(u, sink, ck, cv, cos, sin):
    qw = Q_GROUP * HD
    nblk = LAT_LEN // ABLK
    base = N_CTX // ABLK

    def kv_spec(col0, shift):
        def imap(b, kv, j, s):
            jj = jnp.clip(j + shift, 0, nblk - 1)
            return (base + b * nblk + jj, col0 // HD + kv)
        return pl.BlockSpec((ABLK, HD), imap)

    cache_spec = pl.BlockSpec((None, PAST, HD), lambda b, kv, j, s: (b, 0, kv))
    table_spec = pl.BlockSpec((LAT_LEN, HD), lambda b, kv, j, s: (0, 0))
    return pl.pallas_call(
        _lat_attn_kernel,
        grid_spec=pltpu.PrefetchScalarGridSpec(
            num_scalar_prefetch=1,
            grid=(N_LAT_SEQ, N_KV, nblk),
            in_specs=[pl.BlockSpec((ABLK, qw), lambda b, kv, j, s: (base + b * nblk + j, COL_Q // qw + kv)),
                      kv_spec(COL_K, -1), kv_spec(COL_K, 0), kv_spec(COL_K, 1),
                      kv_spec(COL_V, -1), kv_spec(COL_V, 0), kv_spec(COL_V, 1),
                      cache_spec, cache_spec, table_spec, table_spec],
            out_specs=pl.BlockSpec((ABLK, qw), lambda b, kv, j, s: (b * nblk + j, kv))),
        out_shape=jax.ShapeDtypeStruct((N_LAT, N_HEADS * HD), BF16),
        compiler_params=_params(("arbitrary", "arbitrary", "arbitrary")),
        name="lat_attn",
    )(sink, u, u, u, u, u, u, u, ck, cv, cos, sin)


def _rope_tables():
    n = jnp.arange(LAT_LEN)
    row = (n // GRID_W).astype(F32)
    col = (n % GRID_W).astype(F32)
    quarter = HD // 4
    inv = ROPE_BASE ** (-jnp.arange(quarter, dtype=F32) / quarter)
    ang_r = row[:, None] * inv[None, :]
    ang_c = col[:, None] * inv[None, :]
    cos = jnp.concatenate([jnp.cos(ang_r)] * 2 + [jnp.cos(ang_c)] * 2, axis=-1)
    sin = jnp.concatenate([-jnp.sin(ang_r), jnp.sin(ang_r), -jnp.sin(ang_c), jnp.sin(ang_c)], axis=-1)
    return cos, sin


def _mix_kernel(z_ref, a_ref, ga_ref, gb_ref, wc_ref, wa_ref, o_ref):
    conv_out = jnp.dot(z_ref[...], wc_ref[...], preferred_element_type=F32)
    attn_out = jnp.dot(a_ref[...], wa_ref[...], preferred_element_type=F32)
    mixed = jax.nn.sigmoid(ga_ref[...]) * conv_out + jax.nn.sigmoid(gb_ref[...]) * attn_out
    o_ref[...] = mixed.astype(BF16)


def _mix(zc, att, u, wc_bf, wa_bf):
    tm, tn = 512, 1024
    return pl.pallas_call(
        _mix_kernel,
        grid=(D // tn, N_TOK // tm),
        in_specs=[pl.BlockSpec((tm, CONV_W), lambda j, i: (i, 0)),
                  pl.BlockSpec((tm, N_HEADS * HD), lambda j, i: (i, 0)),
                  pl.BlockSpec((tm, tn), lambda j, i: (i, COL_GA // tn + j)),
                  pl.BlockSpec((tm, tn), lambda j, i: (i, COL_GB // tn + j)),
                  pl.BlockSpec((CONV_W, tn), lambda j, i: (0, j)),
                  pl.BlockSpec((N_HEADS * HD, tn), lambda j, i: (0, j))],
        out_specs=pl.BlockSpec((tm, tn), lambda j, i: (i, j)),
        out_shape=jax.ShapeDtypeStruct((N_TOK, D), BF16),
        compiler_params=_params(("arbitrary", "arbitrary")),
        name="mix",
    )(zc, att, u, u, wc_bf, wa_bf)


def _split_bf16(x):
    hi = x.astype(BF16)
    lo = (x - hi.astype(F32)).astype(BF16)
    return hi, lo


def _outproj_kernel(mx_ref, x_ref, mod_ref, g_ref, w_ref, wr_ref, br_ref, x1_ref, h2_ref, lg_ref):
    m = mod_ref[0]
    x1 = x_ref[...] + m[2:3, :] * jnp.dot(mx_ref[...], w_ref[...], preferred_element_type=F32)
    x1_ref[...] = x1
    h2 = _rms(x1, g_ref[...]) * (1.0 + m[4:5, :]) + m[3:4, :]
    h2_ref[...] = h2
    h_hi, h_lo = _split_bf16(h2)
    w_hi, w_lo = _split_bf16(wr_ref[...])
    nt = (((1,), (1,)), ((), ()))
    lg = (lax.dot_general(w_hi, h_hi, nt, preferred_element_type=F32)
          + lax.dot_general(w_lo, h_hi, nt, preferred_element_type=F32)
          + lax.dot_general(w_hi, h_lo, nt, preferred_element_type=F32))
    lg_ref[...] = lg + br_ref[...]


def _outproj(mixed, x, mod, n2g, w_out_bf, w_router_t, b_router):
    tm = 256
    return pl.pallas_call(
        _outproj_kernel,
        grid=(N_TOK // tm,),
        in_specs=[pl.BlockSpec((tm, D), lambda i: (i, 0)),
                  pl.BlockSpec((tm, D), lambda i: (i, 0)),
                  pl.BlockSpec((1, N_MOD, D), lambda i: (_mod_row(i, tm), 0, 0)),
                  pl.BlockSpec((1, D), lambda i: (0, 0)),
                  pl.BlockSpec((D, D), lambda i: (0, 0)),
                  pl.BlockSpec((N_EXP, D), lambda i: (0, 0)),
                  pl.BlockSpec((N_EXP, 1), lambda i: (0, 0))],
        out_specs=[pl.BlockSpec((tm, D), lambda i: (i, 0)),
                   pl.BlockSpec((tm, D), lambda i: (i, 0)),
                   pl.BlockSpec((N_EXP, tm), lambda i: (0, i))],
        out_shape=[jax.ShapeDtypeStruct((N_TOK, D), F32),
                   jax.ShapeDtypeStruct((N_TOK, D), F32),
                   jax.ShapeDtypeStruct((N_EXP, N_TOK), F32)],
        compiler_params=_params(("arbitrary",)),
        name="outproj",
    )(mixed, x, mod, n2g, w_out_bf, w_router_t, b_router)


SEL_CHUNK = 128
CMB_T = 128
N_CMB = N_CTX // CMB_T


def _prefix_sum(flags, out_ref):
    n = flags.shape[1]
    ri = lax.broadcasted_iota(jnp.int32, (SEL_CHUNK, SEL_CHUNK), 0)
    ci = lax.broadcasted_iota(jnp.int32, (SEL_CHUNK, SEL_CHUNK), 1)
    tri = jnp.where(ri <= ci, 1.0, 0.0).astype(BF16)
    run = jnp.zeros((N_EXP, 1), F32)
    for ch in range(n // SEL_CHUNK):
        blk = flags[:, ch * SEL_CHUNK:(ch + 1) * SEL_CHUNK].astype(BF16)
        within = jnp.dot(blk, tri, preferred_element_type=F32) + run
        out_ref[:, ch * SEL_CHUNK:(ch + 1) * SEL_CHUNK] = within
        run = within[:, SEL_CHUNK - 1:SEL_CHUNK]


def _select_kernel(lg_ref, idx_ref, lo_ref, gw_ref, cs_ref):
    n = lg_ref.shape[1]
    lg = lg_ref[...]
    ex = jnp.exp(lg - jnp.max(lg, axis=0, keepdims=True))
    aff = ex / jnp.sum(ex, axis=0, keepdims=True)

    def bit_step(b, cur):
        cand = cur | (1 << (30 - b))
        cnt = jnp.sum(jnp.where(aff >= pltpu.bitcast(cand, F32), 1.0, 0.0), axis=1, keepdims=True)
        return jnp.where(cnt >= CAP, cand, cur)

    thr = pltpu.bitcast(lax.fori_loop(0, 31, bit_step, jnp.zeros((N_EXP, 1), jnp.int32)), F32)
    above = aff > thr
    tied = aff == thr
    room = CAP - jnp.sum(jnp.where(above, 1.0, 0.0), axis=1, keepdims=True)
    _prefix_sum(jnp.where(tied, 1.0, 0.0), cs_ref)
    sel = jnp.logical_or(above, jnp.logical_and(tied, cs_ref[...] <= room))
    sel_f = jnp.where(sel, 1.0, 0.0)
    gw_ref[0] = jnp.where(sel, aff, 0.0)
    _prefix_sum(sel_f, cs_ref)

    ti = lax.broadcasted_iota(jnp.int32, (n, SEL_CHUNK), 0)
    bi = lax.broadcasted_iota(jnp.int32, (n, SEL_CHUNK), 1)
    step = jnp.where(ti < bi * CMB_T, 1.0, 0.0).astype(BF16)
    lo_ref[0] = jnp.dot(sel_f.astype(BF16), step, preferred_element_type=F32).astype(jnp.int32)

    slot = lax.broadcasted_iota(jnp.int32, (CAP, n), 0).astype(F32)
    ones = jnp.ones((8, n), BF16)
    for e in range(N_EXP):
        le = jnp.where(cs_ref[e:e + 1, :] <= slot, 1.0, 0.0).astype(BF16)
        cnt = lax.dot_general(ones, le, (((1,), (1,)), ((), ())), preferred_element_type=F32)
        idx_ref[0, e:e + 1, :] = cnt[0:1, :].astype(jnp.int32)


def _select(logits_t):
    return pl.pallas_call(
        _select_kernel,
        grid=(N_SETS,),
        in_specs=[pl.BlockSpec((N_EXP, N_CTX), lambda s: (0, s))],
        out_specs=[pl.BlockSpec((1, N_EXP, CAP), lambda s: (s, 0, 0)),
                   pl.BlockSpec((1, N_EXP, SEL_CHUNK), lambda s: (s, 0, 0)),
                   pl.BlockSpec((1, N_EXP, N_CTX), lambda s: (s, 0, 0))],
        out_shape=[jax.ShapeDtypeStruct((N_SETS, N_EXP, CAP), jnp.int32),
                   jax.ShapeDtypeStruct((N_SETS, N_EXP, SEL_CHUNK), jnp.int32),
                   jax.ShapeDtypeStruct((N_SETS, N_EXP, N_CTX), F32)],
        scratch_shapes=[pltpu.VMEM((N_EXP, N_CTX), F32)],
        compiler_params=_params(("arbitrary",)),
        name="select",
    )(logits_t)


MOE_ROWS = N_SETS * CAP
MOE_TF = 256


def _moe_kernel(idx_ref, h_hbm, wg_ref, wu_ref, wd_ref, o_ref, xf_ref, xb_ref, sem):
    e = pl.program_id(0)
    f = pl.program_id(1)

    def row_copy(k):
        tok = idx_ref[e * MOE_ROWS + k]
        return pltpu.make_async_copy(h_hbm.at[pl.ds(tok, 1), :], xf_ref.at[pl.ds(k, 1), :], sem)

    @pl.when(f == 0)
    def _():
        def issue(k, c):
            row_copy(k).start()
            return c
        lax.fori_loop(0, MOE_ROWS, issue, 0)

        def drain(k, c):
            row_copy(k).wait()
            return c
        lax.fori_loop(0, MOE_ROWS, drain, 0)
        xb_ref[...] = xf_ref[...].astype(BF16)

    wg = wg_ref[...].astype(BF16)
    wu = wu_ref[...].astype(BF16)
    wd = wd_ref[...].astype(BF16)
    for s in range(N_SETS):
        rows = slice(s * CAP, (s + 1) * CAP)
        x = xb_ref[rows, :]
        g = jnp.dot(x, wg, preferred_element_type=F32)
        up = jnp.dot(x, wu, preferred_element_type=F32)
        hid = (_silu(g) * up).astype(BF16)
        part = jnp.dot(hid, wd, preferred_element_type=F32)

        @pl.when(f == 0)
        def _():
            o_ref[rows, :] = part

        @pl.when(f > 0)
        def _():
            o_ref[rows, :] += part


def _moe(rows, h2, w_g, w_u, w_d):
    return pl.pallas_call(
        _moe_kernel,
        grid_spec=pltpu.PrefetchScalarGridSpec(
            num_scalar_prefetch=1,
            grid=(N_EXP, D_EXP // MOE_TF),
            in_specs=[pl.BlockSpec(memory_space=pl.ANY),
                      pl.BlockSpec((None, D, MOE_TF), lambda e, f, s: (e, 0, f)),
                      pl.BlockSpec((None, D, MOE_TF), lambda e, f, s: (e, 0, f)),
                      pl.BlockSpec((None, MOE_TF, D), lambda e, f, s: (e, f, 0))],
            out_specs=pl.BlockSpec((None, MOE_ROWS, D), lambda e, f, s: (e, 0, 0)),
            scratch_shapes=[pltpu.VMEM((MOE_ROWS, D), F32),
                            pltpu.VMEM((MOE_ROWS, D), BF16),
                            pltpu.SemaphoreType.DMA(())]),
        out_shape=jax.ShapeDtypeStruct((N_EXP, MOE_ROWS, D), F32),
        compiler_params=_params(("arbitrary", "arbitrary")),
        name="moe",
    )(rows, h2, w_g, w_u, w_d)


def _combine_kernel(idx_ref, lo_ref, ye_hbm, x1_ref, gw_ref, mod_ref, g_ref, o_ref, buf_ref, sem):
    s = pl.program_id(0)
    b = pl.program_id(1)

    @pl.when(jnp.logical_and(s == 0, b == 0))
    def _():
        buf_ref[...] = jnp.zeros(buf_ref.shape, F32)

    tok0 = b * CMB_T

    def row_copy(e, c):
        t = idx_ref[(s * N_EXP + e) * CAP + c] - tok0
        src = ye_hbm.at[pl.ds(e * MOE_ROWS + s * CAP + c, 1), :]
        return pltpu.make_async_copy(src, buf_ref.at[e, pl.ds(t, 1), :], sem)

    def bounds(e):
        base = (s * N_EXP + e) * SEL_CHUNK + b
        return lo_ref[base], lo_ref[base + 1]

    for e in range(N_EXP):
        c0, c1 = bounds(e)

        def issue(c, carry, e=e):
            row_copy(e, c).start()
            return carry
        lax.fori_loop(c0, c1, issue, 0)

    for e in range(N_EXP):
        c0, c1 = bounds(e)

        def drain(c, carry, e=e):
            row_copy(e, c).wait()
            return carry
        lax.fori_loop(c0, c1, drain, 0)

    gw = gw_ref[0]
    acc = jnp.zeros((CMB_T, D), F32)
    for e in range(N_EXP):
        w = gw[:, e:e + 1]
        acc = acc + jnp.where(w != 0.0, w * buf_ref[e], 0.0)
    y = x1_ref[...] + mod_ref[0][5:6, :] * acc
    o_ref[...] = _rms(y, g_ref[...])


def _combine(idx_flat, lo_flat, ye_rows, x1, gw_t, mod, fng):
    return pl.pallas_call(
        _combine_kernel,
        grid_spec=pltpu.PrefetchScalarGridSpec(
            num_scalar_prefetch=2,
            grid=(N_SETS, N_CMB),
            in_specs=[pl.BlockSpec(memory_space=pl.ANY),
                      pl.BlockSpec((CMB_T, D), lambda s, b, i, l: (s * N_CMB + b, 0)),
                      pl.BlockSpec((1, CMB_T, N_EXP), lambda s, b, i, l: (s, b, 0)),
                      pl.BlockSpec((1, N_MOD, D),
                                   lambda s, b, i, l: (_mod_row(s * N_CMB + b, CMB_T), 0, 0)),
                      pl.BlockSpec((1, D), lambda s, b, i, l: (0, 0))],
            out_specs=pl.BlockSpec((CMB_T, D), lambda s, b, i, l: (s * N_CMB + b, 0)),
            scratch_shapes=[pltpu.VMEM((N_EXP, CMB_T, D), F32),
                            pltpu.SemaphoreType.DMA(())]),
        out_shape=jax.ShapeDtypeStruct((N_TOK, D), F32),
        compiler_params=_params(("arbitrary", "arbitrary")),
        name="combine",
    )(idx_flat, lo_flat, ye_rows, x1, gw_t, mod, fng)


def kernel(x_prompt, x_sample, c, cache_k, cache_v, c_ctx, w_ada, b_ada, norm1_g, w_in, conv_dw, conv_dw_b, conv_ln_g, conv_ln_b, w_conv_out, attn_sink, w_attn_out, w_out, norm2_g, w_router, b_router, w_exp_gate, w_exp_up, w_exp_down, final_norm_g):
    x = jnp.concatenate([x_prompt.reshape(N_CTX, D), x_sample.reshape(N_LAT, D)], axis=0)
    cond = jnp.concatenate([c, c_ctx[None, :], jnp.zeros((MOD_ROWS - N_LAT_SEQ - 1, D), F32)], axis=0)
    mod = _ada(cond, w_ada[0], b_ada[0]).reshape(MOD_ROWS, N_MOD, D)

    u = _inproj(x, mod, norm1_g, w_in[0].astype(BF16))
    zc = _conv(u, conv_dw[0], conv_dw_b, conv_ln_g, conv_ln_b)

    sink = attn_sink[0]
    cos, sin = _rope_tables()
    ck = cache_k[:, 0].reshape(N_LAT_SEQ, PAST, N_KV * HD)
    cv = cache_v[:, 0].reshape(N_LAT_SEQ, PAST, N_KV * HD)
    att = jnp.concatenate([_ctx_attn(u, sink), _lat_attn(u, sink, ck, cv, cos, sin)], axis=0)

    mixed = _mix(zc, att, u, w_conv_out[0].astype(BF16), w_attn_out[0].astype(BF16))
    x1, h2, logits_t = _outproj(mixed, x, mod, norm2_g, w_out[0].astype(BF16),
                                w_router[0].T, b_router[0].reshape(N_EXP, 1))

    idx, lo, gw = _select(logits_t)
    rows = (idx + (jnp.arange(N_SETS, dtype=jnp.int32) * N_CTX)[:, None, None])
    rows = rows.transpose(1, 0, 2).reshape(N_EXP * MOE_ROWS)
    ye = _moe(rows, h2, w_exp_gate[0], w_exp_up[0], w_exp_down[0])

    y = _combine(idx.reshape(-1), lo.reshape(-1), ye.reshape(N_EXP * MOE_ROWS, D), x1,
                 gw.transpose(0, 2, 1), mod, final_norm_g.reshape(1, D))

    y_prompt = y[:N_CTX].reshape(N_CTX_SEQ, CTX_LEN, D)
    y_sample = y[N_CTX:].reshape(N_LAT_SEQ, LAT_LEN, D)
    state_k = u[:N_CTX, COL_K:COL_V].reshape(N_CTX_SEQ, 1, CTX_LEN, N_KV, HD)
    state_v = u[:N_CTX, COL_V:COL_GA].reshape(N_CTX_SEQ, 1, CTX_LEN, N_KV, HD)
    return (y_prompt, y_sample, state_k, state_v)
```

```python
import functools

import jax
import jax.numpy as jnp
from jax import lax
from jax.experimental import pallas as pl
from jax.experimental.pallas import tpu as pltpu

D = 2048
N_CTX_SEQ, CTX_LEN = 16, 256
N_LAT_SEQ, LAT_LEN = 4, 1024
N_CTX = N_CTX_SEQ * CTX_LEN
N_LAT = N_LAT_SEQ * LAT_LEN
N_TOK = N_CTX + N_LAT
PAST = 512
GRID_W = 64
N_HEADS, N_KV, HD = 16, 4, 128
Q_GROUP = N_HEADS // N_KV
WINDOW = 128
ABLK = 128
CONV_W = 1024
CONV_K = 31
CONV_PAD = CONV_K // 2
N_EXP = 16
D_EXP = 1024
CAP = 2 * N_CTX // N_EXP
N_SETS = 2
ROPE_BASE = 10000.0
EPS = 1e-6
N_MOD = 6
NEG = -1e30
IN_COLS = 2 * CONV_W + N_HEADS * HD + 2 * N_KV * HD + 2 * D
COL_Q = 2 * CONV_W
COL_K = COL_Q + N_HEADS * HD
COL_V = COL_K + N_KV * HD
COL_GA = COL_V + N_KV * HD
COL_GB = COL_GA + D
MOD_ROWS = 8
CTX_MOD_ROW = N_LAT_SEQ

VMEM_LIMIT = 56 * 1024 * 1024
F32 = jnp.float32
BF16 = jnp.bfloat16


def _params(sem, vmem=VMEM_LIMIT):
    return pltpu.CompilerParams(dimension_semantics=sem, vmem_limit_bytes=vmem)


def _mod_row(tok_block, block_tokens):
    ctx_blocks = N_CTX // block_tokens
    per_seq = LAT_LEN // block_tokens
    return jnp.where(tok_block < ctx_blocks, CTX_MOD_ROW, (tok_block - ctx_blocks) // per_seq)


def _silu(x):
    return x * jax.nn.sigmoid(x)


def _rms(x, g):
    return x * lax.rsqrt(jnp.mean(x * x, axis=-1, keepdims=True) + EPS) * g


def _ada_kernel(cond_ref, w_ref, b_ref, o_ref):
    a = _silu(cond_ref[...]).astype(BF16)
    o_ref[...] = jnp.dot(a, w_ref[...].astype(BF16), preferred_element_type=F32) + b_ref[...]


def _ada(cond, w_ada, b_ada):
    tn = 1024
    n = N_MOD * D
    return pl.pallas_call(
        _ada_kernel,
        grid=(n // tn,),
        in_specs=[pl.BlockSpec((MOD_ROWS, D), lambda j: (0, 0)),
                  pl.BlockSpec((D, tn), lambda j: (0, j)),
                  pl.BlockSpec((1, tn), lambda j: (0, j))],
        out_specs=pl.BlockSpec((MOD_ROWS, tn), lambda j: (0, j)),
        out_shape=jax.ShapeDtypeStruct((MOD_ROWS, n), F32),
        compiler_params=_params(("arbitrary",)),
        name="ada",
    )(cond, w_ada, b_ada.reshape(1, n))


def _inproj_kernel(x_ref, mod_ref, g_ref, w_ref, o_ref, h_ref):
    @pl.when(pl.program_id(1) == 0)
    def _():
        m = mod_ref[0]
        h = _rms(x_ref[...], g_ref[...]) * (1.0 + m[1:2, :]) + m[0:1, :]
        h_ref[...] = h.astype(BF16)

    o_ref[...] = jnp.dot(h_ref[...], w_ref[...], preferred_element_type=F32)


def _inproj(x, mod, n1g, w_in_bf):
    tm, tn = 1024, 1024
    return pl.pallas_call(
        _inproj_kernel,
        grid=(N_TOK // tm, IN_COLS // tn),
        in_specs=[pl.BlockSpec((tm, D), lambda i, j: (i, 0)),
                  pl.BlockSpec((1, N_MOD, D), lambda i, j: (_mod_row(i, tm), 0, 0)),
                  pl.BlockSpec((1, D), lambda i, j: (0, 0)),
                  pl.BlockSpec((D, tn), lambda i, j: (0, j))],
        out_specs=pl.BlockSpec((tm, tn), lambda i, j: (i, j)),
        out_shape=jax.ShapeDtypeStruct((N_TOK, IN_COLS), F32),
        scratch_shapes=[pltpu.VMEM((tm, D), BF16)],
        compiler_params=_params(("arbitrary", "arbitrary")),
        name="inproj",
    )(x, mod, n1g, w_in_bf)


CONV_T = 256
CONV_HALO = 16
CONV_ROWS = 32


def _conv_kernel(main_ref, prev_ref, next_ref, w_ref, b_ref, lg_ref, lb_ref, o_ref, z_ref):
    i = pl.program_id(0)
    ctx_blocks = N_CTX // CONV_T
    per_seq = LAT_LEN // CONV_T
    pos = (i - ctx_blocks) % per_seq
    is_lat = i >= ctx_blocks
    has_prev = jnp.logical_and(is_lat, pos > 0)
    has_next = jnp.logical_and(is_lat, pos < per_seq - 1)

    def glu(ref):
        v = ref[...]
        return v[:, :CONV_W] * jax.nn.sigmoid(v[:, CONV_W:])

    z_ref[0:CONV_HALO, :] = jnp.where(has_prev, glu(prev_ref), 0.0)
    z_ref[CONV_HALO:CONV_HALO + CONV_T, :] = glu(main_ref)
    z_ref[CONV_HALO + CONV_T:, :] = jnp.where(has_next, glu(next_ref), 0.0)

    for r in range(CONV_T // CONV_ROWS):
        base = CONV_HALO - CONV_PAD + r * CONV_ROWS
        acc = jnp.zeros((CONV_ROWS, CONV_W), F32)
        for j in range(CONV_K):
            acc = acc + z_ref[base + j:base + j + CONV_ROWS, :] * w_ref[j:j + 1, :]
        acc = acc + b_ref[...]
        mu = jnp.mean(acc, axis=-1, keepdims=True)
        cen = acc - mu
        var = jnp.mean(cen * cen, axis=-1, keepdims=True)
        y = cen * lax.rsqrt(var + EPS) * lg_ref[...] + lb_ref[...]
        o_ref[r * CONV_ROWS:(r + 1) * CONV_ROWS, :] = _silu(y).astype(BF16)


def _conv(u, w_dw, b_dw, ln_g, ln_b):
    nblk = N_TOK // CONV_T
    hpb = CONV_T // CONV_HALO
    last = N_TOK // CONV_HALO - 1
    row = lambda i: (0, 0)
    return pl.pallas_call(
        _conv_kernel,
        grid=(nblk,),
        in_specs=[pl.BlockSpec((CONV_T, 2 * CONV_W), lambda i: (i, 0)),
                  pl.BlockSpec((CONV_HALO, 2 * CONV_W), lambda i: (jnp.maximum(i * hpb - 1, 0), 0)),
                  pl.BlockSpec((CONV_HALO, 2 * CONV_W), lambda i: (jnp.minimum((i + 1) * hpb, last), 0)),
                  pl.BlockSpec((CONV_K, CONV_W), row),
                  pl.BlockSpec((1, CONV_W), row),
                  pl.BlockSpec((1, CONV_W), row),
                  pl.BlockSpec((1, CONV_W), row)],
        out_specs=pl.BlockSpec((CONV_T, CONV_W), lambda i: (i, 0)),
        out_shape=jax.ShapeDtypeStruct((N_TOK, CONV_W), BF16),
        scratch_shapes=[pltpu.VMEM((CONV_T + 2 * CONV_HALO, CONV_W), F32)],
        compiler_params=_params(("arbitrary",)),
        name="conv",
    )(u, u, u, w_dw, b_dw, ln_g, ln_b)


def _stack_heads(q):
    return jnp.concatenate([q[:, g * HD:(g + 1) * HD] for g in range(Q_GROUP)], axis=0)


def _unstack_heads(o, rows):
    return jnp.concatenate([o[g * rows:(g + 1) * rows, :] for g in range(Q_GROUP)], axis=1)


def _sink_column(sink_ref, kv, rows):
    rid = lax.broadcasted_iota(jnp.int32, (Q_GROUP * rows, 1), 0)
    col = jnp.full((Q_GROUP * rows, 1), sink_ref[kv * Q_GROUP], F32)
    for g in range(1, Q_GROUP):
        col = jnp.where(rid >= g * rows, sink_ref[kv * Q_GROUP + g], col)
    return col


def _softmax_av(s, sk, vals):
    m = jnp.maximum(jnp.max(s, axis=-1, keepdims=True), sk)
    p = jnp.exp(s - m)
    den = jnp.sum(p, axis=-1, keepdims=True) + jnp.exp(sk - m)
    o = jnp.dot(p.astype(BF16), vals, preferred_element_type=F32)
    return o / den


def _ctx_attn_kernel(sink_ref, q_ref, k_ref, v_ref, o_ref):
    kv = pl.program_id(1)
    q = _stack_heads(q_ref[...]).astype(BF16)
    k = k_ref[...].astype(BF16)
    s = lax.dot_general(q, k, (((1,), (1,)), ((), ())), preferred_element_type=F32) * (HD ** -0.5)
    o = _softmax_av(s, _sink_column(sink_ref, kv, CTX_LEN), v_ref[...].astype(BF16))
    o_ref[...] = _unstack_heads(o, CTX_LEN).astype(BF16)


def _ctx_attn(u, sink):
    qw = Q_GROUP * HD
    return pl.pallas_call(
        _ctx_attn_kernel,
        grid_spec=pltpu.PrefetchScalarGridSpec(
            num_scalar_prefetch=1,
            grid=(N_CTX_SEQ, N_KV),
            in_specs=[pl.BlockSpec((CTX_LEN, qw), lambda b, kv, s: (b, COL_Q // qw + kv)),
                      pl.BlockSpec((CTX_LEN, HD), lambda b, kv, s: (b, COL_K // HD + kv)),
                      pl.BlockSpec((CTX_LEN, HD), lambda b, kv, s: (b, COL_V // HD + kv))],
            out_specs=pl.BlockSpec((CTX_LEN, qw), lambda b, kv, s: (b, kv))),
        out_shape=jax.ShapeDtypeStruct((N_CTX, N_HEADS * HD), BF16),
        compiler_params=_params(("arbitrary", "arbitrary")),
        name="ctx_attn",
    )(sink, u, u, u)


def _rope(x, cos, sin):
    lane = lax.broadcasted_iota(jnp.int32, x.shape, 1)
    fwd = pltpu.roll(x, HD - 32, axis=1)
    bwd = pltpu.roll(x, 32, axis=1)
    partner = jnp.where((lane & 32) == 0, fwd, bwd)
    return x * cos + partner * sin


def _lat_attn_kernel(sink_ref, q_ref, kp_ref, kc_ref, kn_ref, vp_ref, vc_ref, vn_ref,
                     ck_ref, cv_ref, cos_ref, sin_ref, o_ref):
    kv = pl.program_id(1)
    j = pl.program_id(2)
    nblk = LAT_LEN // ABLK
    jp = jnp.maximum(j - 1, 0)
    jn = jnp.minimum(j + 1, nblk - 1)

    def table(ref, blk):
        return ref[pl.ds(pl.multiple_of(blk * ABLK, ABLK), ABLK), :]

    cos_c, sin_c = table(cos_ref, j), table(sin_ref, j)
    q = q_ref[...]
    qs = jnp.concatenate(
        [_rope(q[:, g * HD:(g + 1) * HD], cos_c, sin_c) for g in range(Q_GROUP)], axis=0).astype(BF16)
    k_loc = jnp.concatenate(
        [_rope(kp_ref[...], table(cos_ref, jp), table(sin_ref, jp)),
         _rope(kc_ref[...], cos_c, sin_c),
         _rope(kn_ref[...], table(cos_ref, jn), table(sin_ref, jn))], axis=0).astype(BF16)
    keys = jnp.concatenate([k_loc, ck_ref[...].astype(BF16)], axis=0)
    vals = jnp.concatenate([vp_ref[...], vc_ref[...], vn_ref[...], cv_ref[...]], axis=0).astype(BF16)

    s = lax.dot_general(qs, keys, (((1,), (1,)), ((), ())), preferred_element_type=F32) * (HD ** -0.5)
    span = ABLK + 2 * WINDOW
    rows = Q_GROUP * ABLK
    r = lax.broadcasted_iota(jnp.int32, (rows, span + PAST), 0) & (ABLK - 1)
    c = lax.broadcasted_iota(jnp.int32, (rows, span + PAST), 1)
    in_band = jnp.logical_and(c >= r, c <= r + 2 * WINDOW)
    lo = jnp.where(j > 0, 0, WINDOW)
    hi = jnp.where(j < nblk - 1, span, WINDOW + ABLK)
    in_seq = jnp.logical_and(c >= lo, c < hi)
    valid = jnp.logical_or(c >= span, jnp.logical_and(in_band, in_seq))
    s = jnp.where(valid, s, NEG)
    o = _softmax_av(s, _sink_column(sink_ref, kv, ABLK), vals)
    o_ref[...] = _unstack_heads(o, ABLK).astype(BF16)


def _lat_attn(u, sink, ck, cv, cos, sin):
    qw = Q_GROUP * HD
    nblk = LAT_LEN // ABLK
    base = N_CTX // ABLK

    def kv_spec(col0, shift):
        def imap(b, kv, j, s):
            jj = jnp.clip(j + shift, 0, nblk - 1)
            return (base + b * nblk + jj, col0 // HD + kv)
        return pl.BlockSpec((ABLK, HD), imap)

    cache_spec = pl.BlockSpec((None, PAST, HD), lambda b, kv, j, s: (b, 0, kv))
    table_spec = pl.BlockSpec((LAT_LEN, HD), lambda b, kv, j, s: (0, 0))
    return pl.pallas_call(
        _lat_attn_kernel,
        grid_spec=pltpu.PrefetchScalarGridSpec(
            num_scalar_prefetch=1,
            grid=(N_LAT_SEQ, N_KV, nblk),
            in_specs=[pl.BlockSpec((ABLK, qw), lambda b, kv, j, s: (base + b * nblk + j, COL_Q // qw + kv)),
                      kv_spec(COL_K, -1), kv_spec(COL_K, 0), kv_spec(COL_K, 1),
                      kv_spec(COL_V, -1), kv_spec(COL_V, 0), kv_spec(COL_V, 1),
                      cache_spec, cache_spec, table_spec, table_spec],
            out_specs=pl.BlockSpec((ABLK, qw), lambda b, kv, j, s: (b * nblk + j, kv))),
        out_shape=jax.ShapeDtypeStruct((N_LAT, N_HEADS * HD), BF16),
        compiler_params=_params(("arbitrary", "arbitrary", "arbitrary")),
        name="lat_attn",
    )(sink, u, u, u, u, u, u, u, ck, cv, cos, sin)


def _rope_tables():
    n = jnp.arange(LAT_LEN)
    row = (n // GRID_W).astype(F32)
    col = (n % GRID_W).astype(F32)
    quarter = HD // 4
    inv = ROPE_BASE ** (-jnp.arange(quarter, dtype=F32) / quarter)
    ang_r = row[:, None] * inv[None, :]
    ang_c = col[:, None] * inv[None, :]
    cos = jnp.concatenate([jnp.cos(ang_r)] * 2 + [jnp.cos(ang_c)] * 2, axis=-1)
    sin = jnp.concatenate([-jnp.sin(ang_r), jnp.sin(ang_r), -jnp.sin(ang_c), jnp.sin(ang_c)], axis=-1)
    return cos, sin


def _mix_kernel(z_ref, a_ref, ga_ref, gb_ref, wc_ref, wa_ref, o_ref):
    conv_out = jnp.dot(z_ref[...], wc_ref[...], preferred_element_type=F32)
    attn_out = jnp.dot(a_ref[...], wa_ref[...], preferred_element_type=F32)
    mixed = jax.nn.sigmoid(ga_ref[...]) * conv_out + jax.nn.sigmoid(gb_ref[...]) * attn_out
    o_ref[...] = mixed.astype(BF16)


def _mix(zc, att, u, wc_bf, wa_bf):
    tm, tn = 512, 1024
    return pl.pallas_call(
        _mix_kernel,
        grid=(D // tn, N_TOK // tm),
        in_specs=[pl.BlockSpec((tm, CONV_W), lambda j, i: (i, 0)),
                  pl.BlockSpec((tm, N_HEADS * HD), lambda j, i: (i, 0)),
                  pl.BlockSpec((tm, tn), lambda j, i: (i, COL_GA // tn + j)),
                  pl.BlockSpec((tm, tn), lambda j, i: (i, COL_GB // tn + j)),
                  pl.BlockSpec((CONV_W, tn), lambda j, i: (0, j)),
                  pl.BlockSpec((N_HEADS * HD, tn), lambda j, i: (0, j))],
        out_specs=pl.BlockSpec((tm, tn), lambda j, i: (i, j)),
        out_shape=jax.ShapeDtypeStruct((N_TOK, D), BF16),
        compiler_params=_params(("arbitrary", "arbitrary")),
        name="mix",
    )(zc, att, u, u, wc_bf, wa_bf)


def _split_bf16(x):
    hi = x.astype(BF16)
    lo = (x - hi.astype(F32)).astype(BF16)
    return hi, lo


def _outproj_kernel(mx_ref, x_ref, mod_ref, g_ref, w_ref, wr_ref, br_ref, x1_ref, h2_ref, lg_ref):
    m = mod_ref[0]
    x1 = x_ref[...] + m[2:3, :] * jnp.dot(mx_ref[...], w_ref[...], preferred_element_type=F32)
    x1_ref[...] = x1
    h2 = _rms(x1, g_ref[...]) * (1.0 + m[4:5, :]) + m[3:4, :]
    h2_ref[...] = h2
    h_hi, h_lo = _split_bf16(h2)
    w_hi, w_lo = _split_bf16(wr_ref[...])
    nt = (((1,), (1,)), ((), ()))
    lg = (lax.dot_general(w_hi, h_hi, nt, preferred_element_type=F32)
          + lax.dot_general(w_lo, h_hi, nt, preferred_element_type=F32)
          + lax.dot_general(w_hi, h_lo, nt, preferred_element_type=F32))
    lg_ref[...] = lg + br_ref[...]


def _outproj(mixed, x, mod, n2g, w_out_bf, w_router_t, b_router):
    tm = 256
    return pl.pallas_call(
        _outproj_kernel,
        grid=(N_TOK // tm,),
        in_specs=[pl.BlockSpec((tm, D), lambda i: (i, 0)),
                  pl.BlockSpec((tm, D), lambda i: (i, 0)),
                  pl.BlockSpec((1, N_MOD, D), lambda i: (_mod_row(i, tm), 0, 0)),
                  pl.BlockSpec((1, D), lambda i: (0, 0)),
                  pl.BlockSpec((D, D), lambda i: (0, 0)),
                  pl.BlockSpec((N_EXP, D), lambda i: (0, 0)),
                  pl.BlockSpec((N_EXP, 1), lambda i: (0, 0))],
        out_specs=[pl.BlockSpec((tm, D), lambda i: (i, 0)),
                   pl.BlockSpec((tm, D), lambda i: (i, 0)),
                   pl.BlockSpec((N_EXP, tm), lambda i: (0, i))],
        out_shape=[jax.ShapeDtypeStruct((N_TOK, D), F32),
                   jax.ShapeDtypeStruct((N_TOK, D), F32),
                   jax.ShapeDtypeStruct((N_EXP, N_TOK), F32)],
        compiler_params=_params(("arbitrary",)),
        name="outproj",
    )(mixed, x, mod, n2g, w_out_bf, w_router_t, b_router)


SEL_CHUNK = 128


def _prefix_sum(flags, out_ref):
    n = flags.shape[1]
    ri = lax.broadcasted_iota(jnp.int32, (SEL_CHUNK, SEL_CHUNK), 0)
    ci = lax.broadcasted_iota(jnp.int32, (SEL_CHUNK, SEL_CHUNK), 1)
    tri = jnp.where(ri <= ci, 1.0, 0.0).astype(BF16)
    run = jnp.zeros((N_EXP, 1), F32)
    for ch in range(n // SEL_CHUNK):
        blk = flags[:, ch * SEL_CHUNK:(ch + 1) * SEL_CHUNK].astype(BF16)
        within = jnp.dot(blk, tri, preferred_element_type=F32) + run
        out_ref[:, ch * SEL_CHUNK:(ch + 1) * SEL_CHUNK] = within
        run = within[:, SEL_CHUNK - 1:SEL_CHUNK]


def _select_kernel(lg_ref, idx_ref, gate_ref, cs_ref):
    n = lg_ref.shape[1]
    lg = lg_ref[...]
    ex = jnp.exp(lg - jnp.max(lg, axis=0, keepdims=True))
    aff = ex / jnp.sum(ex, axis=0, keepdims=True)

    def bit_step(b, cur):
        cand = cur | (1 << (30 - b))
        cnt = jnp.sum(jnp.where(aff >= pltpu.bitcast(cand, F32), 1.0, 0.0), axis=1, keepdims=True)
        return jnp.where(cnt >= CAP, cand, cur)

    thr = pltpu.bitcast(lax.fori_loop(0, 31, bit_step, jnp.zeros((N_EXP, 1), jnp.int32)), F32)
    above = aff > thr
    tied = aff == thr
    room = CAP - jnp.sum(jnp.where(above, 1.0, 0.0), axis=1, keepdims=True)
    _prefix_sum(jnp.where(tied, 1.0, 0.0), cs_ref)
    sel = jnp.logical_or(above, jnp.logical_and(tied, cs_ref[...] <= room))
    gsel = jnp.where(sel, aff, 0.0)
    _prefix_sum(jnp.where(sel, 1.0, 0.0), cs_ref)

    slot = lax.broadcasted_iota(jnp.int32, (CAP, n), 0).astype(F32)
    ones = jnp.ones((8, n), BF16)
    for e in range(N_EXP):
        pre = cs_ref[e:e + 1, :]
        le = jnp.where(pre <= slot, 1.0, 0.0).astype(BF16)
        cnt = lax.dot_general(ones, le, (((1,), (1,)), ((), ())), preferred_element_type=F32)
        idx_ref[0, e:e + 1, :] = cnt[0:1, :].astype(jnp.int32)
        gate_ref[0, e] = jnp.sum(jnp.where(pre == slot + 1.0, gsel[e:e + 1, :], 0.0),
                                 axis=1, keepdims=True)


def _select(logits_t):
    return pl.pallas_call(
        _select_kernel,
        grid=(N_SETS,),
        in_specs=[pl.BlockSpec((N_EXP, N_CTX), lambda s: (0, s))],
        out_specs=[pl.BlockSpec((1, N_EXP, CAP), lambda s: (s, 0, 0)),
                   pl.BlockSpec((1, N_EXP, CAP, 1), lambda s: (s, 0, 0, 0))],
        out_shape=[jax.ShapeDtypeStruct((N_SETS, N_EXP, CAP), jnp.int32),
                   jax.ShapeDtypeStruct((N_SETS, N_EXP, CAP, 1), F32)],
        scratch_shapes=[pltpu.VMEM((N_EXP, N_CTX), F32)],
        compiler_params=_params(("arbitrary",)),
        name="select",
    )(logits_t)


MOE_ROWS = N_SETS * CAP
MOE_TF = 256


MOE_NF = D_EXP // MOE_TF
MOE_XQ = MOE_ROWS // MOE_NF
MOE_YH = MOE_ROWS // (MOE_NF // 2)


def _moe_kernel(rows_ref, h_hbm, gate_ref, wg_ref, wu_ref, wd_ref, y_in_hbm, y_hbm,
                xf_ref, xb_ref, acc_ref, rmw_ref, sem_x, sem_put, sem_get):
    del y_in_hbm
    e = pl.program_id(0)
    f = pl.program_id(1)
    e_prev = jnp.maximum(e - 1, 0)
    e_next = jnp.minimum(e + 1, N_EXP - 1)

    def x_copy(ex, k):
        tok = rows_ref[ex * MOE_ROWS + k]
        return pltpu.make_async_copy(h_hbm.at[pl.ds(tok, 1), :], xf_ref.at[pl.ds(k, 1), :], sem_x)

    def put_copy(ex, k):
        tok = rows_ref[ex * MOE_ROWS + k]
        return pltpu.make_async_copy(rmw_ref.at[pl.ds(k, 1), :], y_hbm.at[pl.ds(tok, 1), :], sem_put)

    def get_copy(ex, k):
        tok = rows_ref[ex * MOE_ROWS + k]
        return pltpu.make_async_copy(y_hbm.at[pl.ds(tok, 1), :], rmw_ref.at[pl.ds(k, 1), :], sem_get)

    def wait_rows(buf_ref, sem, to_hbm=False):
        whole = y_hbm.at[pl.ds(0, MOE_ROWS), :]
        src, dst = (buf_ref, whole) if to_hbm else (whole, buf_ref)
        pltpu.make_async_copy(src, dst, sem).wait()

    def ffn_step():
        wg = wg_ref[...].astype(BF16)
        wu = wu_ref[...].astype(BF16)
        wd = wd_ref[...].astype(BF16)
        for s in range(N_SETS):
            rows = slice(s * CAP, (s + 1) * CAP)
            x = xb_ref[rows, :]
            g = jnp.dot(x, wg, preferred_element_type=F32)
            up = jnp.dot(x, wu, preferred_element_type=F32)
            hid = (_silu(g) * up).astype(BF16)
            acc_ref[rows, :] += jnp.dot(hid, wd, preferred_element_type=F32)

    @pl.when(jnp.logical_and(e == 0, f == 0))
    def _():
        rmw_ref[...] = jnp.zeros(rmw_ref.shape, F32)

        def issue(k, c):
            x_copy(0, k).start()
            return c
        lax.fori_loop(0, MOE_ROWS, issue, 0)

    @pl.when(f == 0)
    def _():
        wait_rows(xf_ref, sem_x)
        xb_ref[...] = xf_ref[...].astype(BF16)
        acc_ref[...] = jnp.zeros(acc_ref.shape, F32)

    @pl.when(f == MOE_NF // 2)
    def _():
        wait_rows(rmw_ref, sem_put, to_hbm=True)

    @pl.when(f < MOE_NF // 2)
    def _():
        for k in range(MOE_XQ):
            x_copy(e_next, f * MOE_XQ + k).start()
        for k in range(MOE_YH):
            put_copy(e_prev, f * MOE_YH + k).start()
        ffn_step()

    @pl.when(f >= MOE_NF // 2)
    def _():
        for k in range(MOE_XQ):
            x_copy(e_next, f * MOE_XQ + k).start()
        for k in range(MOE_YH):
            get_copy(e, (f - MOE_NF // 2) * MOE_YH + k).start()
        ffn_step()

    @pl.when(f == MOE_NF - 1)
    def _():
        wait_rows(rmw_ref, sem_get)
        rmw_ref[...] = rmw_ref[...] + gate_ref[...] * acc_ref[...]

    @pl.when(jnp.logical_and(e == N_EXP - 1, f == MOE_NF - 1))
    def _():
        def issue(k, c):
            put_copy(N_EXP - 1, k).start()
            return c
        lax.fori_loop(0, MOE_ROWS, issue, 0)
        wait_rows(rmw_ref, sem_put, to_hbm=True)
        wait_rows(xf_ref, sem_x)


def _moe(rows, h2, gates, w_g, w_u, w_d, y0):
    return pl.pallas_call(
        _moe_kernel,
        grid_spec=pltpu.PrefetchScalarGridSpec(
            num_scalar_prefetch=1,
            grid=(N_EXP, MOE_NF),
            in_specs=[pl.BlockSpec(memory_space=pl.ANY),
                      pl.BlockSpec((None, MOE_ROWS, 1), lambda e, f, s: (e, 0, 0)),
                      pl.BlockSpec((None, D, MOE_TF), lambda e, f, s: (e, 0, f)),
                      pl.BlockSpec((None, D, MOE_TF), lambda e, f, s: (e, 0, f)),
                      pl.BlockSpec((None, MOE_TF, D), lambda e, f, s: (e, f, 0)),
                      pl.BlockSpec(memory_space=pl.ANY)],
            out_specs=pl.BlockSpec(memory_space=pl.ANY),
            scratch_shapes=[pltpu.VMEM((MOE_ROWS, D), F32),
                            pltpu.VMEM((MOE_ROWS, D), BF16),
                            pltpu.VMEM((MOE_ROWS, D), F32),
                            pltpu.VMEM((MOE_ROWS, D), F32),
                            pltpu.SemaphoreType.DMA(()),
                            pltpu.SemaphoreType.DMA(()),
                            pltpu.SemaphoreType.DMA(())]),
        out_shape=jax.ShapeDtypeStruct((N_TOK, D), F32),
        input_output_aliases={6: 0},
        compiler_params=_params(("arbitrary", "arbitrary")),
        name="moe",
    )(rows, h2, gates, w_g, w_u, w_d, y0)


def _final_kernel(x1_ref, y_ref, mod_ref, g_ref, o_ref):
    o_ref[...] = _rms(x1_ref[...] + mod_ref[0][5:6, :] * y_ref[...], g_ref[...])


def _final(x1, y, mod, fng):
    tm = 512
    return pl.pallas_call(
        _final_kernel,
        grid=(N_TOK // tm,),
        in_specs=[pl.BlockSpec((tm, D), lambda i: (i, 0)),
                  pl.BlockSpec((tm, D), lambda i: (i, 0)),
                  pl.BlockSpec((1, N_MOD, D), lambda i: (_mod_row(i, tm), 0, 0)),
                  pl.BlockSpec((1, D), lambda i: (0, 0))],
        out_specs=pl.BlockSpec((tm, D), lambda i: (i, 0)),
        out_shape=jax.ShapeDtypeStruct((N_TOK, D), F32),
        compiler_params=_params(("arbitrary",)),
        name="final",
    )(x1, y, mod, fng)


def kernel(x_prompt, x_sample, c, cache_k, cache_v, c_ctx, w_ada, b_ada, norm1_g, w_in, conv_dw, conv_dw_b, conv_ln_g, conv_ln_b, w_conv_out, attn_sink, w_attn_out, w_out, norm2_g, w_router, b_router, w_exp_gate, w_exp_up, w_exp_down, final_norm_g):
    x = jnp.concatenate([x_prompt.reshape(N_CTX, D), x_sample.reshape(N_LAT, D)], axis=0)
    cond = jnp.concatenate([c, c_ctx[None, :], jnp.zeros((MOD_ROWS - N_LAT_SEQ - 1, D), F32)], axis=0)
    mod = _ada(cond, w_ada[0], b_ada[0]).reshape(MOD_ROWS, N_MOD, D)

    u = _inproj(x, mod, norm1_g, w_in[0].astype(BF16))
    zc = _conv(u, conv_dw[0], conv_dw_b, conv_ln_g, conv_ln_b)

    sink = attn_sink[0]
    cos, sin = _rope_tables()
    ck = cache_k[:, 0].reshape(N_LAT_SEQ, PAST, N_KV * HD)
    cv = cache_v[:, 0].reshape(N_LAT_SEQ, PAST, N_KV * HD)
    att = jnp.concatenate([_ctx_attn(u, sink), _lat_attn(u, sink, ck, cv, cos, sin)], axis=0)

    mixed = _mix(zc, att, u, w_conv_out[0].astype(BF16), w_attn_out[0].astype(BF16))
    x1, h2, logits_t = _outproj(mixed, x, mod, norm2_g, w_out[0].astype(BF16),
                                w_router[0].T, b_router[0].reshape(N_EXP, 1))

    idx, gates = _select(logits_t)
    rows = (idx + (jnp.arange(N_SETS, dtype=jnp.int32) * N_CTX)[:, None, None])
    rows = rows.transpose(1, 0, 2).reshape(N_EXP * MOE_ROWS)
    gates = gates.transpose(1, 0, 2, 3).reshape(N_EXP, MOE_ROWS, 1)
    ffn = _moe(rows, h2, gates, w_exp_gate[0], w_exp_up[0], w_exp_down[0], jnp.zeros((N_TOK, D), F32))

    y = _final(x1, ffn, mod, final_norm_g.reshape(1, D))

    y_prompt = y[:N_CTX].reshape(N_CTX_SEQ, CTX_LEN, D)
    y_sample = y[N_CTX:].reshape(N_LAT_SEQ, LAT_LEN, D)
    state_k = u[:N_CTX, COL_K:COL_V].reshape(N_CTX_SEQ, 1, CTX_LEN, N_KV, HD)
    state_v = u[:N_CTX, COL_V:COL_GA].reshape(N_CTX_SEQ, 1, CTX_LEN, N_KV, HD)
    return (y_prompt, y_sample, state_k, state_v)
```

```python
import functools

import jax
import jax.numpy as jnp
from jax import lax
from jax.experimental import pallas as pl
from jax.experimental.pallas import tpu as pltpu

D = 2048
N_CTX_SEQ, CTX_LEN = 16, 256
N_LAT_SEQ, LAT_LEN = 4, 1024
N_CTX = N_CTX_SEQ * CTX_LEN
N_LAT = N_LAT_SEQ * LAT_LEN
N_TOK = N_CTX + N_LAT
PAST = 512
GRID_W = 64
N_HEADS, N_KV, HD = 16, 4, 128
Q_GROUP = N_HEADS // N_KV
WINDOW = 128
ABLK = 128
CONV_W = 1024
CONV_K = 31
CONV_PAD = CONV_K // 2
N_EXP = 16
D_EXP = 1024
CAP = 2 * N_CTX // N_EXP
N_SETS = 2
ROPE_BASE = 10000.0
EPS = 1e-6
N_MOD = 6
NEG = -1e30
IN_COLS = 2 * CONV_W + N_HEADS * HD + 2 * N_KV * HD + 2 * D
COL_Q = 2 * CONV_W
COL_K = COL_Q + N_HEADS * HD
COL_V = COL_K + N_KV * HD
COL_GA = COL_V + N_KV * HD
COL_GB = COL_GA + D
MOD_ROWS = 8
CTX_MOD_ROW = N_LAT_SEQ

VMEM_LIMIT = 58 * 1024 * 1024
F32 = jnp.float32
BF16 = jnp.bfloat16


def _params(sem, vmem=VMEM_LIMIT):
    return pltpu.CompilerParams(dimension_semantics=sem, vmem_limit_bytes=vmem)


def _mod_row(tok_block, block_tokens):
    ctx_blocks = N_CTX // block_tokens
    per_seq = LAT_LEN // block_tokens
    return jnp.where(tok_block < ctx_blocks, CTX_MOD_ROW, (tok_block - ctx_blocks) // per_seq)


def _silu(x):
    return x * jax.nn.sigmoid(x)


def _rms(x, g):
    return x * lax.rsqrt(jnp.mean(x * x, axis=-1, keepdims=True) + EPS) * g


def _ada_kernel(cond_ref, w_ref, b_ref, o_ref):
    a = _silu(cond_ref[...]).astype(BF16)
    o_ref[...] = jnp.dot(a, w_ref[...].astype(BF16), preferred_element_type=F32) + b_ref[...]


def _ada(cond, w_ada, b_ada):
    tn = 1024
    n = N_MOD * D
    return pl.pallas_call(
        _ada_kernel,
        grid=(n // tn,),
        in_specs=[pl.BlockSpec((MOD_ROWS, D), lambda j: (0, 0)),
                  pl.BlockSpec((D, tn), lambda j: (0, j)),
                  pl.BlockSpec((1, tn), lambda j: (0, j))],
        out_specs=pl.BlockSpec((MOD_ROWS, tn), lambda j: (0, j)),
        out_shape=jax.ShapeDtypeStruct((MOD_ROWS, n), F32),
        compiler_params=_params(("arbitrary",)),
        name="ada",
    )(cond, w_ada, b_ada.reshape(1, n))


def _set_specs(tm, buffers):
    nc = N_CTX // tm
    mode = pl.Buffered(buffers)
    ctx = pl.BlockSpec((tm, D), lambda i, *_: (jnp.minimum(i, nc - 1), 0), pipeline_mode=mode)
    lat = pl.BlockSpec((tm, D), lambda i, *_: (jnp.maximum(i - nc, 0), 0), pipeline_mode=mode)
    return ctx, lat


def _inproj_kernel(xc_ref, xl_ref, mod_ref, g_ref, w_ref, o_ref, h_ref):
    @pl.when(pl.program_id(1) == 0)
    def _():
        is_ctx = pl.program_id(0) < N_CTX // xc_ref.shape[0]
        x = jnp.where(is_ctx, xc_ref[...], xl_ref[...])
        m = mod_ref[0]
        h = _rms(x, g_ref[...]) * (1.0 + m[1:2, :]) + m[0:1, :]
        h_ref[...] = h.astype(BF16)

    o_ref[...] = jnp.dot(h_ref[...], w_ref[...].astype(BF16), preferred_element_type=F32)


def _inproj(x_ctx, x_lat, mod, n1g, w_in):
    tm, tn = 1024, 1024
    xc_spec, xl_spec = _set_specs(tm, 1)
    return pl.pallas_call(
        _inproj_kernel,
        grid=(N_TOK // tm, IN_COLS // tn),
        in_specs=[xc_spec, xl_spec,
                  pl.BlockSpec((1, N_MOD, D), lambda i, j: (_mod_row(i, tm), 0, 0)),
                  pl.BlockSpec((1, D), lambda i, j: (0, 0)),
                  pl.BlockSpec((D, tn), lambda i, j: (0, j))],
        out_specs=pl.BlockSpec((tm, tn), lambda i, j: (i, j)),
        out_shape=jax.ShapeDtypeStruct((N_TOK, IN_COLS), F32),
        scratch_shapes=[pltpu.VMEM((tm, D), BF16)],
        compiler_params=_params(("arbitrary", "arbitrary")),
        name="inproj",
    )(x_ctx, x_lat, mod, n1g, w_in)


CONV_T = 256
CONV_HALO = 16
CONV_ROWS = 32
SUBLANES = 8
CONV_ZS = CONV_T + (CONV_K // SUBLANES) * SUBLANES


def _conv_kernel(main_ref, prev_ref, next_ref, w_ref, b_ref, lg_ref, lb_ref, o_ref, z_ref, zs_ref):
    i = pl.program_id(0)
    ctx_blocks = N_CTX // CONV_T
    per_seq = LAT_LEN // CONV_T
    pos = (i - ctx_blocks) % per_seq
    is_lat = i >= ctx_blocks
    has_prev = jnp.logical_and(is_lat, pos > 0)
    has_next = jnp.logical_and(is_lat, pos < per_seq - 1)

    def glu(ref):
        v = ref[...]
        return v[:, :CONV_W] * jax.nn.sigmoid(v[:, CONV_W:])

    z_ref[0:CONV_HALO, :] = jnp.where(has_prev, glu(prev_ref), 0.0)
    z_ref[CONV_HALO:CONV_HALO + CONV_T, :] = glu(main_ref)
    z_ref[CONV_HALO + CONV_T:, :] = jnp.where(has_next, glu(next_ref), 0.0)

    z = z_ref[...]
    zs_ref[0] = z[:CONV_ZS]
    for s in range(1, SUBLANES):
        zs_ref[s] = pltpu.roll(z, z.shape[0] - s, axis=0)[:CONV_ZS]

    for r in range(CONV_T // CONV_ROWS):
        acc = jnp.zeros((CONV_ROWS, CONV_W), F32)
        for j in range(CONV_K):
            a, s = divmod(CONV_HALO - CONV_PAD + j, SUBLANES)
            q = r * CONV_ROWS + a * SUBLANES
            acc = acc + zs_ref[s, q:q + CONV_ROWS, :] * w_ref[j:j + 1, :]
        acc = acc + b_ref[...]
        mu = jnp.mean(acc, axis=-1, keepdims=True)
        cen = acc - mu
        var = jnp.mean(cen * cen, axis=-1, keepdims=True)
        y = cen * lax.rsqrt(var + EPS) * lg_ref[...] + lb_ref[...]
        o_ref[r * CONV_ROWS:(r + 1) * CONV_ROWS, :] = _silu(y).astype(BF16)


def _conv(u, w_dw, b_dw, ln_g, ln_b):
    nblk = N_TOK // CONV_T
    hpb = CONV_T // CONV_HALO
    last = N_TOK // CONV_HALO - 1
    row = lambda i: (0, 0)
    return pl.pallas_call(
        _conv_kernel,
        grid=(nblk,),
        in_specs=[pl.BlockSpec((CONV_T, 2 * CONV_W), lambda i: (i, 0)),
                  pl.BlockSpec((CONV_HALO, 2 * CONV_W), lambda i: (jnp.maximum(i * hpb - 1, 0), 0)),
                  pl.BlockSpec((CONV_HALO, 2 * CONV_W), lambda i: (jnp.minimum((i + 1) * hpb, last), 0)),
                  pl.BlockSpec((CONV_K, CONV_W), row),
                  pl.BlockSpec((1, CONV_W), row),
                  pl.BlockSpec((1, CONV_W), row),
                  pl.BlockSpec((1, CONV_W), row)],
        out_specs=pl.BlockSpec((CONV_T, CONV_W), lambda i: (i, 0)),
        out_shape=jax.ShapeDtypeStruct((N_TOK, CONV_W), BF16),
        scratch_shapes=[pltpu.VMEM((CONV_T + 2 * CONV_HALO, CONV_W), F32),
                        pltpu.VMEM((SUBLANES, CONV_ZS, CONV_W), F32)],
        compiler_params=_params(("arbitrary",)),
        name="conv",
    )(u, u, u, w_dw, b_dw, ln_g, ln_b)


def _stack_heads(q):
    return jnp.concatenate([q[:, g * HD:(g + 1) * HD] for g in range(Q_GROUP)], axis=0)


def _unstack_heads(o, rows):
    return jnp.concatenate([o[g * rows:(g + 1) * rows, :] for g in range(Q_GROUP)], axis=1)


LOG2E = 1.4426950408889634
Q_SCALE = HD ** -0.5 * LOG2E
QW = Q_GROUP * HD


def _sink_column(sink_ref, kv, rows):
    rid = lax.broadcasted_iota(jnp.int32, (Q_GROUP * rows, 1), 0)
    col = jnp.full((Q_GROUP * rows, 1), sink_ref[kv * Q_GROUP], F32)
    for g in range(1, Q_GROUP):
        col = jnp.where(rid >= g * rows, sink_ref[kv * Q_GROUP + g], col)
    return col * LOG2E


def _softmax_av(s, sk, vals):
    m = jnp.maximum(jnp.max(s, axis=-1, keepdims=True), sk)
    p = jnp.exp2(s - m)
    den = jnp.sum(p, axis=-1, keepdims=True) + jnp.exp2(sk - m)
    o = jnp.dot(p.astype(BF16), vals, preferred_element_type=F32)
    return o / den


_NT = (((1,), (1,)), ((), ()))


def _ctx_attn_kernel(sink_ref, q_ref, k_ref, v_ref, o_ref):
    for kv in range(N_KV):
        q = (_stack_heads(q_ref[:, kv * QW:(kv + 1) * QW]) * Q_SCALE).astype(BF16)
        k = k_ref[:, kv * HD:(kv + 1) * HD].astype(BF16)
        v = v_ref[:, kv * HD:(kv + 1) * HD].astype(BF16)
        s = lax.dot_general(q, k, _NT, preferred_element_type=F32)
        o = _softmax_av(s, _sink_column(sink_ref, kv, CTX_LEN), v)
        o_ref[:, kv * QW:(kv + 1) * QW] = _unstack_heads(o, CTX_LEN).astype(BF16)


def _ctx_attn(u, sink):
    qcols, kcols = N_HEADS * HD, N_KV * HD
    return pl.pallas_call(
        _ctx_attn_kernel,
        grid_spec=pltpu.PrefetchScalarGridSpec(
            num_scalar_prefetch=1,
            grid=(N_CTX_SEQ,),
            in_specs=[pl.BlockSpec((CTX_LEN, qcols), lambda b, s: (b, COL_Q // qcols)),
                      pl.BlockSpec((CTX_LEN, kcols), lambda b, s: (b, COL_K // kcols)),
                      pl.BlockSpec((CTX_LEN, kcols), lambda b, s: (b, COL_V // kcols))],
            out_specs=pl.BlockSpec((CTX_LEN, qcols), lambda b, s: (b, 0))),
        out_shape=jax.ShapeDtypeStruct((N_CTX, N_HEADS * HD), BF16),
        compiler_params=_params(("arbitrary",)),
        name="ctx_attn",
    )(sink, u, u, u)


def _rope(x, cos, sin):
    lane = lax.broadcasted_iota(jnp.int32, x.shape, 1)
    fwd = pltpu.roll(x, HD - 32, axis=1)
    bwd = pltpu.roll(x, 32, axis=1)
    partner = jnp.where((lane & 32) == 0, fwd, bwd)
    return x * cos + partner * sin


def _lat_attn_kernel(sink_ref, q_ref, k_ref, v_ref, ck_ref, cv_ref, cos_ref, sin_ref, o_ref):
    kv = pl.program_id(1)
    nblk = LAT_LEN // ABLK
    cos, sin = cos_ref[...], sin_ref[...]
    k_rot = _rope(k_ref[...], cos, sin).astype(BF16)
    v = v_ref[...].astype(BF16)
    ck = ck_ref[...].astype(BF16)
    cv = cv_ref[...].astype(BF16)
    sk = _sink_column(sink_ref, kv, ABLK)
    r = lax.broadcasted_iota(jnp.int32, (Q_GROUP * ABLK, ABLK), 0) & (ABLK - 1)
    c = lax.broadcasted_iota(jnp.int32, (Q_GROUP * ABLK, ABLK), 1)
    for j in range(nblk):
        rows = slice(j * ABLK, (j + 1) * ABLK)
        cos_j, sin_j = cos[rows], sin[rows]
        qs = jnp.concatenate(
            [_rope(q_ref[rows, g * HD:(g + 1) * HD], cos_j, sin_j) * Q_SCALE for g in range(Q_GROUP)],
            axis=0).astype(BF16)
        blocks = range(max(j - 1, 0), min(j + 2, nblk))
        band = slice(blocks[0] * ABLK, (blocks[-1] + 1) * ABLK)
        keys = jnp.concatenate([k_rot[band], ck], axis=0)
        vals = jnp.concatenate([v[band], cv], axis=0)
        s = lax.dot_general(qs, keys, _NT, preferred_element_type=F32)
        tiles = []
        for t, kb in enumerate(blocks):
            tile = s[:, t * ABLK:(t + 1) * ABLK]
            if kb < j:
                tile = jnp.where(c >= r, tile, NEG)
            elif kb > j:
                tile = jnp.where(c <= r, tile, NEG)
            tiles.append(tile)
        tiles.append(s[:, len(blocks) * ABLK:])
        o = _softmax_av(jnp.concatenate(tiles, axis=1), sk, vals)
        o_ref[rows, :] = _unstack_heads(o, ABLK).astype(BF16)


def _lat_attn(u, sink, ck, cv, cos, sin):
    base = N_CTX // LAT_LEN
    cache_spec = pl.BlockSpec((None, PAST, HD), lambda b, kv, s: (b, 0, kv))
    table_spec = pl.BlockSpec((LAT_LEN, HD), lambda b, kv, s: (0, 0))
    return pl.pallas_call(
        _lat_attn_kernel,
        grid_spec=pltpu.PrefetchScalarGridSpec(
            num_scalar_prefetch=1,
            grid=(N_LAT_SEQ, N_KV),
            in_specs=[pl.BlockSpec((LAT_LEN, QW), lambda b, kv, s: (base + b, COL_Q // QW + kv)),
                      pl.BlockSpec((LAT_LEN, HD), lambda b, kv, s: (base + b, COL_K // HD + kv)),
                      pl.BlockSpec((LAT_LEN, HD), lambda b, kv, s: (base + b, COL_V // HD + kv)),
                      cache_spec, cache_spec, table_spec, table_spec],
            out_specs=pl.BlockSpec((LAT_LEN, QW), lambda b, kv, s: (b, kv))),
        out_shape=jax.ShapeDtypeStruct((N_LAT, N_HEADS * HD), BF16),
        compiler_params=_params(("arbitrary", "arbitrary")),
        name="lat_attn",
    )(sink, u, u, u, ck, cv, cos, sin)


def _rope_tables():
    n = jnp.arange(LAT_LEN)
    row = (n // GRID_W).astype(F32)
    col = (n % GRID_W).astype(F32)
    quarter = HD // 4
    inv = ROPE_BASE ** (-jnp.arange(quarter, dtype=F32) / quarter)
    ang_r = row[:, None] * inv[None, :]
    ang_c = col[:, None] * inv[None, :]
    cos = jnp.concatenate([jnp.cos(ang_r)] * 2 + [jnp.cos(ang_c)] * 2, axis=-1)
    sin = jnp.concatenate([-jnp.sin(ang_r), jnp.sin(ang_r), -jnp.sin(ang_c), jnp.sin(ang_c)], axis=-1)
    return cos, sin


def _mix_kernel(z_ref, ac_ref, al_ref, ga_ref, gb_ref, wc_ref, wa_ref, o_ref):
    is_ctx = pl.program_id(1) < N_CTX // z_ref.shape[0]
    att = jnp.where(is_ctx, ac_ref[...], al_ref[...])
    conv_out = jnp.dot(z_ref[...], wc_ref[...], preferred_element_type=F32)
    attn_out = jnp.dot(att, wa_ref[...], preferred_element_type=F32)
    mixed = jax.nn.sigmoid(ga_ref[...]) * conv_out + jax.nn.sigmoid(gb_ref[...]) * attn_out
    o_ref[...] = mixed.astype(BF16)


def _mix(zc, att_ctx, att_lat, u, wc_bf, wa_bf):
    tm, tn = 512, 1024
    nc = N_CTX // tm
    return pl.pallas_call(
        _mix_kernel,
        grid=(D // tn, N_TOK // tm),
        in_specs=[pl.BlockSpec((tm, CONV_W), lambda j, i: (i, 0)),
                  pl.BlockSpec((tm, N_HEADS * HD), lambda j, i: (jnp.minimum(i, nc - 1), 0)),
                  pl.BlockSpec((tm, N_HEADS * HD), lambda j, i: (jnp.maximum(i - nc, 0), 0)),
                  pl.BlockSpec((tm, tn), lambda j, i: (i, COL_GA // tn + j)),
                  pl.BlockSpec((tm, tn), lambda j, i: (i, COL_GB // tn + j)),
                  pl.BlockSpec((CONV_W, tn), lambda j, i: (0, j)),
                  pl.BlockSpec((N_HEADS * HD, tn), lambda j, i: (0, j))],
        out_specs=pl.BlockSpec((tm, tn), lambda j, i: (i, j)),
        out_shape=jax.ShapeDtypeStruct((N_TOK, D), BF16),
        compiler_params=_params(("arbitrary", "arbitrary")),
        name="mix",
    )(zc, att_ctx, att_lat, u, u, wc_bf, wa_bf)


def _split_bf16(x):
    hi = x.astype(BF16)
    lo = (x - hi.astype(F32)).astype(BF16)
    return hi, lo


def _outproj_kernel(mx_ref, xc_ref, xl_ref, mod_ref, g_ref, w_ref, wr_ref, br_ref,
                    x1_ref, h2_ref, lg_ref):
    is_ctx = pl.program_id(0) < N_CTX // xc_ref.shape[0]
    x = jnp.where(is_ctx, xc_ref[...], xl_ref[...])
    m = mod_ref[0]
    x1 = x + m[2:3, :] * jnp.dot(mx_ref[...], w_ref[...], preferred_element_type=F32)
    x1_ref[...] = x1
    h2 = _rms(x1, g_ref[...]) * (1.0 + m[4:5, :]) + m[3:4, :]
    h2_ref[...] = h2
    h_hi, h_lo = _split_bf16(h2)
    w_hi, w_lo = _split_bf16(wr_ref[...])
    lg = (jnp.dot(h_hi, w_hi, preferred_element_type=F32)
          + jnp.dot(h_hi, w_lo, preferred_element_type=F32)
          + jnp.dot(h_lo, w_hi, preferred_element_type=F32))
    lg_ref[...] = lg + br_ref[...]


def _outproj(mixed, x_ctx, x_lat, mod, n2g, w_out_bf, w_router, b_router):
    tm = 512
    const = lambda i: (0, 0)
    xc_spec, xl_spec = _set_specs(tm, 2)
    return pl.pallas_call(
        _outproj_kernel,
        grid=(N_TOK // tm,),
        in_specs=[pl.BlockSpec((tm, D), lambda i: (i, 0)),
                  xc_spec, xl_spec,
                  pl.BlockSpec((1, N_MOD, D), lambda i: (_mod_row(i, tm), 0, 0)),
                  pl.BlockSpec((1, D), const),
                  pl.BlockSpec((D, D), const, pipeline_mode=pl.Buffered(1)),
                  pl.BlockSpec((D, N_EXP), const),
                  pl.BlockSpec((1, N_EXP), const)],
        out_specs=[pl.BlockSpec((tm, D), lambda i: (i, 0)),
                   pl.BlockSpec((tm, D), lambda i: (i, 0)),
                   pl.BlockSpec((tm, N_EXP), lambda i: (i, 0))],
        out_shape=[jax.ShapeDtypeStruct((N_TOK, D), F32),
                   jax.ShapeDtypeStruct((N_TOK, D), F32),
                   jax.ShapeDtypeStruct((N_TOK, N_EXP), F32)],
        compiler_params=_params(("arbitrary",)),
        name="outproj",
    )(mixed, x_ctx, x_lat, mod, n2g, w_out_bf, w_router, b_router)


SEL_CHUNK = 128


def _prefix_sum(flags, out_ref):
    n = flags.shape[1]
    ri = lax.broadcasted_iota(jnp.int32, (SEL_CHUNK, SEL_CHUNK), 0)
    ci = lax.broadcasted_iota(jnp.int32, (SEL_CHUNK, SEL_CHUNK), 1)
    tri = jnp.where(ri <= ci, 1.0, 0.0).astype(BF16)
    run = jnp.zeros((N_EXP, 1), F32)
    for ch in range(n // SEL_CHUNK):
        blk = flags[:, ch * SEL_CHUNK:(ch + 1) * SEL_CHUNK].astype(BF16)
        within = jnp.dot(blk, tri, preferred_element_type=F32) + run
        out_ref[:, ch * SEL_CHUNK:(ch + 1) * SEL_CHUNK] = within
        run = within[:, SEL_CHUNK - 1:SEL_CHUNK]


def _select_kernel(lg_ref, idx_ref, gate_ref, cs_ref):
    n = lg_ref.shape[1]
    lg = lg_ref[...]
    ex = jnp.exp(lg - jnp.max(lg, axis=0, keepdims=True))
    aff = ex / jnp.sum(ex, axis=0, keepdims=True)

    def bit_step(b, cur):
        cand = cur | (1 << (30 - b))
        cnt = jnp.sum(jnp.where(aff >= pltpu.bitcast(cand, F32), 1.0, 0.0), axis=1, keepdims=True)
        return jnp.where(cnt >= CAP, cand, cur)

    thr = pltpu.bitcast(lax.fori_loop(0, 31, bit_step, jnp.zeros((N_EXP, 1), jnp.int32)), F32)
    above = aff > thr
    tied = aff == thr
    room = CAP - jnp.sum(jnp.where(above, 1.0, 0.0), axis=1, keepdims=True)
    _prefix_sum(jnp.where(tied, 1.0, 0.0), cs_ref)
    sel = jnp.logical_or(above, jnp.logical_and(tied, cs_ref[...] <= room))
    gsel = jnp.where(sel, aff, 0.0)
    _prefix_sum(jnp.where(sel, 1.0, 0.0), cs_ref)

    slot = lax.broadcasted_iota(jnp.int32, (CAP, n), 0).astype(F32)
    ones = jnp.ones((8, n), BF16)
    for e in range(N_EXP):
        pre = cs_ref[e:e + 1, :]
        le = jnp.where(pre <= slot, 1.0, 0.0).astype(BF16)
        cnt = lax.dot_general(ones, le, (((1,), (1,)), ((), ())), preferred_element_type=F32)
        idx_ref[0, e:e + 1, :] = cnt[0:1, :].astype(jnp.int32)
        gate_ref[0, e] = jnp.sum(jnp.where(pre == slot + 1.0, gsel[e:e + 1, :], 0.0),
                                 axis=1, keepdims=True)


def _select(logits_t):
    return pl.pallas_call(
        _select_kernel,
        grid=(N_SETS,),
        in_specs=[pl.BlockSpec((N_EXP, N_CTX), lambda s: (0, s))],
        out_specs=[pl.BlockSpec((1, N_EXP, CAP), lambda s: (s, 0, 0)),
                   pl.BlockSpec((1, N_EXP, CAP, 1), lambda s: (s, 0, 0, 0))],
        out_shape=[jax.ShapeDtypeStruct((N_SETS, N_EXP, CAP), jnp.int32),
                   jax.ShapeDtypeStruct((N_SETS, N_EXP, CAP, 1), F32)],
        scratch_shapes=[pltpu.VMEM((N_EXP, N_CTX), F32)],
        compiler_params=_params(("arbitrary",)),
        name="select",
    )(logits_t)


MOE_ROWS = N_SETS * CAP
MOE_TF = 256


MOE_NF = D_EXP // MOE_TF
MOE_XQ = MOE_ROWS // MOE_NF
MOE_YH = MOE_ROWS // (MOE_NF // 2)


def _moe_kernel(rows_ref, h_hbm, gate_ref, wg_ref, wu_ref, wd_ref, y_in_hbm, y_hbm,
                xf_ref, xb_ref, acc_ref, rmw_ref, sem_x, sem_put, sem_get):
    del y_in_hbm
    e = pl.program_id(0)
    f = pl.program_id(1)
    e_prev = jnp.maximum(e - 1, 0)
    e_next = jnp.minimum(e + 1, N_EXP - 1)

    def x_copy(ex, k):
        tok = rows_ref[ex * MOE_ROWS + k]
        return pltpu.make_async_copy(h_hbm.at[pl.ds(tok, 1), :], xf_ref.at[pl.ds(k, 1), :], sem_x)

    def put_copy(ex, k):
        tok = rows_ref[ex * MOE_ROWS + k]
        return pltpu.make_async_copy(rmw_ref.at[pl.ds(k, 1), :], y_hbm.at[pl.ds(tok, 1), :], sem_put)

    def get_copy(ex, k):
        tok = rows_ref[ex * MOE_ROWS + k]
        return pltpu.make_async_copy(y_hbm.at[pl.ds(tok, 1), :], rmw_ref.at[pl.ds(k, 1), :], sem_get)

    def wait_rows(buf_ref, sem, to_hbm=False):
        whole = y_hbm.at[pl.ds(0, MOE_ROWS), :]
        src, dst = (buf_ref, whole) if to_hbm else (whole, buf_ref)
        pltpu.make_async_copy(src, dst, sem).wait()

    def ffn_step():
        wg = wg_ref[...].astype(BF16)
        wu = wu_ref[...].astype(BF16)
        wd = wd_ref[...].astype(BF16)
        for s in range(N_SETS):
            rows = slice(s * CAP, (s + 1) * CAP)
            x = xb_ref[rows, :]
            g = jnp.dot(x, wg, preferred_element_type=F32)
            up = jnp.dot(x, wu, preferred_element_type=F32)
            hid = (_silu(g) * up).astype(BF16)
            acc_ref[rows, :] += jnp.dot(hid, wd, preferred_element_type=F32)

    @pl.when(jnp.logical_and(e == 0, f == 0))
    def _():
        rmw_ref[...] = jnp.zeros(rmw_ref.shape, F32)

        def issue(k, c):
            x_copy(0, k).start()
            return c
        lax.fori_loop(0, MOE_ROWS, issue, 0)

    @pl.when(f == 0)
    def _():
        wait_rows(xf_ref, sem_x)
        xb_ref[...] = xf_ref[...].astype(BF16)
        acc_ref[...] = jnp.zeros(acc_ref.shape, F32)

    @pl.when(f == MOE_NF // 2)
    def _():
        wait_rows(rmw_ref, sem_put, to_hbm=True)

    @pl.when(f < MOE_NF // 2)
    def _():
        for k in range(MOE_XQ):
            x_copy(e_next, f * MOE_XQ + k).start()
        for k in range(MOE_YH):
            put_copy(e_prev, f * MOE_YH + k).start()
        ffn_step()

    @pl.when(f >= MOE_NF // 2)
    def _():
        for k in range(MOE_XQ):
            x_copy(e_next, f * MOE_XQ + k).start()
        for k in range(MOE_YH):
            get_copy(e, (f - MOE_NF // 2) * MOE_YH + k).start()
        ffn_step()

    @pl.when(f == MOE_NF - 1)
    def _():
        wait_rows(rmw_ref, sem_get)
        rmw_ref[...] = rmw_ref[...] + gate_ref[...] * acc_ref[...]

    @pl.when(jnp.logical_and(e == N_EXP - 1, f == MOE_NF - 1))
    def _():
        def issue(k, c):
            put_copy(N_EXP - 1, k).start()
            return c
        lax.fori_loop(0, MOE_ROWS, issue, 0)
        wait_rows(rmw_ref, sem_put, to_hbm=True)
        wait_rows(xf_ref, sem_x)


def _moe(rows, h2, gates, w_g, w_u, w_d, y0):
    return pl.pallas_call(
        _moe_kernel,
        grid_spec=pltpu.PrefetchScalarGridSpec(
            num_scalar_prefetch=1,
            grid=(N_EXP, MOE_NF),
            in_specs=[pl.BlockSpec(memory_space=pl.ANY),
                      pl.BlockSpec((None, MOE_ROWS, 1), lambda e, f, s: (e, 0, 0)),
                      pl.BlockSpec((None, D, MOE_TF), lambda e, f, s: (e, 0, f)),
                      pl.BlockSpec((None, D, MOE_TF), lambda e, f, s: (e, 0, f)),
                      pl.BlockSpec((None, MOE_TF, D), lambda e, f, s: (e, f, 0)),
                      pl.BlockSpec(memory_space=pl.ANY)],
            out_specs=pl.BlockSpec(memory_space=pl.ANY),
            scratch_shapes=[pltpu.VMEM((MOE_ROWS, D), F32),
                            pltpu.VMEM((MOE_ROWS, D), BF16),
                            pltpu.VMEM((MOE_ROWS, D), F32),
                            pltpu.VMEM((MOE_ROWS, D), F32),
                            pltpu.SemaphoreType.DMA(()),
                            pltpu.SemaphoreType.DMA(()),
                            pltpu.SemaphoreType.DMA(())]),
        out_shape=jax.ShapeDtypeStruct((N_TOK, D), F32),
        input_output_aliases={6: 0},
        compiler_params=_params(("arbitrary", "arbitrary")),
        name="moe",
    )(rows, h2, gates, w_g, w_u, w_d, y0)


def _final_kernel(x1_ref, y_ref, mod_ref, g_ref, o_ref):
    o_ref[...] = _rms(x1_ref[...] + mod_ref[0][5:6, :] * y_ref[...], g_ref[...])


def _final(x1, y, tok0, n_tok, mod, fng):
    tm = 512
    blk0 = tok0 // tm
    return pl.pallas_call(
        _final_kernel,
        grid=(n_tok // tm,),
        in_specs=[pl.BlockSpec((tm, D), lambda i: (blk0 + i, 0)),
                  pl.BlockSpec((tm, D), lambda i: (blk0 + i, 0)),
                  pl.BlockSpec((1, N_MOD, D), lambda i: (_mod_row(blk0 + i, tm), 0, 0)),
                  pl.BlockSpec((1, D), lambda i: (0, 0))],
        out_specs=pl.BlockSpec((tm, D), lambda i: (i, 0)),
        out_shape=jax.ShapeDtypeStruct((n_tok, D), F32),
        compiler_params=_params(("arbitrary",)),
        name="final",
    )(x1, y, mod, fng)


def kernel(x_prompt, x_sample, c, cache_k, cache_v, c_ctx, w_ada, b_ada, norm1_g, w_in, conv_dw, conv_dw_b, conv_ln_g, conv_ln_b, w_conv_out, attn_sink, w_attn_out, w_out, norm2_g, w_router, b_router, w_exp_gate, w_exp_up, w_exp_down, final_norm_g):
    x_ctx = x_prompt.reshape(N_CTX, D)
    x_lat = x_sample.reshape(N_LAT, D)
    cond = jnp.concatenate([c, c_ctx[None, :], jnp.zeros((MOD_ROWS - N_LAT_SEQ - 1, D), F32)], axis=0)
    mod = _ada(cond, w_ada[0], b_ada[0]).reshape(MOD_ROWS, N_MOD, D)

    u = _inproj(x_ctx, x_lat, mod, norm1_g, w_in[0])
    zc = _conv(u, conv_dw[0], conv_dw_b, conv_ln_g, conv_ln_b)

    sink = attn_sink[0]
    cos, sin = _rope_tables()
    ck = cache_k[:, 0].reshape(N_LAT_SEQ, PAST, N_KV * HD)
    cv = cache_v[:, 0].reshape(N_LAT_SEQ, PAST, N_KV * HD)
    mixed = _mix(zc, _ctx_attn(u, sink), _lat_attn(u, sink, ck, cv, cos, sin), u,
                 w_conv_out[0].astype(BF16), w_attn_out[0].astype(BF16))

    x1, h2, logits = _outproj(mixed, x_ctx, x_lat, mod, norm2_g, w_out[0].astype(BF16),
                              w_router[0], b_router)

    idx, gates = _select(logits.T)
    rows = (idx + (jnp.arange(N_SETS, dtype=jnp.int32) * N_CTX)[:, None, None])
    rows = rows.transpose(1, 0, 2).reshape(N_EXP * MOE_ROWS)
    gates = gates.transpose(1, 0, 2, 3).reshape(N_EXP, MOE_ROWS, 1)
    ffn = _moe(rows, h2, gates, w_exp_gate[0], w_exp_up[0], w_exp_down[0], jnp.zeros((N_TOK, D), F32))

    fng = final_norm_g.reshape(1, D)
    y_prompt = _final(x1, ffn, 0, N_CTX, mod, fng).reshape(N_CTX_SEQ, CTX_LEN, D)
    y_sample = _final(x1, ffn, N_CTX, N_LAT, mod, fng).reshape(N_LAT_SEQ, LAT_LEN, D)
    state_k = u[:N_CTX, COL_K:COL_V].reshape(N_CTX_SEQ, 1, CTX_LEN, N_KV, HD)
    state_v = u[:N_CTX, COL_V:COL_GA].reshape(N_CTX_SEQ, 1, CTX_LEN, N_KV, HD)
    return (y_prompt, y_sample, state_k, state_v)
```

```python
import functools

import jax
import jax.numpy as jnp
from jax import lax
from jax.experimental import pallas as pl
from jax.experimental.pallas import tpu as pltpu

D = 2048
N_CTX_SEQ, CTX_LEN = 16, 256
N_LAT_SEQ, LAT_LEN = 4, 1024
N_CTX = N_CTX_SEQ * CTX_LEN
N_LAT = N_LAT_SEQ * LAT_LEN
N_TOK = N_CTX + N_LAT
PAST = 512
GRID_W = 64
N_HEADS, N_KV, HD = 16, 4, 128
Q_GROUP = N_HEADS // N_KV
WINDOW = 128
ABLK = 128
CONV_W = 1024
CONV_K = 31
CONV_PAD = CONV_K // 2
N_EXP = 16
D_EXP = 1024
CAP = 2 * N_CTX // N_EXP
N_SETS = 2
ROPE_BASE = 10000.0
EPS = 1e-6
N_MOD = 6
NEG = -1e30
IN_COLS = 2 * CONV_W + N_HEADS * HD + 2 * N_KV * HD + 2 * D
COL_Q = 2 * CONV_W
COL_K = COL_Q + N_HEADS * HD
COL_V = COL_K + N_KV * HD
COL_GA = COL_V + N_KV * HD
COL_GB = COL_GA + D
MOD_ROWS = 8
CTX_MOD_ROW = N_LAT_SEQ

VMEM_LIMIT = 58 * 1024 * 1024
F32 = jnp.float32
BF16 = jnp.bfloat16


def _params(sem, vmem=VMEM_LIMIT):
    return pltpu.CompilerParams(dimension_semantics=sem, vmem_limit_bytes=vmem)


def _mod_row(tok_block, block_tokens):
    ctx_blocks = N_CTX // block_tokens
    per_seq = LAT_LEN // block_tokens
    return jnp.where(tok_block < ctx_blocks, CTX_MOD_ROW, (tok_block - ctx_blocks) // per_seq)


def _silu(x):
    return x * jax.nn.sigmoid(x)


def _rms(x, g):
    return x * lax.rsqrt(jnp.mean(x * x, axis=-1, keepdims=True) + EPS) * g


def _ada_kernel(cond_ref, w_ref, b_ref, o_ref):
    a = _silu(cond_ref[...]).astype(BF16)
    o_ref[...] = jnp.dot(a, w_ref[...].astype(BF16), preferred_element_type=F32) + b_ref[...]


def _ada(cond, w_ada, b_ada):
    tn = 1024
    n = N_MOD * D
    return pl.pallas_call(
        _ada_kernel,
        grid=(n // tn,),
        in_specs=[pl.BlockSpec((MOD_ROWS, D), lambda j: (0, 0)),
                  pl.BlockSpec((D, tn), lambda j: (0, j)),
                  pl.BlockSpec((1, tn), lambda j: (0, j))],
        out_specs=pl.BlockSpec((MOD_ROWS, tn), lambda j: (0, j)),
        out_shape=jax.ShapeDtypeStruct((MOD_ROWS, n), F32),
        compiler_params=_params(("arbitrary",)),
        name="ada",
    )(cond, w_ada, b_ada.reshape(1, n))


def _set_specs(tm, buffers):
    nc = N_CTX // tm
    mode = pl.Buffered(buffers)
    ctx = pl.BlockSpec((tm, D), lambda i, *_: (jnp.minimum(i, nc - 1), 0), pipeline_mode=mode)
    lat = pl.BlockSpec((tm, D), lambda i, *_: (jnp.maximum(i - nc, 0), 0), pipeline_mode=mode)
    return ctx, lat


def _inproj_kernel(xc_ref, xl_ref, mod_ref, g_ref, w_ref, o_ref, h_ref):
    @pl.when(pl.program_id(1) == 0)
    def _():
        is_ctx = pl.program_id(0) < N_CTX // xc_ref.shape[0]
        x = jnp.where(is_ctx, xc_ref[...], xl_ref[...])
        m = mod_ref[0]
        h = _rms(x, g_ref[...]) * (1.0 + m[1:2, :]) + m[0:1, :]
        h_ref[...] = h.astype(BF16)

    o_ref[...] = jnp.dot(h_ref[...], w_ref[...].astype(BF16), preferred_element_type=F32)


def _inproj(x_ctx, x_lat, mod, n1g, w_in):
    tm, tn = 1024, 1024
    xc_spec, xl_spec = _set_specs(tm, 1)
    return pl.pallas_call(
        _inproj_kernel,
        grid=(N_TOK // tm, IN_COLS // tn),
        in_specs=[xc_spec, xl_spec,
                  pl.BlockSpec((1, N_MOD, D), lambda i, j: (_mod_row(i, tm), 0, 0)),
                  pl.BlockSpec((1, D), lambda i, j: (0, 0)),
                  pl.BlockSpec((D, tn), lambda i, j: (0, j))],
        out_specs=pl.BlockSpec((tm, tn), lambda i, j: (i, j)),
        out_shape=jax.ShapeDtypeStruct((N_TOK, IN_COLS), F32),
        scratch_shapes=[pltpu.VMEM((tm, D), BF16)],
        compiler_params=_params(("arbitrary", "arbitrary")),
        name="inproj",
    )(x_ctx, x_lat, mod, n1g, w_in)


CONV_T = 256
CONV_HALO = 16
CONV_ROWS = 32
SUBLANES = 8
CONV_ZS = CONV_T + (CONV_K // SUBLANES) * SUBLANES


def _conv_kernel(main_ref, prev_ref, next_ref, w_ref, b_ref, lg_ref, lb_ref, o_ref, z_ref, zs_ref):
    i = pl.program_id(0)
    ctx_blocks = N_CTX // CONV_T
    per_seq = LAT_LEN // CONV_T
    pos = (i - ctx_blocks) % per_seq
    is_lat = i >= ctx_blocks
    has_prev = jnp.logical_and(is_lat, pos > 0)
    has_next = jnp.logical_and(is_lat, pos < per_seq - 1)

    def glu(ref):
        v = ref[...]
        return v[:, :CONV_W] * jax.nn.sigmoid(v[:, CONV_W:])

    z_ref[0:CONV_HALO, :] = jnp.where(has_prev, glu(prev_ref), 0.0)
    z_ref[CONV_HALO:CONV_HALO + CONV_T, :] = glu(main_ref)
    z_ref[CONV_HALO + CONV_T:, :] = jnp.where(has_next, glu(next_ref), 0.0)

    z = z_ref[...]
    zs_ref[0] = z[:CONV_ZS]
    for s in range(1, SUBLANES):
        zs_ref[s] = pltpu.roll(z, z.shape[0] - s, axis=0)[:CONV_ZS]

    for r in range(CONV_T // CONV_ROWS):
        acc = jnp.zeros((CONV_ROWS, CONV_W), F32)
        for j in range(CONV_K):
            a, s = divmod(CONV_HALO - CONV_PAD + j, SUBLANES)
            q = r * CONV_ROWS + a * SUBLANES
            acc = acc + zs_ref[s, q:q + CONV_ROWS, :] * w_ref[j:j + 1, :]
        acc = acc + b_ref[...]
        mu = jnp.mean(acc, axis=-1, keepdims=True)
        cen = acc - mu
        var = jnp.mean(cen * cen, axis=-1, keepdims=True)
        y = cen * lax.rsqrt(var + EPS) * lg_ref[...] + lb_ref[...]
        o_ref[r * CONV_ROWS:(r + 1) * CONV_ROWS, :] = _silu(y).astype(BF16)


def _conv(u, w_dw, b_dw, ln_g, ln_b):
    nblk = N_TOK // CONV_T
    hpb = CONV_T // CONV_HALO
    last = N_TOK // CONV_HALO - 1
    row = lambda i: (0, 0)
    return pl.pallas_call(
        _conv_kernel,
        grid=(nblk,),
        in_specs=[pl.BlockSpec((CONV_T, 2 * CONV_W), lambda i: (i, 0)),
                  pl.BlockSpec((CONV_HALO, 2 * CONV_W), lambda i: (jnp.maximum(i * hpb - 1, 0), 0)),
                  pl.BlockSpec((CONV_HALO, 2 * CONV_W), lambda i: (jnp.minimum((i + 1) * hpb, last), 0)),
                  pl.BlockSpec((CONV_K, CONV_W), row),
                  pl.BlockSpec((1, CONV_W), row),
                  pl.BlockSpec((1, CONV_W), row),
                  pl.BlockSpec((1, CONV_W), row)],
        out_specs=pl.BlockSpec((CONV_T, CONV_W), lambda i: (i, 0)),
        out_shape=jax.ShapeDtypeStruct((N_TOK, CONV_W), BF16),
        scratch_shapes=[pltpu.VMEM((CONV_T + 2 * CONV_HALO, CONV_W), F32),
                        pltpu.VMEM((SUBLANES, CONV_ZS, CONV_W), F32)],
        compiler_params=_params(("arbitrary",)),
        name="conv",
    )(u, u, u, w_dw, b_dw, ln_g, ln_b)


def _stack_heads(q):
    return jnp.concatenate([q[:, g * HD:(g + 1) * HD] for g in range(Q_GROUP)], axis=0)


def _unstack_heads(o, rows):
    return jnp.concatenate([o[g * rows:(g + 1) * rows, :] for g in range(Q_GROUP)], axis=1)


LOG2E = 1.4426950408889634
Q_SCALE = HD ** -0.5 * LOG2E
QW = Q_GROUP * HD


def _sink_column(sink_ref, kv, rows):
    rid = lax.broadcasted_iota(jnp.int32, (Q_GROUP * rows, 1), 0)
    col = jnp.full((Q_GROUP * rows, 1), sink_ref[kv * Q_GROUP], F32)
    for g in range(1, Q_GROUP):
        col = jnp.where(rid >= g * rows, sink_ref[kv * Q_GROUP + g], col)
    return col * LOG2E


def _softmax_av(s, sk, vals):
    m = jnp.maximum(jnp.max(s, axis=-1, keepdims=True), sk)
    p = jnp.exp2(s - m)
    den = jnp.sum(p, axis=-1, keepdims=True) + jnp.exp2(sk - m)
    o = jnp.dot(p.astype(BF16), vals, preferred_element_type=F32)
    return o / den


_NT = (((1,), (1,)), ((), ()))


def _ctx_attn_kernel(sink_ref, q_ref, k_ref, v_ref, o_ref):
    for kv in range(N_KV):
        q = (_stack_heads(q_ref[:, kv * QW:(kv + 1) * QW]) * Q_SCALE).astype(BF16)
        k = k_ref[:, kv * HD:(kv + 1) * HD].astype(BF16)
        v = v_ref[:, kv * HD:(kv + 1) * HD].astype(BF16)
        s = lax.dot_general(q, k, _NT, preferred_element_type=F32)
        o = _softmax_av(s, _sink_column(sink_ref, kv, CTX_LEN), v)
        o_ref[:, kv * QW:(kv + 1) * QW] = _unstack_heads(o, CTX_LEN).astype(BF16)


def _ctx_attn(u, sink):
    qcols, kcols = N_HEADS * HD, N_KV * HD
    return pl.pallas_call(
        _ctx_attn_kernel,
        grid_spec=pltpu.PrefetchScalarGridSpec(
            num_scalar_prefetch=1,
            grid=(N_CTX_SEQ,),
            in_specs=[pl.BlockSpec((CTX_LEN, qcols), lambda b, s: (b, COL_Q // qcols)),
                      pl.BlockSpec((CTX_LEN, kcols), lambda b, s: (b, COL_K // kcols)),
                      pl.BlockSpec((CTX_LEN, kcols), lambda b, s: (b, COL_V // kcols))],
            out_specs=pl.BlockSpec((CTX_LEN, qcols), lambda b, s: (b, 0))),
        out_shape=jax.ShapeDtypeStruct((N_CTX, N_HEADS * HD), BF16),
        compiler_params=_params(("arbitrary",)),
        name="ctx_attn",
    )(sink, u, u, u)


def _rope(x, cos, sin):
    lane = lax.broadcasted_iota(jnp.int32, x.shape, 1)
    fwd = pltpu.roll(x, HD - 32, axis=1)
    bwd = pltpu.roll(x, 32, axis=1)
    partner = jnp.where((lane & 32) == 0, fwd, bwd)
    return x * cos + partner * sin


def _lat_attn_kernel(sink_ref, q_ref, k_ref, v_ref, ck_ref, cv_ref, cos_ref, sin_ref, o_ref):
    kv = pl.program_id(1)
    nblk = LAT_LEN // ABLK
    cos, sin = cos_ref[...], sin_ref[...]
    k_rot = _rope(k_ref[...], cos, sin).astype(BF16)
    v = v_ref[...].astype(BF16)
    ck = ck_ref[...].astype(BF16)
    cv = cv_ref[...].astype(BF16)
    sk = _sink_column(sink_ref, kv, ABLK)
    r = lax.broadcasted_iota(jnp.int32, (Q_GROUP * ABLK, ABLK), 0) & (ABLK - 1)
    c = lax.broadcasted_iota(jnp.int32, (Q_GROUP * ABLK, ABLK), 1)
    for j in range(nblk):
        rows = slice(j * ABLK, (j + 1) * ABLK)
        cos_j, sin_j = cos[rows], sin[rows]
        qs = jnp.concatenate(
            [_rope(q_ref[rows, g * HD:(g + 1) * HD], cos_j, sin_j) * Q_SCALE for g in range(Q_GROUP)],
            axis=0).astype(BF16)
        blocks = range(max(j - 1, 0), min(j + 2, nblk))
        band = slice(blocks[0] * ABLK, (blocks[-1] + 1) * ABLK)
        keys = jnp.concatenate([k_rot[band], ck], axis=0)
        vals = jnp.concatenate([v[band], cv], axis=0)
        s = lax.dot_general(qs, keys, _NT, preferred_element_type=F32)
        tiles = []
        for t, kb in enumerate(blocks):
            tile = s[:, t * ABLK:(t + 1) * ABLK]
            if kb < j:
                tile = jnp.where(c >= r, tile, NEG)
            elif kb > j:
                tile = jnp.where(c <= r, tile, NEG)
            tiles.append(tile)
        tiles.append(s[:, len(blocks) * ABLK:])
        o = _softmax_av(jnp.concatenate(tiles, axis=1), sk, vals)
        o_ref[rows, :] = _unstack_heads(o, ABLK).astype(BF16)


def _lat_attn(u, sink, ck, cv, cos, sin):
    base = N_CTX // LAT_LEN
    cache_spec = pl.BlockSpec((None, PAST, HD), lambda b, kv, s: (b, 0, kv))
    table_spec = pl.BlockSpec((LAT_LEN, HD), lambda b, kv, s: (0, 0))
    return pl.pallas_call(
        _lat_attn_kernel,
        grid_spec=pltpu.PrefetchScalarGridSpec(
            num_scalar_prefetch=1,
            grid=(N_LAT_SEQ, N_KV),
            in_specs=[pl.BlockSpec((LAT_LEN, QW), lambda b, kv, s: (base + b, COL_Q // QW + kv)),
                      pl.BlockSpec((LAT_LEN, HD), lambda b, kv, s: (base + b, COL_K // HD + kv)),
                      pl.BlockSpec((LAT_LEN, HD), lambda b, kv, s: (base + b, COL_V // HD + kv)),
                      cache_spec, cache_spec, table_spec, table_spec],
            out_specs=pl.BlockSpec((LAT_LEN, QW), lambda b, kv, s: (b, kv))),
        out_shape=jax.ShapeDtypeStruct((N_LAT, N_HEADS * HD), BF16),
        compiler_params=_params(("arbitrary", "arbitrary")),
        name="lat_attn",
    )(sink, u, u, u, ck, cv, cos, sin)


def _rope_tables():
    n = jnp.arange(LAT_LEN)
    row = (n // GRID_W).astype(F32)
    col = (n % GRID_W).astype(F32)
    quarter = HD // 4
    inv = ROPE_BASE ** (-jnp.arange(quarter, dtype=F32) / quarter)
    ang_r = row[:, None] * inv[None, :]
    ang_c = col[:, None] * inv[None, :]
    cos = jnp.concatenate([jnp.cos(ang_r)] * 2 + [jnp.cos(ang_c)] * 2, axis=-1)
    sin = jnp.concatenate([-jnp.sin(ang_r), jnp.sin(ang_r), -jnp.sin(ang_c), jnp.sin(ang_c)], axis=-1)
    return cos, sin


def _mix_kernel(z_ref, ac_ref, al_ref, ga_ref, gb_ref, wc_ref, wa_ref, o_ref):
    is_ctx = pl.program_id(1) < N_CTX // z_ref.shape[0]
    att = jnp.where(is_ctx, ac_ref[...], al_ref[...])
    conv_out = jnp.dot(z_ref[...], wc_ref[...], preferred_element_type=F32)
    attn_out = jnp.dot(att, wa_ref[...], preferred_element_type=F32)
    mixed = jax.nn.sigmoid(ga_ref[...]) * conv_out + jax.nn.sigmoid(gb_ref[...]) * attn_out
    o_ref[...] = mixed.astype(BF16)


def _mix(zc, att_ctx, att_lat, u, wc_bf, wa_bf):
    tm, tn = 512, 1024
    nc = N_CTX // tm
    return pl.pallas_call(
        _mix_kernel,
        grid=(D // tn, N_TOK // tm),
        in_specs=[pl.BlockSpec((tm, CONV_W), lambda j, i: (i, 0)),
                  pl.BlockSpec((tm, N_HEADS * HD), lambda j, i: (jnp.minimum(i, nc - 1), 0)),
                  pl.BlockSpec((tm, N_HEADS * HD), lambda j, i: (jnp.maximum(i - nc, 0), 0)),
                  pl.BlockSpec((tm, tn), lambda j, i: (i, COL_GA // tn + j)),
                  pl.BlockSpec((tm, tn), lambda j, i: (i, COL_GB // tn + j)),
                  pl.BlockSpec((CONV_W, tn), lambda j, i: (0, j)),
                  pl.BlockSpec((N_HEADS * HD, tn), lambda j, i: (0, j))],
        out_specs=pl.BlockSpec((tm, tn), lambda j, i: (i, j)),
        out_shape=jax.ShapeDtypeStruct((N_TOK, D), BF16),
        compiler_params=_params(("arbitrary", "arbitrary")),
        name="mix",
    )(zc, att_ctx, att_lat, u, u, wc_bf, wa_bf)


def _split_bf16(x):
    hi = x.astype(BF16)
    lo = (x - hi.astype(F32)).astype(BF16)
    return hi, lo


def _outproj_kernel(mx_ref, xc_ref, xl_ref, mod_ref, g_ref, w_ref, wr_ref, br_ref,
                    x1_ref, h2_ref, lg_ref):
    is_ctx = pl.program_id(0) < N_CTX // xc_ref.shape[0]
    x = jnp.where(is_ctx, xc_ref[...], xl_ref[...])
    m = mod_ref[0]
    x1 = x + m[2:3, :] * jnp.dot(mx_ref[...], w_ref[...], preferred_element_type=F32)
    x1_ref[...] = x1
    h2 = _rms(x1, g_ref[...]) * (1.0 + m[4:5, :]) + m[3:4, :]
    h2_ref[...] = h2
    h_hi, h_lo = _split_bf16(h2)
    w_hi, w_lo = _split_bf16(wr_ref[...])
    lg = (jnp.dot(h_hi, w_hi, preferred_element_type=F32)
          + jnp.dot(h_hi, w_lo, preferred_element_type=F32)
          + jnp.dot(h_lo, w_hi, preferred_element_type=F32))
    lg_ref[...] = lg + br_ref[...]


def _outproj(mixed, x_ctx, x_lat, mod, n2g, w_out_bf, w_router, b_router):
    tm = 512
    const = lambda i: (0, 0)
    xc_spec, xl_spec = _set_specs(tm, 2)
    return pl.pallas_call(
        _outproj_kernel,
        grid=(N_TOK // tm,),
        in_specs=[pl.BlockSpec((tm, D), lambda i: (i, 0)),
                  xc_spec, xl_spec,
                  pl.BlockSpec((1, N_MOD, D), lambda i: (_mod_row(i, tm), 0, 0)),
                  pl.BlockSpec((1, D), const),
                  pl.BlockSpec((D, D), const, pipeline_mode=pl.Buffered(1)),
                  pl.BlockSpec((D, N_EXP), const),
                  pl.BlockSpec((1, N_EXP), const)],
        out_specs=[pl.BlockSpec((tm, D), lambda i: (i, 0)),
                   pl.BlockSpec((tm, D), lambda i: (i, 0)),
                   pl.BlockSpec((tm, N_EXP), lambda i: (i, 0))],
        out_shape=[jax.ShapeDtypeStruct((N_TOK, D), F32),
                   jax.ShapeDtypeStruct((N_TOK, D), F32),
                   jax.ShapeDtypeStruct((N_TOK, N_EXP), F32)],
        compiler_params=_params(("arbitrary",)),
        name="outproj",
    )(mixed, x_ctx, x_lat, mod, n2g, w_out_bf, w_router, b_router)


SEL_CHUNK = 128


def _prefix_sum(flags, out_ref):
    n = flags.shape[1]
    ri = lax.broadcasted_iota(jnp.int32, (SEL_CHUNK, SEL_CHUNK), 0)
    ci = lax.broadcasted_iota(jnp.int32, (SEL_CHUNK, SEL_CHUNK), 1)
    tri = jnp.where(ri <= ci, 1.0, 0.0).astype(BF16)
    run = jnp.zeros((N_EXP, 1), F32)
    for ch in range(n // SEL_CHUNK):
        blk = flags[:, ch * SEL_CHUNK:(ch + 1) * SEL_CHUNK].astype(BF16)
        within = jnp.dot(blk, tri, preferred_element_type=F32) + run
        out_ref[:, ch * SEL_CHUNK:(ch + 1) * SEL_CHUNK] = within
        run = within[:, SEL_CHUNK - 1:SEL_CHUNK]


def _select_kernel(lg_ref, idx_ref, gate_ref, cs_ref):
    n = lg_ref.shape[1]
    lg = lg_ref[...]
    ex = jnp.exp(lg - jnp.max(lg, axis=0, keepdims=True))
    aff = ex / jnp.sum(ex, axis=0, keepdims=True)

    def bit_step(b, cur):
        cand = cur | (1 << (30 - b))
        cnt = jnp.sum(jnp.where(aff >= pltpu.bitcast(cand, F32), 1.0, 0.0), axis=1, keepdims=True)
        return jnp.where(cnt >= CAP, cand, cur)

    thr = pltpu.bitcast(lax.fori_loop(0, 31, bit_step, jnp.zeros((N_EXP, 1), jnp.int32)), F32)
    above = aff > thr
    tied = aff == thr
    room = CAP - jnp.sum(jnp.where(above, 1.0, 0.0), axis=1, keepdims=True)
    _prefix_sum(jnp.where(tied, 1.0, 0.0), cs_ref)
    sel = jnp.logical_or(above, jnp.logical_and(tied, cs_ref[...] <= room))
    gsel = jnp.where(sel, aff, 0.0)
    _prefix_sum(jnp.where(sel, 1.0, 0.0), cs_ref)

    slot = lax.broadcasted_iota(jnp.int32, (CAP, n), 0).astype(F32)
    ones = jnp.ones((8, n), BF16)
    for e in range(N_EXP):
        pre = cs_ref[e:e + 1, :]
        le = jnp.where(pre <= slot, 1.0, 0.0).astype(BF16)
        cnt = lax.dot_general(ones, le, (((1,), (1,)), ((), ())), preferred_element_type=F32)
        idx_ref[0, e:e + 1, :] = cnt[0:1, :].astype(jnp.int32)
        gate_ref[0, e] = jnp.sum(jnp.where(pre == slot + 1.0, gsel[e:e + 1, :], 0.0),
                                 axis=1, keepdims=True)


def _select(logits_t):
    return pl.pallas_call(
        _select_kernel,
        grid=(N_SETS,),
        in_specs=[pl.BlockSpec((N_EXP, N_CTX), lambda s: (0, s))],
        out_specs=[pl.BlockSpec((1, N_EXP, CAP), lambda s: (s, 0, 0)),
                   pl.BlockSpec((1, N_EXP, CAP, 1), lambda s: (s, 0, 0, 0))],
        out_shape=[jax.ShapeDtypeStruct((N_SETS, N_EXP, CAP), jnp.int32),
                   jax.ShapeDtypeStruct((N_SETS, N_EXP, CAP, 1), F32)],
        scratch_shapes=[pltpu.VMEM((N_EXP, N_CTX), F32)],
        compiler_params=_params(("arbitrary",)),
        name="select",
    )(logits_t)


MOE_ROWS = N_SETS * CAP
MOE_TF = 256
MOE_NF = D_EXP // MOE_TF
MOE_TN = 512
N_FILL = N_TOK // MOE_ROWS
assert MOE_NF == 4


def _moe_kernel(rows_ref, h_hbm, gate_ref, wg_ref, wu_ref, wd_ref, y_hbm,
                xf_ref, xb_ref, acc_ref, rmw_ref, sem_x, sem_put, sem_get, sem_fill):
    e = pl.program_id(0)
    f = pl.program_id(1)
    e_next = jnp.minimum(e + 1, N_EXP - 1)

    def tok_row(hbm_ref, ex, k):
        return hbm_ref.at[pl.ds(rows_ref[ex * MOE_ROWS + k], 1), :]

    def x_copy(ex, k):
        return pltpu.make_async_copy(tok_row(h_hbm, ex, k), xf_ref.at[pl.ds(k, 1), :], sem_x)

    def put_copy(ex, k):
        return pltpu.make_async_copy(rmw_ref.at[pl.ds(k, 1), :], tok_row(y_hbm, ex, k), sem_put)

    def get_copy(ex, k):
        return pltpu.make_async_copy(tok_row(y_hbm, ex, k), rmw_ref.at[pl.ds(k, 1), :], sem_get)

    def fill_copy(c):
        return pltpu.make_async_copy(rmw_ref, y_hbm.at[pl.ds(c * MOE_ROWS, MOE_ROWS), :], sem_fill)

    def start_rows(copy, ex):
        for k in range(MOE_ROWS):
            copy(ex, k).start(priority=k % 2)

    def wait_rows(sem, src, dst):
        pltpu.make_async_copy(src, dst, sem).wait()

    y_rows = y_hbm.at[pl.ds(0, MOE_ROWS), :]
    h_rows = h_hbm.at[pl.ds(0, MOE_ROWS), :]

    def ffn_step(first=False):
        x = xb_ref[...]
        g = jnp.dot(x, wg_ref[...].astype(BF16), preferred_element_type=F32)
        up = jnp.dot(x, wu_ref[...].astype(BF16), preferred_element_type=F32)
        hid = (_silu(g) * up).astype(BF16)
        wd = wd_ref[...].astype(BF16)
        for c in range(D // MOE_TN):
            cols = slice(c * MOE_TN, (c + 1) * MOE_TN)
            part = jnp.dot(hid, wd[:, cols], preferred_element_type=F32)
            if first:
                acc_ref[:, cols] = part
            else:
                acc_ref[:, cols] += part

    @pl.when(jnp.logical_and(e == 0, f == 0))
    def _():
        rmw_ref[...] = jnp.zeros(rmw_ref.shape, F32)
        for c in range(N_FILL):
            fill_copy(c).start()

        def issue(k, c):
            x_copy(0, k).start()
            return c
        lax.fori_loop(0, MOE_ROWS, issue, 0)

    @pl.when(f == 0)
    def _():
        wait_rows(sem_x, h_rows, xf_ref)
        xb_ref[...] = xf_ref[...].astype(BF16)

    @pl.when(jnp.logical_and(e > 0, f == 0))
    def _():
        start_rows(put_copy, e - 1)
        ffn_step(first=True)

    @pl.when(jnp.logical_and(e == 0, f == 0))
    def _():
        ffn_step(first=True)

    @pl.when(f == 1)
    def _():
        start_rows(x_copy, e_next)
        ffn_step()

    @pl.when(jnp.logical_and(e == 0, f == 2))
    def _():
        for c in range(N_FILL):
            fill_copy(c).wait()

    @pl.when(jnp.logical_and(e > 0, f == 2))
    def _():
        wait_rows(sem_put, rmw_ref, y_rows)

    @pl.when(f == 2)
    def _():
        start_rows(get_copy, e)
        ffn_step()

    @pl.when(f == 3)
    def _():
        ffn_step()
        wait_rows(sem_get, y_rows, rmw_ref)
        rmw_ref[...] = rmw_ref[...] + gate_ref[...] * acc_ref[...]

    @pl.when(jnp.logical_and(e == N_EXP - 1, f == MOE_NF - 1))
    def _():
        def issue(k, c):
            put_copy(N_EXP - 1, k).start()
            return c
        lax.fori_loop(0, MOE_ROWS, issue, 0)
        wait_rows(sem_put, rmw_ref, y_rows)
        wait_rows(sem_x, h_rows, xf_ref)


def _moe(rows, h2, gates, w_g, w_u, w_d):
    return pl.pallas_call(
        _moe_kernel,
        grid_spec=pltpu.PrefetchScalarGridSpec(
            num_scalar_prefetch=1,
            grid=(N_EXP, MOE_NF),
            in_specs=[pl.BlockSpec(memory_space=pl.ANY),
                      pl.BlockSpec((None, MOE_ROWS, 1), lambda e, f, s: (e, 0, 0)),
                      pl.BlockSpec((None, D, MOE_TF), lambda e, f, s: (e, 0, f)),
                      pl.BlockSpec((None, D, MOE_TF), lambda e, f, s: (e, 0, f)),
                      pl.BlockSpec((None, MOE_TF, D), lambda e, f, s: (e, f, 0))],
            out_specs=pl.BlockSpec(memory_space=pl.ANY),
            scratch_shapes=[pltpu.VMEM((MOE_ROWS, D), F32),
                            pltpu.VMEM((MOE_ROWS, D), BF16),
                            pltpu.VMEM((MOE_ROWS, D), F32),
                            pltpu.VMEM((MOE_ROWS, D), F32),
                            pltpu.SemaphoreType.DMA(()),
                            pltpu.SemaphoreType.DMA(()),
                            pltpu.SemaphoreType.DMA(()),
                            pltpu.SemaphoreType.DMA(())]),
        out_shape=jax.ShapeDtypeStruct((N_TOK, D), F32),
        compiler_params=_params(("arbitrary", "arbitrary")),
        name="moe",
    )(rows, h2, gates, w_g, w_u, w_d)


def _final_kernel(x1_ref, y_ref, mod_ref, g_ref, o_ref):
    o_ref[...] = _rms(x1_ref[...] + mod_ref[0][5:6, :] * y_ref[...], g_ref[...])


def _final(x1, y, tok0, n_tok, mod, fng):
    tm = 512
    blk0 = tok0 // tm
    return pl.pallas_call(
        _final_kernel,
        grid=(n_tok // tm,),
        in_specs=[pl.BlockSpec((tm, D), lambda i: (blk0 + i, 0)),
                  pl.BlockSpec((tm, D), lambda i: (blk0 + i, 0)),
                  pl.BlockSpec((1, N_MOD, D), lambda i: (_mod_row(blk0 + i, tm), 0, 0)),
                  pl.BlockSpec((1, D), lambda i: (0, 0))],
        out_specs=pl.BlockSpec((tm, D), lambda i: (i, 0)),
        out_shape=jax.ShapeDtypeStruct((n_tok, D), F32),
        compiler_params=_params(("arbitrary",)),
        name="final",
    )(x1, y, mod, fng)


def kernel(x_prompt, x_sample, c, cache_k, cache_v, c_ctx, w_ada, b_ada, norm1_g, w_in, conv_dw, conv_dw_b, conv_ln_g, conv_ln_b, w_conv_out, attn_sink, w_attn_out, w_out, norm2_g, w_router, b_router, w_exp_gate, w_exp_up, w_exp_down, final_norm_g):
    x_ctx = x_prompt.reshape(N_CTX, D)
    x_lat = x_sample.reshape(N_LAT, D)
    cond = jnp.concatenate([c, c_ctx[None, :], jnp.zeros((MOD_ROWS - N_LAT_SEQ - 1, D), F32)], axis=0)
    mod = _ada(cond, w_ada[0], b_ada[0]).reshape(MOD_ROWS, N_MOD, D)

    u = _inproj(x_ctx, x_lat, mod, norm1_g, w_in[0])
    zc = _conv(u, conv_dw[0], conv_dw_b, conv_ln_g, conv_ln_b)

    sink = attn_sink[0]
    cos, sin = _rope_tables()
    ck = cache_k[:, 0].reshape(N_LAT_SEQ, PAST, N_KV * HD)
    cv = cache_v[:, 0].reshape(N_LAT_SEQ, PAST, N_KV * HD)
    mixed = _mix(zc, _ctx_attn(u, sink), _lat_attn(u, sink, ck, cv, cos, sin), u,
                 w_conv_out[0].astype(BF16), w_attn_out[0].astype(BF16))

    x1, h2, logits = _outproj(mixed, x_ctx, x_lat, mod, norm2_g, w_out[0].astype(BF16),
                              w_router[0], b_router)

    idx, gates = _select(logits.T)
    rows = (idx + (jnp.arange(N_SETS, dtype=jnp.int32) * N_CTX)[:, None, None])
    rows = rows.transpose(1, 0, 2).reshape(N_EXP * MOE_ROWS)
    gates = gates.transpose(1, 0, 2, 3).reshape(N_EXP, MOE_ROWS, 1)
    ffn = _moe(rows, h2, gates, w_exp_gate[0], w_exp_up[0], w_exp_down[0])

    fng = final_norm_g.reshape(1, D)
    y_prompt = _final(x1, ffn, 0, N_CTX, mod, fng).reshape(N_CTX_SEQ, CTX_LEN, D)
    y_sample = _final(x1, ffn, N_CTX, N_LAT, mod, fng).reshape(N_LAT_SEQ, LAT_LEN, D)
    state_k = u[:N_CTX, COL_K:COL_V].reshape(N_CTX_SEQ, 1, CTX_LEN, N_KV, HD)
    state_v = u[:N_CTX, COL_V:COL_GA].reshape(N_CTX_SEQ, 1, CTX_LEN, N_KV, HD)
    return (y_prompt, y_sample, state_k, state_v)
```

```python
import functools

import jax
import jax.numpy as jnp
from jax import lax
from jax.experimental import pallas as pl
from jax.experimental.pallas import tpu as pltpu

D = 2048
N_CTX_SEQ, CTX_LEN = 16, 256
N_LAT_SEQ, LAT_LEN = 4, 1024
N_CTX = N_CTX_SEQ * CTX_LEN
N_LAT = N_LAT_SEQ * LAT_LEN
N_TOK = N_CTX + N_LAT
PAST = 512
GRID_W = 64
N_HEADS, N_KV, HD = 16, 4, 128
Q_GROUP = N_HEADS // N_KV
WINDOW = 128
ABLK = 128
CONV_W = 1024
CONV_K = 31
CONV_PAD = CONV_K // 2
N_EXP = 16
D_EXP = 1024
CAP = 2 * N_CTX // N_EXP
N_SETS = 2
ROPE_BASE = 10000.0
EPS = 1e-6
N_MOD = 6
NEG = -1e30
IN_COLS = 2 * CONV_W + N_HEADS * HD + 2 * N_KV * HD + 2 * D
COL_Q = 2 * CONV_W
COL_K = COL_Q + N_HEADS * HD
COL_V = COL_K + N_KV * HD
COL_GA = COL_V + N_KV * HD
COL_GB = COL_GA + D
MOD_ROWS = 8
CTX_MOD_ROW = N_LAT_SEQ

VMEM_LIMIT = 58 * 1024 * 1024
F32 = jnp.float32
BF16 = jnp.bfloat16


def _params(sem, vmem=VMEM_LIMIT):
    return pltpu.CompilerParams(dimension_semantics=sem, vmem_limit_bytes=vmem)


def _mod_row(tok_block, block_tokens):
    ctx_blocks = N_CTX // block_tokens
    per_seq = LAT_LEN // block_tokens
    return jnp.where(tok_block < ctx_blocks, CTX_MOD_ROW, (tok_block - ctx_blocks) // per_seq)


def _silu(x):
    return x * jax.nn.sigmoid(x)


def _rms(x, g):
    return x * lax.rsqrt(jnp.mean(x * x, axis=-1, keepdims=True) + EPS) * g


def _ada_kernel(cond_ref, w_ref, b_ref, o_ref):
    a = _silu(cond_ref[...]).astype(BF16)
    o_ref[...] = jnp.dot(a, w_ref[...].astype(BF16), preferred_element_type=F32) + b_ref[...]


def _ada(cond, w_ada, b_ada):
    tn = 1024
    n = N_MOD * D
    return pl.pallas_call(
        _ada_kernel,
        grid=(n // tn,),
        in_specs=[pl.BlockSpec((MOD_ROWS, D), lambda j: (0, 0)),
                  pl.BlockSpec((D, tn), lambda j: (0, j)),
                  pl.BlockSpec((1, tn), lambda j: (0, j))],
        out_specs=pl.BlockSpec((MOD_ROWS, tn), lambda j: (0, j)),
        out_shape=jax.ShapeDtypeStruct((MOD_ROWS, n), F32),
        compiler_params=_params(("arbitrary",)),
        name="ada",
    )(cond, w_ada, b_ada.reshape(1, n))


def _set_specs(tm, buffers):
    nc = N_CTX // tm
    mode = pl.Buffered(buffers)
    ctx = pl.BlockSpec((tm, D), lambda i, *_: (jnp.minimum(i, nc - 1), 0), pipeline_mode=mode)
    lat = pl.BlockSpec((tm, D), lambda i, *_: (jnp.maximum(i - nc, 0), 0), pipeline_mode=mode)
    return ctx, lat


def _prenorm_kernel(xc_ref, xl_ref, mod_ref, g_ref, h_ref):
    is_ctx = pl.program_id(0) < N_CTX // xc_ref.shape[0]
    x = jnp.where(is_ctx, xc_ref[...], xl_ref[...])
    m = mod_ref[0]
    h = _rms(x, g_ref[...]) * (1.0 + m[1:2, :]) + m[0:1, :]
    h_ref[...] = h.astype(BF16)


def _prenorm(x_ctx, x_lat, mod, n1g):
    tm = 512
    xc_spec, xl_spec = _set_specs(tm, 2)
    return pl.pallas_call(
        _prenorm_kernel,
        grid=(N_TOK // tm,),
        in_specs=[xc_spec, xl_spec,
                  pl.BlockSpec((1, N_MOD, D), lambda i: (_mod_row(i, tm), 0, 0)),
                  pl.BlockSpec((1, D), lambda i: (0, 0))],
        out_specs=pl.BlockSpec((tm, D), lambda i: (i, 0)),
        out_shape=jax.ShapeDtypeStruct((N_TOK, D), BF16),
        compiler_params=_params(("arbitrary",)),
        name="prenorm",
    )(x_ctx, x_lat, mod, n1g)


def _inproj_kernel(h_ref, w_ref, o_ref, wb_ref):
    @pl.when(pl.program_id(1) == 0)
    def _():
        wb_ref[...] = w_ref[...].astype(BF16)

    o_ref[...] = jnp.dot(h_ref[...], wb_ref[...], preferred_element_type=F32)


def _inproj(h, w_in):
    tm, tn = 1024, 1024
    return pl.pallas_call(
        _inproj_kernel,
        grid=(IN_COLS // tn, N_TOK // tm),
        in_specs=[pl.BlockSpec((tm, D), lambda j, i: (i, 0)),
                  pl.BlockSpec((D, tn), lambda j, i: (0, j))],
        out_specs=pl.BlockSpec((tm, tn), lambda j, i: (i, j)),
        out_shape=jax.ShapeDtypeStruct((N_TOK, IN_COLS), F32),
        scratch_shapes=[pltpu.VMEM((D, tn), BF16)],
        compiler_params=_params(("arbitrary", "arbitrary")),
        name="inproj",
    )(h, w_in)


CONV_T = 256
CONV_HALO = 16
CONV_ROWS = 32
SUBLANES = 8
CONV_ZS = CONV_T + (CONV_K // SUBLANES) * SUBLANES


def _conv_kernel(main_ref, prev_ref, next_ref, w_ref, b_ref, lg_ref, lb_ref, o_ref, z_ref, zs_ref):
    i = pl.program_id(0)
    ctx_blocks = N_CTX // CONV_T
    per_seq = LAT_LEN // CONV_T
    pos = (i - ctx_blocks) % per_seq
    is_lat = i >= ctx_blocks
    has_prev = jnp.logical_and(is_lat, pos > 0)
    has_next = jnp.logical_and(is_lat, pos < per_seq - 1)

    def glu(ref):
        v = ref[...]
        return v[:, :CONV_W] * jax.nn.sigmoid(v[:, CONV_W:])

    z_ref[0:CONV_HALO, :] = jnp.where(has_prev, glu(prev_ref), 0.0)
    z_ref[CONV_HALO:CONV_HALO + CONV_T, :] = glu(main_ref)
    z_ref[CONV_HALO + CONV_T:, :] = jnp.where(has_next, glu(next_ref), 0.0)

    z = z_ref[...]
    zs_ref[0] = z[:CONV_ZS]
    for s in range(1, SUBLANES):
        zs_ref[s] = pltpu.roll(z, z.shape[0] - s, axis=0)[:CONV_ZS]

    for r in range(CONV_T // CONV_ROWS):
        acc = jnp.zeros((CONV_ROWS, CONV_W), F32)
        for j in range(CONV_K):
            a, s = divmod(CONV_HALO - CONV_PAD + j, SUBLANES)
            q = r * CONV_ROWS + a * SUBLANES
            acc = acc + zs_ref[s, q:q + CONV_ROWS, :] * w_ref[j:j + 1, :]
        acc = acc + b_ref[...]
        mu = jnp.mean(acc, axis=-1, keepdims=True)
        cen = acc - mu
        var = jnp.mean(cen * cen, axis=-1, keepdims=True)
        y = cen * lax.rsqrt(var + EPS) * lg_ref[...] + lb_ref[...]
        o_ref[r * CONV_ROWS:(r + 1) * CONV_ROWS, :] = _silu(y).astype(BF16)


def _conv(u, w_dw, b_dw, ln_g, ln_b):
    nblk = N_TOK // CONV_T
    hpb = CONV_T // CONV_HALO
    last = N_TOK // CONV_HALO - 1
    row = lambda i: (0, 0)
    return pl.pallas_call(
        _conv_kernel,
        grid=(nblk,),
        in_specs=[pl.BlockSpec((CONV_T, 2 * CONV_W), lambda i: (i, 0)),
                  pl.BlockSpec((CONV_HALO, 2 * CONV_W), lambda i: (jnp.maximum(i * hpb - 1, 0), 0)),
                  pl.BlockSpec((CONV_HALO, 2 * CONV_W), lambda i: (jnp.minimum((i + 1) * hpb, last), 0)),
                  pl.BlockSpec((CONV_K, CONV_W), row),
                  pl.BlockSpec((1, CONV_W), row),
                  pl.BlockSpec((1, CONV_W), row),
                  pl.BlockSpec((1, CONV_W), row)],
        out_specs=pl.BlockSpec((CONV_T, CONV_W), lambda i: (i, 0)),
        out_shape=jax.ShapeDtypeStruct((N_TOK, CONV_W), BF16),
        scratch_shapes=[pltpu.VMEM((CONV_T + 2 * CONV_HALO, CONV_W), F32),
                        pltpu.VMEM((SUBLANES, CONV_ZS, CONV_W), F32)],
        compiler_params=_params(("arbitrary",)),
        name="conv",
    )(u, u, u, w_dw, b_dw, ln_g, ln_b)


def _stack_heads(q):
    return jnp.concatenate([q[:, g * HD:(g + 1) * HD] for g in range(Q_GROUP)], axis=0)


def _unstack_heads(o, rows):
    return jnp.concatenate([o[g * rows:(g + 1) * rows, :] for g in range(Q_GROUP)], axis=1)


LOG2E = 1.4426950408889634
Q_SCALE = HD ** -0.5 * LOG2E
QW = Q_GROUP * HD


def _sink_column(sink_ref, kv, rows):
    rid = lax.broadcasted_iota(jnp.int32, (Q_GROUP * rows, 1), 0)
    col = jnp.full((Q_GROUP * rows, 1), sink_ref[kv * Q_GROUP], F32)
    for g in range(1, Q_GROUP):
        col = jnp.where(rid >= g * rows, sink_ref[kv * Q_GROUP + g], col)
    return col * LOG2E


def _softmax_av(s, sk, vals):
    m = jnp.maximum(jnp.max(s, axis=-1, keepdims=True), sk)
    p = jnp.exp2(s - m)
    den = jnp.sum(p, axis=-1, keepdims=True) + jnp.exp2(sk - m)
    o = jnp.dot(p.astype(BF16), vals, preferred_element_type=F32)
    return o / den


_NT = (((1,), (1,)), ((), ()))


def _ctx_attn_kernel(sink_ref, q_ref, k_ref, v_ref, o_ref):
    for kv in range(N_KV):
        q = (_stack_heads(q_ref[:, kv * QW:(kv + 1) * QW]) * Q_SCALE).astype(BF16)
        k = k_ref[:, kv * HD:(kv + 1) * HD].astype(BF16)
        v = v_ref[:, kv * HD:(kv + 1) * HD].astype(BF16)
        s = lax.dot_general(q, k, _NT, preferred_element_type=F32)
        o = _softmax_av(s, _sink_column(sink_ref, kv, CTX_LEN), v)
        o_ref[:, kv * QW:(kv + 1) * QW] = _unstack_heads(o, CTX_LEN).astype(BF16)


def _ctx_attn(u, sink):
    qcols, kcols = N_HEADS * HD, N_KV * HD
    return pl.pallas_call(
        _ctx_attn_kernel,
        grid_spec=pltpu.PrefetchScalarGridSpec(
            num_scalar_prefetch=1,
            grid=(N_CTX_SEQ,),
            in_specs=[pl.BlockSpec((CTX_LEN, qcols), lambda b, s: (b, COL_Q // qcols)),
                      pl.BlockSpec((CTX_LEN, kcols), lambda b, s: (b, COL_K // kcols)),
                      pl.BlockSpec((CTX_LEN, kcols), lambda b, s: (b, COL_V // kcols))],
            out_specs=pl.BlockSpec((CTX_LEN, qcols), lambda b, s: (b, 0))),
        out_shape=jax.ShapeDtypeStruct((N_CTX, N_HEADS * HD), BF16),
        compiler_params=_params(("arbitrary",)),
        name="ctx_attn",
    )(sink, u, u, u)


def _rope(x, cos, sin):
    lane = lax.broadcasted_iota(jnp.int32, x.shape, 1)
    fwd = pltpu.roll(x, HD - 32, axis=1)
    bwd = pltpu.roll(x, 32, axis=1)
    partner = jnp.where((lane & 32) == 0, fwd, bwd)
    return x * cos + partner * sin


def _lat_attn_kernel(sink_ref, q_ref, k_ref, v_ref, ck_ref, cv_ref, cos_ref, sin_ref, o_ref):
    kv = pl.program_id(1)
    nblk = LAT_LEN // ABLK
    cos, sin = cos_ref[...], sin_ref[...]
    k_rot = _rope(k_ref[...], cos, sin).astype(BF16)
    v = v_ref[...].astype(BF16)
    ck = ck_ref[...].astype(BF16)
    cv = cv_ref[...].astype(BF16)
    sk = _sink_column(sink_ref, kv, ABLK)
    r = lax.broadcasted_iota(jnp.int32, (Q_GROUP * ABLK, ABLK), 0) & (ABLK - 1)
    c = lax.broadcasted_iota(jnp.int32, (Q_GROUP * ABLK, ABLK), 1)
    for j in range(nblk):
        rows = slice(j * ABLK, (j + 1) * ABLK)
        cos_j, sin_j = cos[rows], sin[rows]
        qs = jnp.concatenate(
            [_rope(q_ref[rows, g * HD:(g + 1) * HD], cos_j, sin_j) * Q_SCALE for g in range(Q_GROUP)],
            axis=0).astype(BF16)
        blocks = range(max(j - 1, 0), min(j + 2, nblk))
        band = slice(blocks[0] * ABLK, (blocks[-1] + 1) * ABLK)
        keys = jnp.concatenate([k_rot[band], ck], axis=0)
        vals = jnp.concatenate([v[band], cv], axis=0)
        s = lax.dot_general(qs, keys, _NT, preferred_element_type=F32)
        tiles = []
        for t, kb in enumerate(blocks):
            tile = s[:, t * ABLK:(t + 1) * ABLK]
            if kb < j:
                tile = jnp.where(c >= r, tile, NEG)
            elif kb > j:
                tile = jnp.where(c <= r, tile, NEG)
            tiles.append(tile)
        tiles.append(s[:, len(blocks) * ABLK:])
        o = _softmax_av(jnp.concatenate(tiles, axis=1), sk, vals)
        o_ref[rows, :] = _unstack_heads(o, ABLK).astype(BF16)


def _lat_attn(u, sink, ck, cv, cos, sin):
    base = N_CTX // LAT_LEN
    cache_spec = pl.BlockSpec((None, PAST, HD), lambda b, kv, s: (b, 0, kv))
    table_spec = pl.BlockSpec((LAT_LEN, HD), lambda b, kv, s: (0, 0))
    return pl.pallas_call(
        _lat_attn_kernel,
        grid_spec=pltpu.PrefetchScalarGridSpec(
            num_scalar_prefetch=1,
            grid=(N_LAT_SEQ, N_KV),
            in_specs=[pl.BlockSpec((LAT_LEN, QW), lambda b, kv, s: (base + b, COL_Q // QW + kv)),
                      pl.BlockSpec((LAT_LEN, HD), lambda b, kv, s: (base + b, COL_K // HD + kv)),
                      pl.BlockSpec((LAT_LEN, HD), lambda b, kv, s: (base + b, COL_V // HD + kv)),
                      cache_spec, cache_spec, table_spec, table_spec],
            out_specs=pl.BlockSpec((LAT_LEN, QW), lambda b, kv, s: (b, kv))),
        out_shape=jax.ShapeDtypeStruct((N_LAT, N_HEADS * HD), BF16),
        compiler_params=_params(("arbitrary", "arbitrary")),
        name="lat_attn",
    )(sink, u, u, u, ck, cv, cos, sin)


def _rope_tables():
    n = jnp.arange(LAT_LEN)
    row = (n // GRID_W).astype(F32)
    col = (n % GRID_W).astype(F32)
    quarter = HD // 4
    inv = ROPE_BASE ** (-jnp.arange(quarter, dtype=F32) / quarter)
    ang_r = row[:, None] * inv[None, :]
    ang_c = col[:, None] * inv[None, :]
    cos = jnp.concatenate([jnp.cos(ang_r)] * 2 + [jnp.cos(ang_c)] * 2, axis=-1)
    sin = jnp.concatenate([-jnp.sin(ang_r), jnp.sin(ang_r), -jnp.sin(ang_c), jnp.sin(ang_c)], axis=-1)
    return cos, sin


def _mix_kernel(z_ref, ac_ref, al_ref, ga_ref, gb_ref, wc_ref, wa_ref, o_ref):
    is_ctx = pl.program_id(1) < N_CTX // z_ref.shape[0]
    att = jnp.where(is_ctx, ac_ref[...], al_ref[...])
    conv_out = jnp.dot(z_ref[...], wc_ref[...], preferred_element_type=F32)
    attn_out = jnp.dot(att, wa_ref[...], preferred_element_type=F32)
    mixed = jax.nn.sigmoid(ga_ref[...]) * conv_out + jax.nn.sigmoid(gb_ref[...]) * attn_out
    o_ref[...] = mixed.astype(BF16)


def _mix(zc, att_ctx, att_lat, u, wc_bf, wa_bf):
    tm, tn = 512, 1024
    nc = N_CTX // tm
    return pl.pallas_call(
        _mix_kernel,
        grid=(D // tn, N_TOK // tm),
        in_specs=[pl.BlockSpec((tm, CONV_W), lambda j, i: (i, 0)),
                  pl.BlockSpec((tm, N_HEADS * HD), lambda j, i: (jnp.minimum(i, nc - 1), 0)),
                  pl.BlockSpec((tm, N_HEADS * HD), lambda j, i: (jnp.maximum(i - nc, 0), 0)),
                  pl.BlockSpec((tm, tn), lambda j, i: (i, COL_GA // tn + j)),
                  pl.BlockSpec((tm, tn), lambda j, i: (i, COL_GB // tn + j)),
                  pl.BlockSpec((CONV_W, tn), lambda j, i: (0, j)),
                  pl.BlockSpec((N_HEADS * HD, tn), lambda j, i: (0, j))],
        out_specs=pl.BlockSpec((tm, tn), lambda j, i: (i, j)),
        out_shape=jax.ShapeDtypeStruct((N_TOK, D), BF16),
        compiler_params=_params(("arbitrary", "arbitrary")),
        name="mix",
    )(zc, att_ctx, att_lat, u, u, wc_bf, wa_bf)


def _split_bf16(x):
    hi = x.astype(BF16)
    lo = (x - hi.astype(F32)).astype(BF16)
    return hi, lo


def _outproj_kernel(mx_ref, xc_ref, xl_ref, mod_ref, g_ref, w_ref, wr_ref, br_ref,
                    x1_ref, h2_ref, lg_ref):
    is_ctx = pl.program_id(0) < N_CTX // xc_ref.shape[0]
    x = jnp.where(is_ctx, xc_ref[...], xl_ref[...])
    m = mod_ref[0]
    x1 = x + m[2:3, :] * jnp.dot(mx_ref[...], w_ref[...], preferred_element_type=F32)
    x1_ref[...] = x1
    h2 = _rms(x1, g_ref[...]) * (1.0 + m[4:5, :]) + m[3:4, :]
    h2_ref[...] = h2
    h_hi, h_lo = _split_bf16(h2)
    w_hi, w_lo = _split_bf16(wr_ref[...])
    lg = (jnp.dot(h_hi, w_hi, preferred_element_type=F32)
          + jnp.dot(h_hi, w_lo, preferred_element_type=F32)
          + jnp.dot(h_lo, w_hi, preferred_element_type=F32))
    lg_ref[...] = lg + br_ref[...]


def _outproj(mixed, x_ctx, x_lat, mod, n2g, w_out_bf, w_router, b_router):
    tm = 512
    const = lambda i: (0, 0)
    xc_spec, xl_spec = _set_specs(tm, 2)
    return pl.pallas_call(
        _outproj_kernel,
        grid=(N_TOK // tm,),
        in_specs=[pl.BlockSpec((tm, D), lambda i: (i, 0)),
                  xc_spec, xl_spec,
                  pl.BlockSpec((1, N_MOD, D), lambda i: (_mod_row(i, tm), 0, 0)),
                  pl.BlockSpec((1, D), const),
                  pl.BlockSpec((D, D), const, pipeline_mode=pl.Buffered(1)),
                  pl.BlockSpec((D, N_EXP), const),
                  pl.BlockSpec((1, N_EXP), const)],
        out_specs=[pl.BlockSpec((tm, D), lambda i: (i, 0)),
                   pl.BlockSpec((tm, D), lambda i: (i, 0)),
                   pl.BlockSpec((tm, N_EXP), lambda i: (i, 0))],
        out_shape=[jax.ShapeDtypeStruct((N_TOK, D), F32),
                   jax.ShapeDtypeStruct((N_TOK, D), F32),
                   jax.ShapeDtypeStruct((N_TOK, N_EXP), F32)],
        compiler_params=_params(("arbitrary",)),
        name="outproj",
    )(mixed, x_ctx, x_lat, mod, n2g, w_out_bf, w_router, b_router)


SEL_CHUNK = 128


def _prefix_sum(flags, out_ref):
    n = flags.shape[1]
    ri = lax.broadcasted_iota(jnp.int32, (SEL_CHUNK, SEL_CHUNK), 0)
    ci = lax.broadcasted_iota(jnp.int32, (SEL_CHUNK, SEL_CHUNK), 1)
    tri = jnp.where(ri <= ci, 1.0, 0.0).astype(BF16)
    run = jnp.zeros((N_EXP, 1), F32)
    for ch in range(n // SEL_CHUNK):
        blk = flags[:, ch * SEL_CHUNK:(ch + 1) * SEL_CHUNK].astype(BF16)
        within = jnp.dot(blk, tri, preferred_element_type=F32) + run
        out_ref[:, ch * SEL_CHUNK:(ch + 1) * SEL_CHUNK] = within
        run = within[:, SEL_CHUNK - 1:SEL_CHUNK]


def _select_kernel(lg_ref, idx_ref, gate_ref, cs_ref):
    n = lg_ref.shape[1]
    lg = lg_ref[...]
    ex = jnp.exp(lg - jnp.max(lg, axis=0, keepdims=True))
    aff = ex / jnp.sum(ex, axis=0, keepdims=True)

    def bit_step(b, cur):
        cand = cur | (1 << (30 - b))
        cnt = jnp.sum(jnp.where(aff >= pltpu.bitcast(cand, F32), 1.0, 0.0), axis=1, keepdims=True)
        return jnp.where(cnt >= CAP, cand, cur)

    thr = pltpu.bitcast(lax.fori_loop(0, 31, bit_step, jnp.zeros((N_EXP, 1), jnp.int32)), F32)
    above = aff > thr
    tied = aff == thr
    room = CAP - jnp.sum(jnp.where(above, 1.0, 0.0), axis=1, keepdims=True)
    _prefix_sum(jnp.where(tied, 1.0, 0.0), cs_ref)
    sel = jnp.logical_or(above, jnp.logical_and(tied, cs_ref[...] <= room))
    gsel = jnp.where(sel, aff, 0.0)
    _prefix_sum(jnp.where(sel, 1.0, 0.0), cs_ref)

    slot = lax.broadcasted_iota(jnp.int32, (CAP, n), 0).astype(F32)
    g_hi = gsel.astype(BF16).astype(F32)
    rest = gsel - g_hi
    g_mid = rest.astype(BF16).astype(F32)
    g_lo = rest - g_mid
    ones = jnp.ones((1, n), F32)
    pad = jnp.zeros((4, n), F32)
    last_slot = lax.broadcasted_iota(jnp.int32, (1, CAP), 1) == CAP - 1
    total = jnp.sum(gsel, axis=1, keepdims=True)
    for e in range(N_EXP):
        row = slice(e, e + 1)
        le = jnp.where(cs_ref[row, :] <= slot, 1.0, 0.0).astype(BF16)
        lhs = jnp.concatenate([ones, g_hi[row], g_mid[row], g_lo[row], pad], axis=0).astype(BF16)
        res = lax.dot_general(lhs, le, (((1,), (1,)), ((), ())), preferred_element_type=F32)
        idx_ref[0, row, :] = res[0:1, :].astype(jnp.int32)
        cum = res[1:2, :] + res[2:3, :] + res[3:4, :]
        gate_ref[0, row, :] = jnp.where(last_slot, total[row], pltpu.roll(cum, CAP - 1, axis=1)) - cum


def _select(logits_t):
    return pl.pallas_call(
        _select_kernel,
        grid=(N_SETS,),
        in_specs=[pl.BlockSpec((N_EXP, N_CTX), lambda s: (0, s))],
        out_specs=[pl.BlockSpec((1, N_EXP, CAP), lambda s: (s, 0, 0)),
                   pl.BlockSpec((1, N_EXP, CAP), lambda s: (s, 0, 0))],
        out_shape=[jax.ShapeDtypeStruct((N_SETS, N_EXP, CAP), jnp.int32),
                   jax.ShapeDtypeStruct((N_SETS, N_EXP, CAP), F32)],
        scratch_shapes=[pltpu.VMEM((N_EXP, N_CTX), F32)],
        compiler_params=_params(("arbitrary",)),
        name="select",
    )(logits_t)


MOE_ROWS = N_SETS * CAP
MOE_TF = 256
MOE_NF = D_EXP // MOE_TF
MOE_TN = 512
N_FILL = N_TOK // MOE_ROWS
assert MOE_NF == 4


def _moe_kernel(rows_ref, h_hbm, gate_ref, wg_ref, wu_ref, wd_ref, y_hbm,
                xf_ref, xb_ref, acc_ref, rmw_ref, sem_x, sem_put, sem_get, sem_fill):
    e = pl.program_id(0)
    f = pl.program_id(1)
    e_next = jnp.minimum(e + 1, N_EXP - 1)

    def tok_row(hbm_ref, ex, k):
        return hbm_ref.at[pl.ds(rows_ref[ex * MOE_ROWS + k], 1), :]

    def x_copy(ex, k):
        return pltpu.make_async_copy(tok_row(h_hbm, ex, k), xf_ref.at[pl.ds(k, 1), :], sem_x)

    def put_copy(ex, k):
        return pltpu.make_async_copy(rmw_ref.at[pl.ds(k, 1), :], tok_row(y_hbm, ex, k), sem_put)

    def get_copy(ex, k):
        return pltpu.make_async_copy(tok_row(y_hbm, ex, k), rmw_ref.at[pl.ds(k, 1), :], sem_get)

    def fill_copy(c):
        return pltpu.make_async_copy(rmw_ref, y_hbm.at[pl.ds(c * MOE_ROWS, MOE_ROWS), :], sem_fill)

    def start_rows(copy, ex):
        for k in range(MOE_ROWS):
            copy(ex, k).start(priority=k % 2)

    def wait_rows(sem, src, dst):
        pltpu.make_async_copy(src, dst, sem).wait()

    y_rows = y_hbm.at[pl.ds(0, MOE_ROWS), :]
    h_rows = h_hbm.at[pl.ds(0, MOE_ROWS), :]

    def ffn_step(first=False):
        x = xb_ref[...]
        g = jnp.dot(x, wg_ref[...].astype(BF16), preferred_element_type=F32)
        up = jnp.dot(x, wu_ref[...].astype(BF16), preferred_element_type=F32)
        hid = (_silu(g) * up).astype(BF16)
        wd = wd_ref[...].astype(BF16)
        for c in range(D // MOE_TN):
            cols = slice(c * MOE_TN, (c + 1) * MOE_TN)
            part = jnp.dot(hid, wd[:, cols], preferred_element_type=F32)
            if first:
                acc_ref[:, cols] = part
            else:
                acc_ref[:, cols] += part

    @pl.when(jnp.logical_and(e == 0, f == 0))
    def _():
        rmw_ref[...] = jnp.zeros(rmw_ref.shape, F32)
        for c in range(N_FILL):
            fill_copy(c).start()

        def issue(k, c):
            x_copy(0, k).start()
            return c
        lax.fori_loop(0, MOE_ROWS, issue, 0)

    @pl.when(f == 0)
    def _():
        wait_rows(sem_x, h_rows, xf_ref)
        xb_ref[...] = xf_ref[...].astype(BF16)

    @pl.when(jnp.logical_and(e > 0, f == 0))
    def _():
        start_rows(put_copy, e - 1)
        ffn_step(first=True)

    @pl.when(jnp.logical_and(e == 0, f == 0))
    def _():
        ffn_step(first=True)

    @pl.when(f == 1)
    def _():
        start_rows(x_copy, e_next)
        ffn_step()

    @pl.when(jnp.logical_and(e == 0, f == 2))
    def _():
        for c in range(N_FILL):
            fill_copy(c).wait()

    @pl.when(jnp.logical_and(e > 0, f == 2))
    def _():
        wait_rows(sem_put, rmw_ref, y_rows)

    @pl.when(f == 2)
    def _():
        start_rows(get_copy, e)
        ffn_step()

    @pl.when(f == 3)
    def _():
        ffn_step()
        wait_rows(sem_get, y_rows, rmw_ref)
        rmw_ref[...] = rmw_ref[...] + gate_ref[...] * acc_ref[...]

    @pl.when(jnp.logical_and(e == N_EXP - 1, f == MOE_NF - 1))
    def _():
        def issue(k, c):
            put_copy(N_EXP - 1, k).start()
            return c
        lax.fori_loop(0, MOE_ROWS, issue, 0)
        wait_rows(sem_put, rmw_ref, y_rows)
        wait_rows(sem_x, h_rows, xf_ref)


def _moe(rows, h2, gates, w_g, w_u, w_d):
    return pl.pallas_call(
        _moe_kernel,
        grid_spec=pltpu.PrefetchScalarGridSpec(
            num_scalar_prefetch=1,
            grid=(N_EXP, MOE_NF),
            in_specs=[pl.BlockSpec(memory_space=pl.ANY),
                      pl.BlockSpec((None, MOE_ROWS, 1), lambda e, f, s: (e, 0, 0)),
                      pl.BlockSpec((None, D, MOE_TF), lambda e, f, s: (e, 0, f)),
                      pl.BlockSpec((None, D, MOE_TF), lambda e, f, s: (e, 0, f)),
                      pl.BlockSpec((None, MOE_TF, D), lambda e, f, s: (e, f, 0))],
            out_specs=pl.BlockSpec(memory_space=pl.ANY),
            scratch_shapes=[pltpu.VMEM((MOE_ROWS, D), F32),
                            pltpu.VMEM((MOE_ROWS, D), BF16),
                            pltpu.VMEM((MOE_ROWS, D), F32),
                            pltpu.VMEM((MOE_ROWS, D), F32),
                            pltpu.SemaphoreType.DMA(()),
                            pltpu.SemaphoreType.DMA(()),
                            pltpu.SemaphoreType.DMA(()),
                            pltpu.SemaphoreType.DMA(())]),
        out_shape=jax.ShapeDtypeStruct((N_TOK, D), F32),
        compiler_params=_params(("arbitrary", "arbitrary")),
        name="moe",
    )(rows, h2, gates, w_g, w_u, w_d)


def _final_kernel(x1_ref, y_ref, mod_ref, g_ref, o_ref):
    o_ref[...] = _rms(x1_ref[...] + mod_ref[0][5:6, :] * y_ref[...], g_ref[...])


def _final(x1, y, tok0, n_tok, mod, fng):
    tm = 512
    blk0 = tok0 // tm
    return pl.pallas_call(
        _final_kernel,
        grid=(n_tok // tm,),
        in_specs=[pl.BlockSpec((tm, D), lambda i: (blk0 + i, 0)),
                  pl.BlockSpec((tm, D), lambda i: (blk0 + i, 0)),
                  pl.BlockSpec((1, N_MOD, D), lambda i: (_mod_row(blk0 + i, tm), 0, 0)),
                  pl.BlockSpec((1, D), lambda i: (0, 0))],
        out_specs=pl.BlockSpec((tm, D), lambda i: (i, 0)),
        out_shape=jax.ShapeDtypeStruct((n_tok, D), F32),
        compiler_params=_params(("arbitrary",)),
        name="final",
    )(x1, y, mod, fng)


def kernel(x_prompt, x_sample, c, cache_k, cache_v, c_ctx, w_ada, b_ada, norm1_g, w_in, conv_dw, conv_dw_b, conv_ln_g, conv_ln_b, w_conv_out, attn_sink, w_attn_out, w_out, norm2_g, w_router, b_router, w_exp_gate, w_exp_up, w_exp_down, final_norm_g):
    x_ctx = x_prompt.reshape(N_CTX, D)
    x_lat = x_sample.reshape(N_LAT, D)
    cond = jnp.concatenate([c, c_ctx[None, :], jnp.zeros((MOD_ROWS - N_LAT_SEQ - 1, D), F32)], axis=0)
    mod = _ada(cond, w_ada[0], b_ada[0]).reshape(MOD_ROWS, N_MOD, D)

    u = _inproj(_prenorm(x_ctx, x_lat, mod, norm1_g), w_in[0])
    zc = _conv(u, conv_dw[0], conv_dw_b, conv_ln_g, conv_ln_b)

    sink = attn_sink[0]
    cos, sin = _rope_tables()
    ck = cache_k[:, 0].reshape(N_LAT_SEQ, PAST, N_KV * HD)
    cv = cache_v[:, 0].reshape(N_LAT_SEQ, PAST, N_KV * HD)
    mixed = _mix(zc, _ctx_attn(u, sink), _lat_attn(u, sink, ck, cv, cos, sin), u,
                 w_conv_out[0].astype(BF16), w_attn_out[0].astype(BF16))

    x1, h2, logits = _outproj(mixed, x_ctx, x_lat, mod, norm2_g, w_out[0].astype(BF16),
                              w_router[0], b_router)

    idx, gates = _select(logits.T)
    rows = (idx + (jnp.arange(N_SETS, dtype=jnp.int32) * N_CTX)[:, None, None])
    rows = rows.transpose(1, 0, 2).reshape(N_EXP * MOE_ROWS)
    gates = gates.transpose(1, 0, 2).reshape(N_EXP, MOE_ROWS, 1)
    ffn = _moe(rows, h2, gates, w_exp_gate[0], w_exp_up[0], w_exp_down[0])

    fng = final_norm_g.reshape(1, D)
    y_prompt = _final(x1, ffn, 0, N_CTX, mod, fng).reshape(N_CTX_SEQ, CTX_LEN, D)
    y_sample = _final(x1, ffn, N_CTX, N_LAT, mod, fng).reshape(N_LAT_SEQ, LAT_LEN, D)
    state_k = u[:N_CTX, COL_K:COL_V].reshape(N_CTX_SEQ, 1, CTX_LEN, N_KV, HD)
    state_v = u[:N_CTX, COL_V:COL_GA].reshape(N_CTX_SEQ, 1, CTX_LEN, N_KV, HD)
    return (y_prompt, y_sample, state_k, state_v)
```

```python
import functools

import jax
import jax.numpy as jnp
from jax import lax
from jax.experimental import pallas as pl
from jax.experimental.pallas import tpu as pltpu

D = 2048
N_CTX_SEQ, CTX_LEN = 16, 256
N_LAT_SEQ, LAT_LEN = 4, 1024
N_CTX = N_CTX_SEQ * CTX_LEN
N_LAT = N_LAT_SEQ * LAT_LEN
N_TOK = N_CTX + N_LAT
PAST = 512
GRID_W = 64
N_HEADS, N_KV, HD = 16, 4, 128
Q_GROUP = N_HEADS // N_KV
WINDOW = 128
ABLK = 128
CONV_W = 1024
CONV_K = 31
CONV_PAD = CONV_K // 2
N_EXP = 16
D_EXP = 1024
CAP = 2 * N_CTX // N_EXP
N_SETS = 2
ROPE_BASE = 10000.0
EPS = 1e-6
N_MOD = 6
NEG = -1e30
IN_COLS = 2 * CONV_W + N_HEADS * HD + 2 * N_KV * HD + 2 * D
COL_Q = 2 * CONV_W
COL_K = COL_Q + N_HEADS * HD
COL_V = COL_K + N_KV * HD
COL_GA = COL_V + N_KV * HD
COL_GB = COL_GA + D
MOD_ROWS = 8
CTX_MOD_ROW = N_LAT_SEQ

VMEM_LIMIT = 58 * 1024 * 1024
F32 = jnp.float32
BF16 = jnp.bfloat16


def _params(sem, vmem=VMEM_LIMIT):
    return pltpu.CompilerParams(dimension_semantics=sem, vmem_limit_bytes=vmem)


def _mod_row(tok_block, block_tokens):
    ctx_blocks = N_CTX // block_tokens
    per_seq = LAT_LEN // block_tokens
    return jnp.where(tok_block < ctx_blocks, CTX_MOD_ROW, (tok_block - ctx_blocks) // per_seq)


def _silu(x):
    return x * jax.nn.sigmoid(x)


def _rms(x, g):
    return x * lax.rsqrt(jnp.mean(x * x, axis=-1, keepdims=True) + EPS) * g


def _ada_kernel(cond_ref, w_ref, b_ref, o_ref):
    a = _silu(cond_ref[...]).astype(BF16)
    o_ref[...] = jnp.dot(a, w_ref[...].astype(BF16), preferred_element_type=F32) + b_ref[...]


def _ada(cond, w_ada, b_ada):
    tn = 1024
    n = N_MOD * D
    return pl.pallas_call(
        _ada_kernel,
        grid=(n // tn,),
        in_specs=[pl.BlockSpec((MOD_ROWS, D), lambda j: (0, 0)),
                  pl.BlockSpec((D, tn), lambda j: (0, j)),
                  pl.BlockSpec((1, tn), lambda j: (0, j))],
        out_specs=pl.BlockSpec((MOD_ROWS, tn), lambda j: (0, j)),
        out_shape=jax.ShapeDtypeStruct((MOD_ROWS, n), F32),
        compiler_params=_params(("arbitrary",)),
        name="ada",
    )(cond, w_ada, b_ada.reshape(1, n))


def _set_specs(tm, buffers):
    nc = N_CTX // tm
    mode = pl.Buffered(buffers)
    ctx = pl.BlockSpec((tm, D), lambda i, *_: (jnp.minimum(i, nc - 1), 0), pipeline_mode=mode)
    lat = pl.BlockSpec((tm, D), lambda i, *_: (jnp.maximum(i - nc, 0), 0), pipeline_mode=mode)
    return ctx, lat


def _prenorm_kernel(xc_ref, xl_ref, mod_ref, g_ref, h_ref):
    is_ctx = pl.program_id(0) < N_CTX // xc_ref.shape[0]
    x = jnp.where(is_ctx, xc_ref[...], xl_ref[...])
    m = mod_ref[0]
    h = _rms(x, g_ref[...]) * (1.0 + m[1:2, :]) + m[0:1, :]
    h_ref[...] = h.astype(BF16)


def _prenorm(x_ctx, x_lat, mod, n1g):
    tm = 512
    xc_spec, xl_spec = _set_specs(tm, 2)
    return pl.pallas_call(
        _prenorm_kernel,
        grid=(N_TOK // tm,),
        in_specs=[xc_spec, xl_spec,
                  pl.BlockSpec((1, N_MOD, D), lambda i: (_mod_row(i, tm), 0, 0)),
                  pl.BlockSpec((1, D), lambda i: (0, 0))],
        out_specs=pl.BlockSpec((tm, D), lambda i: (i, 0)),
        out_shape=jax.ShapeDtypeStruct((N_TOK, D), BF16),
        compiler_params=_params(("arbitrary",)),
        name="prenorm",
    )(x_ctx, x_lat, mod, n1g)


def _inproj_kernel(h_ref, w_ref, o_ref, wb_ref):
    @pl.when(pl.program_id(1) == 0)
    def _():
        wb_ref[...] = w_ref[...].astype(BF16)

    o_ref[...] = jnp.dot(h_ref[...], wb_ref[...], preferred_element_type=F32)


def _inproj(h, w_in):
    tm, tn = 1024, 1024
    return pl.pallas_call(
        _inproj_kernel,
        grid=(IN_COLS // tn, N_TOK // tm),
        in_specs=[pl.BlockSpec((tm, D), lambda j, i: (i, 0)),
                  pl.BlockSpec((D, tn), lambda j, i: (0, j))],
        out_specs=pl.BlockSpec((tm, tn), lambda j, i: (i, j)),
        out_shape=jax.ShapeDtypeStruct((N_TOK, IN_COLS), F32),
        scratch_shapes=[pltpu.VMEM((D, tn), BF16)],
        compiler_params=_params(("arbitrary", "arbitrary")),
        name="inproj",
    )(h, w_in)


CONV_T = 256
CONV_HALO = 16
CONV_ROWS = 32
SUBLANES = 8
CONV_ZS = CONV_T + (CONV_K // SUBLANES) * SUBLANES


def _conv_kernel(main_ref, prev_ref, next_ref, w_ref, b_ref, lg_ref, lb_ref, o_ref, z_ref, zs_ref):
    i = pl.program_id(0)
    ctx_blocks = N_CTX // CONV_T
    per_seq = LAT_LEN // CONV_T
    pos = (i - ctx_blocks) % per_seq
    is_lat = i >= ctx_blocks
    has_prev = jnp.logical_and(is_lat, pos > 0)
    has_next = jnp.logical_and(is_lat, pos < per_seq - 1)

    def glu(ref):
        v = ref[...]
        return v[:, :CONV_W] * jax.nn.sigmoid(v[:, CONV_W:])

    z_ref[0:CONV_HALO, :] = jnp.where(has_prev, glu(prev_ref), 0.0)
    z_ref[CONV_HALO:CONV_HALO + CONV_T, :] = glu(main_ref)
    z_ref[CONV_HALO + CONV_T:, :] = jnp.where(has_next, glu(next_ref), 0.0)

    z = z_ref[...]
    zs_ref[0] = z[:CONV_ZS]
    for s in range(1, SUBLANES):
        zs_ref[s] = pltpu.roll(z, z.shape[0] - s, axis=0)[:CONV_ZS]

    for r in range(CONV_T // CONV_ROWS):
        acc = jnp.zeros((CONV_ROWS, CONV_W), F32)
        for j in range(CONV_K):
            a, s = divmod(CONV_HALO - CONV_PAD + j, SUBLANES)
            q = r * CONV_ROWS + a * SUBLANES
            acc = acc + zs_ref[s, q:q + CONV_ROWS, :] * w_ref[j:j + 1, :]
        acc = acc + b_ref[...]
        mu = jnp.mean(acc, axis=-1, keepdims=True)
        cen = acc - mu
        var = jnp.mean(cen * cen, axis=-1, keepdims=True)
        y = cen * lax.rsqrt(var + EPS) * lg_ref[...] + lb_ref[...]
        o_ref[r * CONV_ROWS:(r + 1) * CONV_ROWS, :] = _silu(y).astype(BF16)


def _conv(u, w_dw, b_dw, ln_g, ln_b):
    nblk = N_TOK // CONV_T
    hpb = CONV_T // CONV_HALO
    last = N_TOK // CONV_HALO - 1
    row = lambda i: (0, 0)
    return pl.pallas_call(
        _conv_kernel,
        grid=(nblk,),
        in_specs=[pl.BlockSpec((CONV_T, 2 * CONV_W), lambda i: (i, 0)),
                  pl.BlockSpec((CONV_HALO, 2 * CONV_W), lambda i: (jnp.maximum(i * hpb - 1, 0), 0)),
                  pl.BlockSpec((CONV_HALO, 2 * CONV_W), lambda i: (jnp.minimum((i + 1) * hpb, last), 0)),
                  pl.BlockSpec((CONV_K, CONV_W), row),
                  pl.BlockSpec((1, CONV_W), row),
                  pl.BlockSpec((1, CONV_W), row),
                  pl.BlockSpec((1, CONV_W), row)],
        out_specs=pl.BlockSpec((CONV_T, CONV_W), lambda i: (i, 0)),
        out_shape=jax.ShapeDtypeStruct((N_TOK, CONV_W), BF16),
        scratch_shapes=[pltpu.VMEM((CONV_T + 2 * CONV_HALO, CONV_W), F32),
                        pltpu.VMEM((SUBLANES, CONV_ZS, CONV_W), F32)],
        compiler_params=_params(("arbitrary",)),
        name="conv",
    )(u, u, u, w_dw, b_dw, ln_g, ln_b)


def _stack_heads(q):
    return jnp.concatenate([q[:, g * HD:(g + 1) * HD] for g in range(Q_GROUP)], axis=0)


def _unstack_heads(o, rows):
    return jnp.concatenate([o[g * rows:(g + 1) * rows, :] for g in range(Q_GROUP)], axis=1)


LOG2E = 1.4426950408889634
Q_SCALE = HD ** -0.5 * LOG2E
QW = Q_GROUP * HD


def _sink_column(sink_ref, kv, rows):
    rid = lax.broadcasted_iota(jnp.int32, (Q_GROUP * rows, 1), 0)
    col = jnp.full((Q_GROUP * rows, 1), sink_ref[kv * Q_GROUP], F32)
    for g in range(1, Q_GROUP):
        col = jnp.where(rid >= g * rows, sink_ref[kv * Q_GROUP + g], col)
    return col * LOG2E


def _softmax_av(s, sk, vals):
    m = jnp.maximum(jnp.max(s, axis=-1, keepdims=True), sk)
    p = jnp.exp2(s - m)
    den = jnp.sum(p, axis=-1, keepdims=True) + jnp.exp2(sk - m)
    o = jnp.dot(p.astype(BF16), vals, preferred_element_type=F32)
    return o / den


_NT = (((1,), (1,)), ((), ()))


def _ctx_attn_kernel(sink_ref, q_ref, k_ref, v_ref, o_ref, sk_ref, sv_ref):
    for kv in range(N_KV):
        k = k_ref[:, kv * HD:(kv + 1) * HD]
        v = v_ref[:, kv * HD:(kv + 1) * HD]
        sk_ref[0, 0, :, kv, :] = k
        sv_ref[0, 0, :, kv, :] = v
        q = (_stack_heads(q_ref[:, kv * QW:(kv + 1) * QW]) * Q_SCALE).astype(BF16)
        s = lax.dot_general(q, k.astype(BF16), _NT, preferred_element_type=F32)
        o = _softmax_av(s, _sink_column(sink_ref, kv, CTX_LEN), v.astype(BF16))
        o_ref[:, kv * QW:(kv + 1) * QW] = _unstack_heads(o, CTX_LEN).astype(BF16)


def _ctx_attn(u, sink):
    qcols, kcols = N_HEADS * HD, N_KV * HD
    state_spec = pl.BlockSpec((1, 1, CTX_LEN, N_KV, HD), lambda b, s: (b, 0, 0, 0, 0))
    state_shape = jax.ShapeDtypeStruct((N_CTX_SEQ, 1, CTX_LEN, N_KV, HD), F32)
    return pl.pallas_call(
        _ctx_attn_kernel,
        grid_spec=pltpu.PrefetchScalarGridSpec(
            num_scalar_prefetch=1,
            grid=(N_CTX_SEQ,),
            in_specs=[pl.BlockSpec((CTX_LEN, qcols), lambda b, s: (b, COL_Q // qcols)),
                      pl.BlockSpec((CTX_LEN, kcols), lambda b, s: (b, COL_K // kcols)),
                      pl.BlockSpec((CTX_LEN, kcols), lambda b, s: (b, COL_V // kcols))],
            out_specs=[pl.BlockSpec((CTX_LEN, qcols), lambda b, s: (b, 0)),
                       state_spec, state_spec]),
        out_shape=[jax.ShapeDtypeStruct((N_CTX, N_HEADS * HD), BF16), state_shape, state_shape],
        compiler_params=_params(("arbitrary",)),
        name="ctx_attn",
    )(sink, u, u, u)


def _rope(x, cos, sin):
    lane = lax.broadcasted_iota(jnp.int32, x.shape, 1)
    fwd = pltpu.roll(x, HD - 32, axis=1)
    bwd = pltpu.roll(x, 32, axis=1)
    partner = jnp.where((lane & 32) == 0, fwd, bwd)
    return x * cos + partner * sin


def _lat_attn_kernel(sink_ref, q_ref, k_ref, v_ref, ck_ref, cv_ref, cos_ref, sin_ref, o_ref):
    kv = pl.program_id(1)
    nblk = LAT_LEN // ABLK
    cos, sin = cos_ref[...], sin_ref[...]
    k_rot = _rope(k_ref[...], cos, sin).astype(BF16)
    v = v_ref[...].astype(BF16)
    ck = ck_ref[...].astype(BF16)
    cv = cv_ref[...].astype(BF16)
    sk = _sink_column(sink_ref, kv, ABLK)
    r = lax.broadcasted_iota(jnp.int32, (Q_GROUP * ABLK, ABLK), 0) & (ABLK - 1)
    c = lax.broadcasted_iota(jnp.int32, (Q_GROUP * ABLK, ABLK), 1)
    open_tile = jnp.zeros((Q_GROUP * ABLK, ABLK), F32)
    below = jnp.where(c >= r, 0.0, NEG)
    above = jnp.where(c <= r, 0.0, NEG)
    ctx_open = jnp.zeros((Q_GROUP * ABLK, PAST), F32)
    bias_first = jnp.concatenate([open_tile, above, ctx_open], axis=1)
    bias_mid = jnp.concatenate([below, open_tile, above, ctx_open], axis=1)
    bias_last = jnp.concatenate([below, open_tile, ctx_open], axis=1)
    for j in range(nblk):
        rows = slice(j * ABLK, (j + 1) * ABLK)
        cos_j, sin_j = cos[rows], sin[rows]
        qs = jnp.concatenate(
            [_rope(q_ref[rows, g * HD:(g + 1) * HD], cos_j, sin_j) * Q_SCALE for g in range(Q_GROUP)],
            axis=0).astype(BF16)
        blocks = range(max(j - 1, 0), min(j + 2, nblk))
        band = slice(blocks[0] * ABLK, (blocks[-1] + 1) * ABLK)
        keys = jnp.concatenate([k_rot[band], ck], axis=0)
        vals = jnp.concatenate([v[band], cv], axis=0)
        bias = bias_first if j == 0 else bias_last if j == nblk - 1 else bias_mid
        s = lax.dot_general(qs, keys, _NT, preferred_element_type=F32) + bias
        o = _softmax_av(s, sk, vals)
        o_ref[rows, :] = _unstack_heads(o, ABLK).astype(BF16)


def _lat_attn(u, sink, ck, cv, cos, sin):
    base = N_CTX // LAT_LEN
    cache_spec = pl.BlockSpec((None, PAST, HD), lambda b, kv, s: (b, 0, kv))
    table_spec = pl.BlockSpec((LAT_LEN, HD), lambda b, kv, s: (0, 0))
    return pl.pallas_call(
        _lat_attn_kernel,
        grid_spec=pltpu.PrefetchScalarGridSpec(
            num_scalar_prefetch=1,
            grid=(N_LAT_SEQ, N_KV),
            in_specs=[pl.BlockSpec((LAT_LEN, QW), lambda b, kv, s: (base + b, COL_Q // QW + kv)),
                      pl.BlockSpec((LAT_LEN, HD), lambda b, kv, s: (base + b, COL_K // HD + kv)),
                      pl.BlockSpec((LAT_LEN, HD), lambda b, kv, s: (base + b, COL_V // HD + kv)),
                      cache_spec, cache_spec, table_spec, table_spec],
            out_specs=pl.BlockSpec((LAT_LEN, QW), lambda b, kv, s: (b, kv))),
        out_shape=jax.ShapeDtypeStruct((N_LAT, N_HEADS * HD), BF16),
        compiler_params=_params(("arbitrary", "arbitrary")),
        name="lat_attn",
    )(sink, u, u, u, ck, cv, cos, sin)


def _rope_tables():
    n = jnp.arange(LAT_LEN)
    row = (n // GRID_W).astype(F32)
    col = (n % GRID_W).astype(F32)
    quarter = HD // 4
    inv = ROPE_BASE ** (-jnp.arange(quarter, dtype=F32) / quarter)
    ang_r = row[:, None] * inv[None, :]
    ang_c = col[:, None] * inv[None, :]
    cos = jnp.concatenate([jnp.cos(ang_r)] * 2 + [jnp.cos(ang_c)] * 2, axis=-1)
    sin = jnp.concatenate([-jnp.sin(ang_r), jnp.sin(ang_r), -jnp.sin(ang_c), jnp.sin(ang_c)], axis=-1)
    return cos, sin


def _mix_kernel(z_ref, ac_ref, al_ref, ga_ref, gb_ref, wc_ref, wa_ref, o_ref, wcb_ref, wab_ref):
    @pl.when(pl.program_id(1) == 0)
    def _():
        wcb_ref[...] = wc_ref[...].astype(BF16)
        wab_ref[...] = wa_ref[...].astype(BF16)

    is_ctx = pl.program_id(1) < N_CTX // z_ref.shape[0]
    att = jnp.where(is_ctx, ac_ref[...], al_ref[...])
    conv_out = jnp.dot(z_ref[...], wcb_ref[...], preferred_element_type=F32)
    attn_out = jnp.dot(att, wab_ref[...], preferred_element_type=F32)
    mixed = jax.nn.sigmoid(ga_ref[...]) * conv_out + jax.nn.sigmoid(gb_ref[...]) * attn_out
    o_ref[...] = mixed.astype(BF16)


def _mix(zc, att_ctx, att_lat, u, w_conv_out, w_attn_out):
    tm, tn = 512, 1024
    nc = N_CTX // tm
    return pl.pallas_call(
        _mix_kernel,
        grid=(D // tn, N_TOK // tm),
        in_specs=[pl.BlockSpec((tm, CONV_W), lambda j, i: (i, 0)),
                  pl.BlockSpec((tm, N_HEADS * HD), lambda j, i: (jnp.minimum(i, nc - 1), 0)),
                  pl.BlockSpec((tm, N_HEADS * HD), lambda j, i: (jnp.maximum(i - nc, 0), 0)),
                  pl.BlockSpec((tm, tn), lambda j, i: (i, COL_GA // tn + j)),
                  pl.BlockSpec((tm, tn), lambda j, i: (i, COL_GB // tn + j)),
                  pl.BlockSpec((CONV_W, tn), lambda j, i: (0, j)),
                  pl.BlockSpec((N_HEADS * HD, tn), lambda j, i: (0, j))],
        out_specs=pl.BlockSpec((tm, tn), lambda j, i: (i, j)),
        out_shape=jax.ShapeDtypeStruct((N_TOK, D), BF16),
        scratch_shapes=[pltpu.VMEM((CONV_W, tn), BF16), pltpu.VMEM((N_HEADS * HD, tn), BF16)],
        compiler_params=_params(("arbitrary", "arbitrary")),
        name="mix",
    )(zc, att_ctx, att_lat, u, u, w_conv_out, w_attn_out)


def _split_bf16(x):
    hi = x.astype(BF16)
    lo = (x - hi.astype(F32)).astype(BF16)
    return hi, lo


def _outproj_kernel(mx_ref, xc_ref, xl_ref, mod_ref, g_ref, w_ref, wr_ref, br_ref,
                    x1_ref, h2_ref, lg_ref):
    is_ctx = pl.program_id(0) < N_CTX // xc_ref.shape[0]
    x = jnp.where(is_ctx, xc_ref[...], xl_ref[...])
    m = mod_ref[0]
    x1 = x + m[2:3, :] * jnp.dot(mx_ref[...], w_ref[...], preferred_element_type=F32)
    x1_ref[...] = x1
    h2 = _rms(x1, g_ref[...]) * (1.0 + m[4:5, :]) + m[3:4, :]
    h2_ref[...] = h2
    h_hi, h_lo = _split_bf16(h2)
    w_hi, w_lo = _split_bf16(wr_ref[...])
    lg = (jnp.dot(h_hi, w_hi, preferred_element_type=F32)
          + jnp.dot(h_hi, w_lo, preferred_element_type=F32)
          + jnp.dot(h_lo, w_hi, preferred_element_type=F32))
    lg_ref[...] = lg + br_ref[...]


def _outproj(mixed, x_ctx, x_lat, mod, n2g, w_out_bf, w_router, b_router):
    tm = 512
    const = lambda i: (0, 0)
    xc_spec, xl_spec = _set_specs(tm, 2)
    return pl.pallas_call(
        _outproj_kernel,
        grid=(N_TOK // tm,),
        in_specs=[pl.BlockSpec((tm, D), lambda i: (i, 0)),
                  xc_spec, xl_spec,
                  pl.BlockSpec((1, N_MOD, D), lambda i: (_mod_row(i, tm), 0, 0)),
                  pl.BlockSpec((1, D), const),
                  pl.BlockSpec((D, D), const, pipeline_mode=pl.Buffered(1)),
                  pl.BlockSpec((D, N_EXP), const),
                  pl.BlockSpec((1, N_EXP), const)],
        out_specs=[pl.BlockSpec((tm, D), lambda i: (i, 0)),
                   pl.BlockSpec((tm, D), lambda i: (i, 0)),
                   pl.BlockSpec((tm, N_EXP), lambda i: (i, 0))],
        out_shape=[jax.ShapeDtypeStruct((N_TOK, D), F32),
                   jax.ShapeDtypeStruct((N_TOK, D), F32),
                   jax.ShapeDtypeStruct((N_TOK, N_EXP), F32)],
        compiler_params=_params(("arbitrary",)),
        name="outproj",
    )(mixed, x_ctx, x_lat, mod, n2g, w_out_bf, w_router, b_router)


SEL_CHUNK = 128


def _prefix_sum(flags, out_ref):
    n = flags.shape[1]
    ri = lax.broadcasted_iota(jnp.int32, (SEL_CHUNK, SEL_CHUNK), 0)
    ci = lax.broadcasted_iota(jnp.int32, (SEL_CHUNK, SEL_CHUNK), 1)
    tri = jnp.where(ri <= ci, 1.0, 0.0).astype(BF16)
    run = jnp.zeros((N_EXP, 1), F32)
    for ch in range(n // SEL_CHUNK):
        blk = flags[:, ch * SEL_CHUNK:(ch + 1) * SEL_CHUNK].astype(BF16)
        within = jnp.dot(blk, tri, preferred_element_type=F32) + run
        out_ref[:, ch * SEL_CHUNK:(ch + 1) * SEL_CHUNK] = within
        run = within[:, SEL_CHUNK - 1:SEL_CHUNK]


def _select_kernel(lg_ref, idx_ref, gate_ref, cs_ref):
    n = lg_ref.shape[1]
    lg = lg_ref[...]
    ex = jnp.exp(lg - jnp.max(lg, axis=0, keepdims=True))
    aff = ex / jnp.sum(ex, axis=0, keepdims=True)

    def bit_step(b, cur):
        cand = cur | (1 << (30 - b))
        cnt = jnp.sum(jnp.where(aff >= pltpu.bitcast(cand, F32), 1.0, 0.0), axis=1, keepdims=True)
        return jnp.where(cnt >= CAP, cand, cur)

    thr = pltpu.bitcast(lax.fori_loop(0, 31, bit_step, jnp.zeros((N_EXP, 1), jnp.int32)), F32)
    above = aff > thr
    tied = aff == thr
    room = CAP - jnp.sum(jnp.where(above, 1.0, 0.0), axis=1, keepdims=True)
    _prefix_sum(jnp.where(tied, 1.0, 0.0), cs_ref)
    sel = jnp.logical_or(above, jnp.logical_and(tied, cs_ref[...] <= room))
    gsel = jnp.where(sel, aff, 0.0)
    _prefix_sum(jnp.where(sel, 1.0, 0.0), cs_ref)

    slot = lax.broadcasted_iota(jnp.int32, (CAP, n), 0).astype(F32)
    g_hi = gsel.astype(BF16).astype(F32)
    rest = gsel - g_hi
    g_mid = rest.astype(BF16).astype(F32)
    g_lo = rest - g_mid
    ones = jnp.ones((1, n), F32)
    pad = jnp.zeros((4, n), F32)
    last_slot = lax.broadcasted_iota(jnp.int32, (1, CAP), 1) == CAP - 1
    total = jnp.sum(gsel, axis=1, keepdims=True)
    for e in range(N_EXP):
        row = slice(e, e + 1)
        le = jnp.where(cs_ref[row, :] <= slot, 1.0, 0.0).astype(BF16)
        lhs = jnp.concatenate([ones, g_hi[row], g_mid[row], g_lo[row], pad], axis=0).astype(BF16)
        res = lax.dot_general(lhs, le, (((1,), (1,)), ((), ())), preferred_element_type=F32)
        idx_ref[0, row, :] = res[0:1, :].astype(jnp.int32)
        cum = res[1:2, :] + res[2:3, :] + res[3:4, :]
        gate_ref[0, row, :] = jnp.where(last_slot, total[row], pltpu.roll(cum, CAP - 1, axis=1)) - cum


def _select(logits_t):
    return pl.pallas_call(
        _select_kernel,
        grid=(N_SETS,),
        in_specs=[pl.BlockSpec((N_EXP, N_CTX), lambda s: (0, s))],
        out_specs=[pl.BlockSpec((1, N_EXP, CAP), lambda s: (s, 0, 0)),
                   pl.BlockSpec((1, N_EXP, CAP), lambda s: (s, 0, 0))],
        out_shape=[jax.ShapeDtypeStruct((N_SETS, N_EXP, CAP), jnp.int32),
                   jax.ShapeDtypeStruct((N_SETS, N_EXP, CAP), F32)],
        scratch_shapes=[pltpu.VMEM((N_EXP, N_CTX), F32)],
        compiler_params=_params(("arbitrary",)),
        name="select",
    )(logits_t)


MOE_ROWS = N_SETS * CAP
MOE_TF = 256
MOE_NF = D_EXP // MOE_TF
MOE_TN = 512
N_FILL = N_TOK // MOE_ROWS
assert MOE_NF == 4


def _moe_kernel(rows_ref, h_hbm, gate_ref, wg_ref, wu_ref, wd_ref, y_hbm,
                xf_ref, xb_ref, acc_ref, rmw_ref, sem_x, sem_put, sem_get, sem_fill):
    e = pl.program_id(0)
    f = pl.program_id(1)
    e_next = jnp.minimum(e + 1, N_EXP - 1)

    def tok_row(hbm_ref, ex, k):
        return hbm_ref.at[pl.ds(rows_ref[ex * MOE_ROWS + k], 1), :]

    def x_copy(ex, k):
        return pltpu.make_async_copy(tok_row(h_hbm, ex, k), xf_ref.at[pl.ds(k, 1), :], sem_x)

    def put_copy(ex, k):
        return pltpu.make_async_copy(rmw_ref.at[pl.ds(k, 1), :], tok_row(y_hbm, ex, k), sem_put)

    def get_copy(ex, k):
        return pltpu.make_async_copy(tok_row(y_hbm, ex, k), rmw_ref.at[pl.ds(k, 1), :], sem_get)

    def fill_copy(c):
        return pltpu.make_async_copy(rmw_ref, y_hbm.at[pl.ds(c * MOE_ROWS, MOE_ROWS), :], sem_fill)

    def start_rows(copy, ex):
        for k in range(MOE_ROWS):
            copy(ex, k).start(priority=k % 2)

    def wait_rows(sem, src, dst):
        pltpu.make_async_copy(src, dst, sem).wait()

    y_rows = y_hbm.at[pl.ds(0, MOE_ROWS), :]
    h_rows = h_hbm.at[pl.ds(0, MOE_ROWS), :]

    def ffn_step(first=False):
        x = xb_ref[...]
        g = jnp.dot(x, wg_ref[...].astype(BF16), preferred_element_type=F32)
        up = jnp.dot(x, wu_ref[...].astype(BF16), preferred_element_type=F32)
        hid = (_silu(g) * up).astype(BF16)
        wd = wd_ref[...].astype(BF16)
        for c in range(D // MOE_TN):
            cols = slice(c * MOE_TN, (c + 1) * MOE_TN)
            part = jnp.dot(hid, wd[:, cols], preferred_element_type=F32)
            if first:
                acc_ref[:, cols] = part
            else:
                acc_ref[:, cols] += part

    @pl.when(jnp.logical_and(e == 0, f == 0))
    def _():
        rmw_ref[...] = jnp.zeros(rmw_ref.shape, F32)
        for c in range(N_FILL):
            fill_copy(c).start()

        def issue(k, c):
            x_copy(0, k).start()
            return c
        lax.fori_loop(0, MOE_ROWS, issue, 0)

    @pl.when(f == 0)
    def _():
        wait_rows(sem_x, h_rows, xf_ref)
        xb_ref[...] = xf_ref[...].astype(BF16)

    @pl.when(jnp.logical_and(e > 0, f == 0))
    def _():
        start_rows(put_copy, e - 1)
        ffn_step(first=True)

    @pl.when(jnp.logical_and(e == 0, f == 0))
    def _():
        ffn_step(first=True)

    @pl.when(f == 1)
    def _():
        start_rows(x_copy, e_next)
        ffn_step()

    @pl.when(jnp.logical_and(e == 0, f == 2))
    def _():
        for c in range(N_FILL):
            fill_copy(c).wait()

    @pl.when(jnp.logical_and(e > 0, f == 2))
    def _():
        wait_rows(sem_put, rmw_ref, y_rows)

    @pl.when(f == 2)
    def _():
        start_rows(get_copy, e)
        ffn_step()

    @pl.when(f == 3)
    def _():
        ffn_step()
        wait_rows(sem_get, y_rows, rmw_ref)
        rmw_ref[...] = rmw_ref[...] + gate_ref[...] * acc_ref[...]

    @pl.when(jnp.logical_and(e == N_EXP - 1, f == MOE_NF - 1))
    def _():
        def issue(k, c):
            put_copy(N_EXP - 1, k).start()
            return c
        lax.fori_loop(0, MOE_ROWS, issue, 0)
        wait_rows(sem_put, rmw_ref, y_rows)
        wait_rows(sem_x, h_rows, xf_ref)


def _moe(rows, h2, gates, w_g, w_u, w_d):
    return pl.pallas_call(
        _moe_kernel,
        grid_spec=pltpu.PrefetchScalarGridSpec(
            num_scalar_prefetch=1,
            grid=(N_EXP, MOE_NF),
            in_specs=[pl.BlockSpec(memory_space=pl.ANY),
                      pl.BlockSpec((None, MOE_ROWS, 1), lambda e, f, s: (e, 0, 0)),
                      pl.BlockSpec((None, D, MOE_TF), lambda e, f, s: (e, 0, f)),
                      pl.BlockSpec((None, D, MOE_TF), lambda e, f, s: (e, 0, f)),
                      pl.BlockSpec((None, MOE_TF, D), lambda e, f, s: (e, f, 0))],
            out_specs=pl.BlockSpec(memory_space=pl.ANY),
            scratch_shapes=[pltpu.VMEM((MOE_ROWS, D), F32),
                            pltpu.VMEM((MOE_ROWS, D), BF16),
                            pltpu.VMEM((MOE_ROWS, D), F32),
                            pltpu.VMEM((MOE_ROWS, D), F32),
                            pltpu.SemaphoreType.DMA(()),
                            pltpu.SemaphoreType.DMA(()),
                            pltpu.SemaphoreType.DMA(()),
                            pltpu.SemaphoreType.DMA(())]),
        out_shape=jax.ShapeDtypeStruct((N_TOK, D), F32),
        compiler_params=_params(("arbitrary", "arbitrary")),
        name="moe",
    )(rows, h2, gates, w_g, w_u, w_d)


def _final_kernel(x1_ref, y_ref, mod_ref, g_ref, o_ref):
    o_ref[...] = _rms(x1_ref[...] + mod_ref[0][5:6, :] * y_ref[...], g_ref[...])


def _final(x1, y, tok0, n_tok, mod, fng):
    tm = 512
    blk0 = tok0 // tm
    return pl.pallas_call(
        _final_kernel,
        grid=(n_tok // tm,),
        in_specs=[pl.BlockSpec((tm, D), lambda i: (blk0 + i, 0)),
                  pl.BlockSpec((tm, D), lambda i: (blk0 + i, 0)),
                  pl.BlockSpec((1, N_MOD, D), lambda i: (_mod_row(blk0 + i, tm), 0, 0)),
                  pl.BlockSpec((1, D), lambda i: (0, 0))],
        out_specs=pl.BlockSpec((tm, D), lambda i: (i, 0)),
        out_shape=jax.ShapeDtypeStruct((n_tok, D), F32),
        compiler_params=_params(("arbitrary",)),
        name="final",
    )(x1, y, mod, fng)


def kernel(x_prompt, x_sample, c, cache_k, cache_v, c_ctx, w_ada, b_ada, norm1_g, w_in, conv_dw, conv_dw_b, conv_ln_g, conv_ln_b, w_conv_out, attn_sink, w_attn_out, w_out, norm2_g, w_router, b_router, w_exp_gate, w_exp_up, w_exp_down, final_norm_g):
    x_ctx = x_prompt.reshape(N_CTX, D)
    x_lat = x_sample.reshape(N_LAT, D)
    cond = jnp.concatenate([c, c_ctx[None, :], jnp.zeros((MOD_ROWS - N_LAT_SEQ - 1, D), F32)], axis=0)
    mod = _ada(cond, w_ada[0], b_ada[0]).reshape(MOD_ROWS, N_MOD, D)

    u = _inproj(_prenorm(x_ctx, x_lat, mod, norm1_g), w_in[0])
    zc = _conv(u, conv_dw[0], conv_dw_b, conv_ln_g, conv_ln_b)

    sink = attn_sink[0]
    cos, sin = _rope_tables()
    ck = cache_k[:, 0].reshape(N_LAT_SEQ, PAST, N_KV * HD)
    cv = cache_v[:, 0].reshape(N_LAT_SEQ, PAST, N_KV * HD)
    att_ctx, state_k, state_v = _ctx_attn(u, sink)
    mixed = _mix(zc, att_ctx, _lat_attn(u, sink, ck, cv, cos, sin), u,
                 w_conv_out[0], w_attn_out[0])

    x1, h2, logits = _outproj(mixed, x_ctx, x_lat, mod, norm2_g, w_out[0].astype(BF16),
                              w_router[0], b_router)

    idx, gates = _select(logits.T)
    rows = (idx + (jnp.arange(N_SETS, dtype=jnp.int32) * N_CTX)[:, None, None])
    rows = rows.transpose(1, 0, 2).reshape(N_EXP * MOE_ROWS)
    gates = gates.transpose(1, 0, 2).reshape(N_EXP, MOE_ROWS, 1)
    ffn = _moe(rows, h2, gates, w_exp_gate[0], w_exp_up[0], w_exp_down[0])

    fng = final_norm_g.reshape(1, D)
    y_prompt = _final(x1, ffn, 0, N_CTX, mod, fng).reshape(N_CTX_SEQ, CTX_LEN, D)
    y_sample = _final(x1, ffn, N_CTX, N_LAT, mod, fng).reshape(N_LAT_SEQ, LAT_LEN, D)
    return (y_prompt, y_sample, state_k, state_v)
```

```python
import functools

import jax
import jax.numpy as jnp
from jax import lax
from jax.experimental import pallas as pl
from jax.experimental.pallas import tpu as pltpu

D = 2048
N_CTX_SEQ, CTX_LEN = 16, 256
N_LAT_SEQ, LAT_LEN = 4, 1024
N_CTX = N_CTX_SEQ * CTX_LEN
N_LAT = N_LAT_SEQ * LAT_LEN
N_TOK = N_CTX + N_LAT
PAST = 512
GRID_W = 64
N_HEADS, N_KV, HD = 16, 4, 128
Q_GROUP = N_HEADS // N_KV
WINDOW = 128
ABLK = 128
CONV_W = 1024
CONV_K = 31
CONV_PAD = CONV_K // 2
N_EXP = 16
D_EXP = 1024
CAP = 2 * N_CTX // N_EXP
N_SETS = 2
ROPE_BASE = 10000.0
EPS = 1e-6
N_MOD = 6
NEG = -1e30
IN_COLS = 2 * CONV_W + N_HEADS * HD + 2 * N_KV * HD + 2 * D
COL_Q = 2 * CONV_W
COL_K = COL_Q + N_HEADS * HD
COL_V = COL_K + N_KV * HD
COL_GA = COL_V + N_KV * HD
COL_GB = COL_GA + D
MOD_ROWS = 8
CTX_MOD_ROW = N_LAT_SEQ

VMEM_LIMIT = 58 * 1024 * 1024
F32 = jnp.float32
BF16 = jnp.bfloat16


def _params(sem, vmem=VMEM_LIMIT):
    return pltpu.CompilerParams(dimension_semantics=sem, vmem_limit_bytes=vmem)


def _mod_row(tok_block, block_tokens):
    ctx_blocks = N_CTX // block_tokens
    per_seq = LAT_LEN // block_tokens
    return jnp.where(tok_block < ctx_blocks, CTX_MOD_ROW, (tok_block - ctx_blocks) // per_seq)


def _silu(x):
    return x * jax.nn.sigmoid(x)


def _rms(x, g):
    return x * lax.rsqrt(jnp.mean(x * x, axis=-1, keepdims=True) + EPS) * g


def _ada_kernel(cond_ref, w_ref, b_ref, o_ref):
    a = _silu(cond_ref[...]).astype(BF16)
    o_ref[...] = jnp.dot(a, w_ref[...].astype(BF16), preferred_element_type=F32) + b_ref[...]


def _ada(cond, w_ada, b_ada):
    tn = 1024
    n = N_MOD * D
    return pl.pallas_call(
        _ada_kernel,
        grid=(n // tn,),
        in_specs=[pl.BlockSpec((MOD_ROWS, D), lambda j: (0, 0)),
                  pl.BlockSpec((D, tn), lambda j: (0, j)),
                  pl.BlockSpec((1, tn), lambda j: (0, j))],
        out_specs=pl.BlockSpec((MOD_ROWS, tn), lambda j: (0, j)),
        out_shape=jax.ShapeDtypeStruct((MOD_ROWS, n), F32),
        compiler_params=_params(("arbitrary",)),
        name="ada",
    )(cond, w_ada, b_ada.reshape(1, n))


def _set_specs(tm, buffers):
    nc = N_CTX // tm
    mode = pl.Buffered(buffers)
    ctx = pl.BlockSpec((tm, D), lambda i, *_: (jnp.minimum(i, nc - 1), 0), pipeline_mode=mode)
    lat = pl.BlockSpec((tm, D), lambda i, *_: (jnp.maximum(i - nc, 0), 0), pipeline_mode=mode)
    return ctx, lat


def _prenorm_kernel(xc_ref, xl_ref, mod_ref, g_ref, h_ref):
    is_ctx = pl.program_id(0) < N_CTX // xc_ref.shape[0]
    x = jnp.where(is_ctx, xc_ref[...], xl_ref[...])
    m = mod_ref[0]
    h = _rms(x, g_ref[...]) * (1.0 + m[1:2, :]) + m[0:1, :]
    h_ref[...] = h.astype(BF16)


def _prenorm(x_ctx, x_lat, mod, n1g):
    tm = 512
    xc_spec, xl_spec = _set_specs(tm, 2)
    return pl.pallas_call(
        _prenorm_kernel,
        grid=(N_TOK // tm,),
        in_specs=[xc_spec, xl_spec,
                  pl.BlockSpec((1, N_MOD, D), lambda i: (_mod_row(i, tm), 0, 0)),
                  pl.BlockSpec((1, D), lambda i: (0, 0))],
        out_specs=pl.BlockSpec((tm, D), lambda i: (i, 0)),
        out_shape=jax.ShapeDtypeStruct((N_TOK, D), BF16),
        compiler_params=_params(("arbitrary",)),
        name="prenorm",
    )(x_ctx, x_lat, mod, n1g)


def _inproj_kernel(h_ref, w_ref, o_ref, wb_ref):
    @pl.when(pl.program_id(1) == 0)
    def _():
        wb_ref[...] = w_ref[...].astype(BF16)

    o_ref[...] = jnp.dot(h_ref[...], wb_ref[...], preferred_element_type=F32)


def _inproj(h, w_in):
    tm, tn = 1024, 1024
    return pl.pallas_call(
        _inproj_kernel,
        grid=(IN_COLS // tn, N_TOK // tm),
        in_specs=[pl.BlockSpec((tm, D), lambda j, i: (i, 0)),
                  pl.BlockSpec((D, tn), lambda j, i: (0, j))],
        out_specs=pl.BlockSpec((tm, tn), lambda j, i: (i, j)),
        out_shape=jax.ShapeDtypeStruct((N_TOK, IN_COLS), F32),
        scratch_shapes=[pltpu.VMEM((D, tn), BF16)],
        compiler_params=_params(("arbitrary", "arbitrary")),
        name="inproj",
    )(h, w_in)


CONV_T = 256
CONV_HALO = 16
CONV_ROWS = 32
SUBLANES = 8
LANES = 128
CONV_ZS = CONV_T + (CONV_K // SUBLANES) * SUBLANES


def _conv_kernel(main_ref, prev_ref, next_ref, w_ref, b_ref, lg_ref, lb_ref, o_ref,
                 z_ref, zs_ref, y_ref):
    i = pl.program_id(0)
    ctx_blocks = N_CTX // CONV_T
    per_seq = LAT_LEN // CONV_T
    pos = (i - ctx_blocks) % per_seq
    is_lat = i >= ctx_blocks
    has_prev = jnp.logical_and(is_lat, pos > 0)
    has_next = jnp.logical_and(is_lat, pos < per_seq - 1)

    def glu(ref):
        v = ref[...]
        return v[:, :CONV_W] * jax.nn.sigmoid(v[:, CONV_W:])

    z_ref[0:CONV_HALO, :] = jnp.where(has_prev, glu(prev_ref), 0.0)
    z_ref[CONV_HALO:CONV_HALO + CONV_T, :] = glu(main_ref)
    z_ref[CONV_HALO + CONV_T:, :] = jnp.where(has_next, glu(next_ref), 0.0)

    z = z_ref[...]
    zs_ref[0] = z[:CONV_ZS]
    for s in range(1, SUBLANES):
        zs_ref[s] = pltpu.roll(z, z.shape[0] - s, axis=0)[:CONV_ZS]

    def channel_group(c, carry):
        cols = pl.ds(pl.multiple_of(c * LANES, LANES), LANES)
        acc = jnp.zeros((CONV_T, LANES), F32)
        for j in range(CONV_K):
            a, s = divmod(CONV_HALO - CONV_PAD + j, SUBLANES)
            acc = acc + zs_ref[s, a * SUBLANES:a * SUBLANES + CONV_T, cols] * w_ref[j:j + 1, cols]
        y_ref[:, cols] = acc + b_ref[:, cols]
        return carry

    lax.fori_loop(0, CONV_W // LANES, channel_group, 0)

    for r in range(CONV_T // CONV_ROWS):
        acc = y_ref[r * CONV_ROWS:(r + 1) * CONV_ROWS, :]
        mu = jnp.mean(acc, axis=-1, keepdims=True)
        cen = acc - mu
        var = jnp.mean(cen * cen, axis=-1, keepdims=True)
        y = cen * lax.rsqrt(var + EPS) * lg_ref[...] + lb_ref[...]
        o_ref[r * CONV_ROWS:(r + 1) * CONV_ROWS, :] = _silu(y).astype(BF16)


def _conv(u, w_dw, b_dw, ln_g, ln_b):
    nblk = N_TOK // CONV_T
    hpb = CONV_T // CONV_HALO
    last = N_TOK // CONV_HALO - 1
    row = lambda i: (0, 0)
    return pl.pallas_call(
        _conv_kernel,
        grid=(nblk,),
        in_specs=[pl.BlockSpec((CONV_T, 2 * CONV_W), lambda i: (i, 0)),
                  pl.BlockSpec((CONV_HALO, 2 * CONV_W), lambda i: (jnp.maximum(i * hpb - 1, 0), 0)),
                  pl.BlockSpec((CONV_HALO, 2 * CONV_W), lambda i: (jnp.minimum((i + 1) * hpb, last), 0)),
                  pl.BlockSpec((CONV_K, CONV_W), row),
                  pl.BlockSpec((1, CONV_W), row),
                  pl.BlockSpec((1, CONV_W), row),
                  pl.BlockSpec((1, CONV_W), row)],
        out_specs=pl.BlockSpec((CONV_T, CONV_W), lambda i: (i, 0)),
        out_shape=jax.ShapeDtypeStruct((N_TOK, CONV_W), BF16),
        scratch_shapes=[pltpu.VMEM((CONV_T + 2 * CONV_HALO, CONV_W), F32),
                        pltpu.VMEM((SUBLANES, CONV_ZS, CONV_W), F32),
                        pltpu.VMEM((CONV_T, CONV_W), F32)],
        compiler_params=_params(("arbitrary",)),
        name="conv",
    )(u, u, u, w_dw, b_dw, ln_g, ln_b)


def _stack_heads(q):
    return jnp.concatenate([q[:, g * HD:(g + 1) * HD] for g in range(Q_GROUP)], axis=0)


def _unstack_heads(o, rows):
    return jnp.concatenate([o[g * rows:(g + 1) * rows, :] for g in range(Q_GROUP)], axis=1)


LOG2E = 1.4426950408889634
Q_SCALE = HD ** -0.5 * LOG2E
QW = Q_GROUP * HD


def _sink_column(sink_ref, kv, rows):
    rid = lax.broadcasted_iota(jnp.int32, (Q_GROUP * rows, 1), 0)
    col = jnp.full((Q_GROUP * rows, 1), sink_ref[kv * Q_GROUP], F32)
    for g in range(1, Q_GROUP):
        col = jnp.where(rid >= g * rows, sink_ref[kv * Q_GROUP + g], col)
    return col * LOG2E


def _softmax_av(s, sk, vals):
    m = jnp.maximum(jnp.max(s, axis=-1, keepdims=True), sk)
    p = jnp.exp2(s - m)
    den = jnp.sum(p, axis=-1, keepdims=True) + jnp.exp2(sk - m)
    o = jnp.dot(p.astype(BF16), vals, preferred_element_type=F32)
    return o / den


_NT = (((1,), (1,)), ((), ()))


def _ctx_attn_kernel(sink_ref, q_ref, k_ref, v_ref, o_ref, sk_ref, sv_ref):
    for kv in range(N_KV):
        k = k_ref[:, kv * HD:(kv + 1) * HD]
        v = v_ref[:, kv * HD:(kv + 1) * HD]
        sk_ref[0, 0, :, kv, :] = k
        sv_ref[0, 0, :, kv, :] = v
        q = (_stack_heads(q_ref[:, kv * QW:(kv + 1) * QW]) * Q_SCALE).astype(BF16)
        s = lax.dot_general(q, k.astype(BF16), _NT, preferred_element_type=F32)
        o = _softmax_av(s, _sink_column(sink_ref, kv, CTX_LEN), v.astype(BF16))
        o_ref[:, kv * QW:(kv + 1) * QW] = _unstack_heads(o, CTX_LEN).astype(BF16)


def _ctx_attn(u, sink):
    qcols, kcols = N_HEADS * HD, N_KV * HD
    state_spec = pl.BlockSpec((1, 1, CTX_LEN, N_KV, HD), lambda b, s: (b, 0, 0, 0, 0))
    state_shape = jax.ShapeDtypeStruct((N_CTX_SEQ, 1, CTX_LEN, N_KV, HD), F32)
    return pl.pallas_call(
        _ctx_attn_kernel,
        grid_spec=pltpu.PrefetchScalarGridSpec(
            num_scalar_prefetch=1,
            grid=(N_CTX_SEQ,),
            in_specs=[pl.BlockSpec((CTX_LEN, qcols), lambda b, s: (b, COL_Q // qcols)),
                      pl.BlockSpec((CTX_LEN, kcols), lambda b, s: (b, COL_K // kcols)),
                      pl.BlockSpec((CTX_LEN, kcols), lambda b, s: (b, COL_V // kcols))],
            out_specs=[pl.BlockSpec((CTX_LEN, qcols), lambda b, s: (b, 0)),
                       state_spec, state_spec]),
        out_shape=[jax.ShapeDtypeStruct((N_CTX, N_HEADS * HD), BF16), state_shape, state_shape],
        compiler_params=_params(("arbitrary",)),
        name="ctx_attn",
    )(sink, u, u, u)


def _rope(x, cos, sin):
    lane = lax.broadcasted_iota(jnp.int32, x.shape, 1)
    fwd = pltpu.roll(x, HD - 32, axis=1)
    bwd = pltpu.roll(x, 32, axis=1)
    partner = jnp.where((lane & 32) == 0, fwd, bwd)
    return x * cos + partner * sin


def _lat_attn_kernel(sink_ref, q_ref, k_ref, v_ref, ck_ref, cv_ref, cos_ref, sin_ref, o_ref):
    kv = pl.program_id(1)
    nblk = LAT_LEN // ABLK
    cos, sin = cos_ref[...], sin_ref[...]
    k_rot = _rope(k_ref[...], cos, sin).astype(BF16)
    v = v_ref[...].astype(BF16)
    ck = ck_ref[...].astype(BF16)
    cv = cv_ref[...].astype(BF16)
    sk = _sink_column(sink_ref, kv, ABLK)
    r = lax.broadcasted_iota(jnp.int32, (Q_GROUP * ABLK, ABLK), 0) & (ABLK - 1)
    c = lax.broadcasted_iota(jnp.int32, (Q_GROUP * ABLK, ABLK), 1)
    open_tile = jnp.zeros((Q_GROUP * ABLK, ABLK), F32)
    below = jnp.where(c >= r, 0.0, NEG)
    above = jnp.where(c <= r, 0.0, NEG)
    ctx_open = jnp.zeros((Q_GROUP * ABLK, PAST), F32)
    bias_first = jnp.concatenate([open_tile, above, ctx_open], axis=1)
    bias_mid = jnp.concatenate([below, open_tile, above, ctx_open], axis=1)
    bias_last = jnp.concatenate([below, open_tile, ctx_open], axis=1)
    for j in range(nblk):
        rows = slice(j * ABLK, (j + 1) * ABLK)
        cos_j, sin_j = cos[rows], sin[rows]
        qs = jnp.concatenate(
            [_rope(q_ref[rows, g * HD:(g + 1) * HD], cos_j, sin_j) * Q_SCALE for g in range(Q_GROUP)],
            axis=0).astype(BF16)
        blocks = range(max(j - 1, 0), min(j + 2, nblk))
        band = slice(blocks[0] * ABLK, (blocks[-1] + 1) * ABLK)
        keys = jnp.concatenate([k_rot[band], ck], axis=0)
        vals = jnp.concatenate([v[band], cv], axis=0)
        bias = bias_first if j == 0 else bias_last if j == nblk - 1 else bias_mid
        s = lax.dot_general(qs, keys, _NT, preferred_element_type=F32) + bias
        o = _softmax_av(s, sk, vals)
        o_ref[rows, :] = _unstack_heads(o, ABLK).astype(BF16)


def _lat_attn(u, sink, ck, cv, cos, sin):
    base = N_CTX // LAT_LEN
    cache_spec = pl.BlockSpec((None, PAST, HD), lambda b, kv, s: (b, 0, kv))
    table_spec = pl.BlockSpec((LAT_LEN, HD), lambda b, kv, s: (0, 0))
    return pl.pallas_call(
        _lat_attn_kernel,
        grid_spec=pltpu.PrefetchScalarGridSpec(
            num_scalar_prefetch=1,
            grid=(N_LAT_SEQ, N_KV),
            in_specs=[pl.BlockSpec((LAT_LEN, QW), lambda b, kv, s: (base + b, COL_Q // QW + kv)),
                      pl.BlockSpec((LAT_LEN, HD), lambda b, kv, s: (base + b, COL_K // HD + kv)),
                      pl.BlockSpec((LAT_LEN, HD), lambda b, kv, s: (base + b, COL_V // HD + kv)),
                      cache_spec, cache_spec, table_spec, table_spec],
            out_specs=pl.BlockSpec((LAT_LEN, QW), lambda b, kv, s: (b, kv))),
        out_shape=jax.ShapeDtypeStruct((N_LAT, N_HEADS * HD), BF16),
        compiler_params=_params(("arbitrary", "arbitrary")),
        name="lat_attn",
    )(sink, u, u, u, ck, cv, cos, sin)


def _rope_tables():
    n = jnp.arange(LAT_LEN)
    row = (n // GRID_W).astype(F32)
    col = (n % GRID_W).astype(F32)
    quarter = HD // 4
    inv = ROPE_BASE ** (-jnp.arange(quarter, dtype=F32) / quarter)
    ang_r = row[:, None] * inv[None, :]
    ang_c = col[:, None] * inv[None, :]
    cos = jnp.concatenate([jnp.cos(ang_r)] * 2 + [jnp.cos(ang_c)] * 2, axis=-1)
    sin = jnp.concatenate([-jnp.sin(ang_r), jnp.sin(ang_r), -jnp.sin(ang_c), jnp.sin(ang_c)], axis=-1)
    return cos, sin


def _mix_kernel(z_ref, ac_ref, al_ref, ga_ref, gb_ref, wc_ref, wa_ref, o_ref, wcb_ref, wab_ref):
    @pl.when(pl.program_id(1) == 0)
    def _():
        wcb_ref[...] = wc_ref[...].astype(BF16)
        wab_ref[...] = wa_ref[...].astype(BF16)

    is_ctx = pl.program_id(1) < N_CTX // z_ref.shape[0]
    att = jnp.where(is_ctx, ac_ref[...], al_ref[...])
    conv_out = jnp.dot(z_ref[...], wcb_ref[...], preferred_element_type=F32)
    attn_out = jnp.dot(att, wab_ref[...], preferred_element_type=F32)
    mixed = jax.nn.sigmoid(ga_ref[...]) * conv_out + jax.nn.sigmoid(gb_ref[...]) * attn_out
    o_ref[...] = mixed.astype(BF16)


def _mix(zc, att_ctx, att_lat, u, w_conv_out, w_attn_out):
    tm, tn = 512, 1024
    nc = N_CTX // tm
    return pl.pallas_call(
        _mix_kernel,
        grid=(D // tn, N_TOK // tm),
        in_specs=[pl.BlockSpec((tm, CONV_W), lambda j, i: (i, 0)),
                  pl.BlockSpec((tm, N_HEADS * HD), lambda j, i: (jnp.minimum(i, nc - 1), 0)),
                  pl.BlockSpec((tm, N_HEADS * HD), lambda j, i: (jnp.maximum(i - nc, 0), 0)),
                  pl.BlockSpec((tm, tn), lambda j, i: (i, COL_GA // tn + j)),
                  pl.BlockSpec((tm, tn), lambda j, i: (i, COL_GB // tn + j)),
                  pl.BlockSpec((CONV_W, tn), lambda j, i: (0, j)),
                  pl.BlockSpec((N_HEADS * HD, tn), lambda j, i: (0, j))],
        out_specs=pl.BlockSpec((tm, tn), lambda j, i: (i, j)),
        out_shape=jax.ShapeDtypeStruct((N_TOK, D), BF16),
        scratch_shapes=[pltpu.VMEM((CONV_W, tn), BF16), pltpu.VMEM((N_HEADS * HD, tn), BF16)],
        compiler_params=_params(("arbitrary", "arbitrary")),
        name="mix",
    )(zc, att_ctx, att_lat, u, u, w_conv_out, w_attn_out)


def _split_bf16(x):
    hi = x.astype(BF16)
    lo = (x - hi.astype(F32)).astype(BF16)
    return hi, lo


def _outproj_kernel(mx_ref, xc_ref, xl_ref, mod_ref, g_ref, w_ref, wr_ref, br_ref,
                    x1_ref, h2_ref, lg_ref):
    is_ctx = pl.program_id(0) < N_CTX // xc_ref.shape[0]
    x = jnp.where(is_ctx, xc_ref[...], xl_ref[...])
    m = mod_ref[0]
    x1 = x + m[2:3, :] * jnp.dot(mx_ref[...], w_ref[...], preferred_element_type=F32)
    x1_ref[...] = x1
    h2 = _rms(x1, g_ref[...]) * (1.0 + m[4:5, :]) + m[3:4, :]
    h2_ref[...] = h2
    h_hi, h_lo = _split_bf16(h2)
    w_hi, w_lo = _split_bf16(wr_ref[...])
    lg = (jnp.dot(h_hi, w_hi, preferred_element_type=F32)
          + jnp.dot(h_hi, w_lo, preferred_element_type=F32)
          + jnp.dot(h_lo, w_hi, preferred_element_type=F32))
    lg_ref[...] = lg + br_ref[...]


def _outproj(mixed, x_ctx, x_lat, mod, n2g, w_out_bf, w_router, b_router):
    tm = 512
    const = lambda i: (0, 0)
    xc_spec, xl_spec = _set_specs(tm, 2)
    return pl.pallas_call(
        _outproj_kernel,
        grid=(N_TOK // tm,),
        in_specs=[pl.BlockSpec((tm, D), lambda i: (i, 0)),
                  xc_spec, xl_spec,
                  pl.BlockSpec((1, N_MOD, D), lambda i: (_mod_row(i, tm), 0, 0)),
                  pl.BlockSpec((1, D), const),
                  pl.BlockSpec((D, D), const, pipeline_mode=pl.Buffered(1)),
                  pl.BlockSpec((D, N_EXP), const),
                  pl.BlockSpec((1, N_EXP), const)],
        out_specs=[pl.BlockSpec((tm, D), lambda i: (i, 0)),
                   pl.BlockSpec((tm, D), lambda i: (i, 0)),
                   pl.BlockSpec((tm, N_EXP), lambda i: (i, 0))],
        out_shape=[jax.ShapeDtypeStruct((N_TOK, D), F32),
                   jax.ShapeDtypeStruct((N_TOK, D), F32),
                   jax.ShapeDtypeStruct((N_TOK, N_EXP), F32)],
        compiler_params=_params(("arbitrary",)),
        name="outproj",
    )(mixed, x_ctx, x_lat, mod, n2g, w_out_bf, w_router, b_router)


SEL_CHUNK = 128


def _prefix_sum(flags, out_ref):
    n = flags.shape[1]
    ri = lax.broadcasted_iota(jnp.int32, (SEL_CHUNK, SEL_CHUNK), 0)
    ci = lax.broadcasted_iota(jnp.int32, (SEL_CHUNK, SEL_CHUNK), 1)
    tri = jnp.where(ri <= ci, 1.0, 0.0).astype(BF16)
    run = jnp.zeros((N_EXP, 1), F32)
    for ch in range(n // SEL_CHUNK):
        blk = flags[:, ch * SEL_CHUNK:(ch + 1) * SEL_CHUNK].astype(BF16)
        within = jnp.dot(blk, tri, preferred_element_type=F32) + run
        out_ref[:, ch * SEL_CHUNK:(ch + 1) * SEL_CHUNK] = within
        run = within[:, SEL_CHUNK - 1:SEL_CHUNK]


def _select_kernel(lg_ref, idx_ref, gate_ref, cs_ref):
    n = lg_ref.shape[1]
    lg = lg_ref[...]
    ex = jnp.exp(lg - jnp.max(lg, axis=0, keepdims=True))
    aff = ex / jnp.sum(ex, axis=0, keepdims=True)

    def bit_step(b, cur):
        cand = cur | (1 << (30 - b))
        cnt = jnp.sum(jnp.where(aff >= pltpu.bitcast(cand, F32), 1.0, 0.0), axis=1, keepdims=True)
        return jnp.where(cnt >= CAP, cand, cur)

    thr = pltpu.bitcast(lax.fori_loop(0, 31, bit_step, jnp.zeros((N_EXP, 1), jnp.int32)), F32)
    above = aff > thr
    tied = aff == thr
    room = CAP - jnp.sum(jnp.where(above, 1.0, 0.0), axis=1, keepdims=True)
    _prefix_sum(jnp.where(tied, 1.0, 0.0), cs_ref)
    sel = jnp.logical_or(above, jnp.logical_and(tied, cs_ref[...] <= room))
    gsel = jnp.where(sel, aff, 0.0)
    _prefix_sum(jnp.where(sel, 1.0, 0.0), cs_ref)

    slot = lax.broadcasted_iota(jnp.int32, (CAP, n), 0).astype(F32)
    g_hi = gsel.astype(BF16).astype(F32)
    rest = gsel - g_hi
    g_mid = rest.astype(BF16).astype(F32)
    g_lo = rest - g_mid
    ones = jnp.ones((1, n), F32)
    pad = jnp.zeros((4, n), F32)
    last_slot = lax.broadcasted_iota(jnp.int32, (1, CAP), 1) == CAP - 1
    total = jnp.sum(gsel, axis=1, keepdims=True)
    for e in range(N_EXP):
        row = slice(e, e + 1)
        le = jnp.where(cs_ref[row, :] <= slot, 1.0, 0.0).astype(BF16)
        lhs = jnp.concatenate([ones, g_hi[row], g_mid[row], g_lo[row], pad], axis=0).astype(BF16)
        res = lax.dot_general(lhs, le, (((1,), (1,)), ((), ())), preferred_element_type=F32)
        idx_ref[0, row, :] = res[0:1, :].astype(jnp.int32)
        cum = res[1:2, :] + res[2:3, :] + res[3:4, :]
        gate_ref[0, row, :] = jnp.where(last_slot, total[row], pltpu.roll(cum, CAP - 1, axis=1)) - cum


def _select(logits_t):
    return pl.pallas_call(
        _select_kernel,
        grid=(N_SETS,),
        in_specs=[pl.BlockSpec((N_EXP, N_CTX), lambda s: (0, s))],
        out_specs=[pl.BlockSpec((1, N_EXP, CAP), lambda s: (s, 0, 0)),
                   pl.BlockSpec((1, N_EXP, CAP), lambda s: (s, 0, 0))],
        out_shape=[jax.ShapeDtypeStruct((N_SETS, N_EXP, CAP), jnp.int32),
                   jax.ShapeDtypeStruct((N_SETS, N_EXP, CAP), F32)],
        scratch_shapes=[pltpu.VMEM((N_EXP, N_CTX), F32)],
        compiler_params=_params(("arbitrary",)),
        name="select",
    )(logits_t)


MOE_ROWS = N_SETS * CAP
MOE_TF = 256
MOE_NF = D_EXP // MOE_TF
MOE_TN = 512
N_FILL = N_TOK // MOE_ROWS
assert MOE_NF == 4


def _moe_kernel(rows_ref, h_hbm, gate_ref, wg_ref, wu_ref, wd_ref, y_hbm,
                xf_ref, xb_ref, acc_ref, rmw_ref, sem_x, sem_put, sem_get, sem_fill):
    e = pl.program_id(0)
    f = pl.program_id(1)
    e_next = jnp.minimum(e + 1, N_EXP - 1)

    def tok_row(hbm_ref, ex, k):
        return hbm_ref.at[pl.ds(rows_ref[ex * MOE_ROWS + k], 1), :]

    def x_copy(ex, k):
        return pltpu.make_async_copy(tok_row(h_hbm, ex, k), xf_ref.at[pl.ds(k, 1), :], sem_x)

    def put_copy(ex, k):
        return pltpu.make_async_copy(rmw_ref.at[pl.ds(k, 1), :], tok_row(y_hbm, ex, k), sem_put)

    def get_copy(ex, k):
        return pltpu.make_async_copy(tok_row(y_hbm, ex, k), rmw_ref.at[pl.ds(k, 1), :], sem_get)

    def fill_copy(c):
        return pltpu.make_async_copy(rmw_ref, y_hbm.at[pl.ds(c * MOE_ROWS, MOE_ROWS), :], sem_fill)

    def start_rows(copy, ex):
        for k in range(MOE_ROWS):
            copy(ex, k).start(priority=k % 2)

    def wait_rows(sem, src, dst):
        pltpu.make_async_copy(src, dst, sem).wait()

    y_rows = y_hbm.at[pl.ds(0, MOE_ROWS), :]
    h_rows = h_hbm.at[pl.ds(0, MOE_ROWS), :]

    def ffn_step(first=False):
        x = xb_ref[...]
        g = jnp.dot(x, wg_ref[...].astype(BF16), preferred_element_type=F32)
        up = jnp.dot(x, wu_ref[...].astype(BF16), preferred_element_type=F32)
        hid = (_silu(g) * up).astype(BF16)
        wd = wd_ref[...].astype(BF16)
        for c in range(D // MOE_TN):
            cols = slice(c * MOE_TN, (c + 1) * MOE_TN)
            part = jnp.dot(hid, wd[:, cols], preferred_element_type=F32)
            if first:
                acc_ref[:, cols] = part
            else:
                acc_ref[:, cols] += part

    @pl.when(jnp.logical_and(e == 0, f == 0))
    def _():
        rmw_ref[...] = jnp.zeros(rmw_ref.shape, F32)
        for c in range(N_FILL):
            fill_copy(c).start()

        def issue(k, c):
            x_copy(0, k).start()
            return c
        lax.fori_loop(0, MOE_ROWS, issue, 0)

    @pl.when(f == 0)
    def _():
        wait_rows(sem_x, h_rows, xf_ref)
        xb_ref[...] = xf_ref[...].astype(BF16)

    @pl.when(jnp.logical_and(e > 0, f == 0))
    def _():
        start_rows(put_copy, e - 1)
        ffn_step(first=True)

    @pl.when(jnp.logical_and(e == 0, f == 0))
    def _():
        ffn_step(first=True)

    @pl.when(f == 1)
    def _():
        start_rows(x_copy, e_next)
        ffn_step()

    @pl.when(jnp.logical_and(e == 0, f == 2))
    def _():
        for c in range(N_FILL):
            fill_copy(c).wait()

    @pl.when(jnp.logical_and(e > 0, f == 2))
    def _():
        wait_rows(sem_put, rmw_ref, y_rows)

    @pl.when(f == 2)
    def _():
        start_rows(get_copy, e)
        ffn_step()

    @pl.when(f == 3)
    def _():
        ffn_step()
        wait_rows(sem_get, y_rows, rmw_ref)
        rmw_ref[...] = rmw_ref[...] + gate_ref[...] * acc_ref[...]

    @pl.when(jnp.logical_and(e == N_EXP - 1, f == MOE_NF - 1))
    def _():
        def issue(k, c):
            put_copy(N_EXP - 1, k).start()
            return c
        lax.fori_loop(0, MOE_ROWS, issue, 0)
        wait_rows(sem_put, rmw_ref, y_rows)
        wait_rows(sem_x, h_rows, xf_ref)


def _moe(rows, h2, gates, w_g, w_u, w_d):
    return pl.pallas_call(
        _moe_kernel,
        grid_spec=pltpu.PrefetchScalarGridSpec(
            num_scalar_prefetch=1,
            grid=(N_EXP, MOE_NF),
            in_specs=[pl.BlockSpec(memory_space=pl.ANY),
                      pl.BlockSpec((None, MOE_ROWS, 1), lambda e, f, s: (e, 0, 0)),
                      pl.BlockSpec((None, D, MOE_TF), lambda e, f, s: (e, 0, f)),
                      pl.BlockSpec((None, D, MOE_TF), lambda e, f, s: (e, 0, f)),
                      pl.BlockSpec((None, MOE_TF, D), lambda e, f, s: (e, f, 0))],
            out_specs=pl.BlockSpec(memory_space=pl.ANY),
            scratch_shapes=[pltpu.VMEM((MOE_ROWS, D), F32),
                            pltpu.VMEM((MOE_ROWS, D), BF16),
                            pltpu.VMEM((MOE_ROWS, D), F32),
                            pltpu.VMEM((MOE_ROWS, D), F32),
                            pltpu.SemaphoreType.DMA(()),
                            pltpu.SemaphoreType.DMA(()),
                            pltpu.SemaphoreType.DMA(()),
                            pltpu.SemaphoreType.DMA(())]),
        out_shape=jax.ShapeDtypeStruct((N_TOK, D), F32),
        compiler_params=_params(("arbitrary", "arbitrary")),
        name="moe",
    )(rows, h2, gates, w_g, w_u, w_d)


def _final_kernel(x1_ref, y_ref, mod_ref, g_ref, o_ref):
    o_ref[...] = _rms(x1_ref[...] + mod_ref[0][5:6, :] * y_ref[...], g_ref[...])


def _final(x1, y, tok0, n_tok, mod, fng):
    tm = 512
    blk0 = tok0 // tm
    return pl.pallas_call(
        _final_kernel,
        grid=(n_tok // tm,),
        in_specs=[pl.BlockSpec((tm, D), lambda i: (blk0 + i, 0)),
                  pl.BlockSpec((tm, D), lambda i: (blk0 + i, 0)),
                  pl.BlockSpec((1, N_MOD, D), lambda i: (_mod_row(blk0 + i, tm), 0, 0)),
                  pl.BlockSpec((1, D), lambda i: (0, 0))],
        out_specs=pl.BlockSpec((tm, D), lambda i: (i, 0)),
        out_shape=jax.ShapeDtypeStruct((n_tok, D), F32),
        compiler_params=_params(("arbitrary",)),
        name="final",
    )(x1, y, mod, fng)


def kernel(x_prompt, x_sample, c, cache_k, cache_v, c_ctx, w_ada, b_ada, norm1_g, w_in, conv_dw, conv_dw_b, conv_ln_g, conv_ln_b, w_conv_out, attn_sink, w_attn_out, w_out, norm2_g, w_router, b_router, w_exp_gate, w_exp_up, w_exp_down, final_norm_g):
    x_ctx = x_prompt.reshape(N_CTX, D)
    x_lat = x_sample.reshape(N_LAT, D)
    cond = jnp.concatenate([c, c_ctx[None, :], jnp.zeros((MOD_ROWS - N_LAT_SEQ - 1, D), F32)], axis=0)
    mod = _ada(cond, w_ada[0], b_ada[0]).reshape(MOD_ROWS, N_MOD, D)

    u = _inproj(_prenorm(x_ctx, x_lat, mod, norm1_g), w_in[0])
    zc = _conv(u, conv_dw[0], conv_dw_b, conv_ln_g, conv_ln_b)

    sink = attn_sink[0]
    cos, sin = _rope_tables()
    ck = cache_k[:, 0].reshape(N_LAT_SEQ, PAST, N_KV * HD)
    cv = cache_v[:, 0].reshape(N_LAT_SEQ, PAST, N_KV * HD)
    att_ctx, state_k, state_v = _ctx_attn(u, sink)
    mixed = _mix(zc, att_ctx, _lat_attn(u, sink, ck, cv, cos, sin), u,
                 w_conv_out[0], w_attn_out[0])

    x1, h2, logits = _outproj(mixed, x_ctx, x_lat, mod, norm2_g, w_out[0].astype(BF16),
                              w_router[0], b_router)

    idx, gates = _select(logits.T)
    rows = (idx + (jnp.arange(N_SETS, dtype=jnp.int32) * N_CTX)[:, None, None])
    rows = rows.transpose(1, 0, 2).reshape(N_EXP * MOE_ROWS)
    gates = gates.transpose(1, 0, 2).reshape(N_EXP, MOE_ROWS, 1)
    ffn = _moe(rows, h2, gates, w_exp_gate[0], w_exp_up[0], w_exp_down[0])

    fng = final_norm_g.reshape(1, D)
    y_prompt = _final(x1, ffn, 0, N_CTX, mod, fng).reshape(N_CTX_SEQ, CTX_LEN, D)
    y_sample = _final(x1, ffn, N_CTX, N_LAT, mod, fng).reshape(N_LAT_SEQ, LAT_LEN, D)
    return (y_prompt, y_sample, state_k, state_v)
```

```python
import functools

import jax
import jax.numpy as jnp
from jax import lax
from jax.experimental import pallas as pl
from jax.experimental.pallas import tpu as pltpu

D = 2048
N_CTX_SEQ, CTX_LEN = 16, 256
N_LAT_SEQ, LAT_LEN = 4, 1024
N_CTX = N_CTX_SEQ * CTX_LEN
N_LAT = N_LAT_SEQ * LAT_LEN
N_TOK = N_CTX + N_LAT
PAST = 512
GRID_W = 64
N_HEADS, N_KV, HD = 16, 4, 128
Q_GROUP = N_HEADS // N_KV
WINDOW = 128
ABLK = 128
CONV_W = 1024
CONV_K = 31
CONV_PAD = CONV_K // 2
N_EXP = 16
D_EXP = 1024
CAP = 2 * N_CTX // N_EXP
N_SETS = 2
ROPE_BASE = 10000.0
EPS = 1e-6
N_MOD = 6
NEG = -1e30
IN_COLS = 2 * CONV_W + N_HEADS * HD + 2 * N_KV * HD + 2 * D
COL_Q = 2 * CONV_W
COL_K = COL_Q + N_HEADS * HD
COL_V = COL_K + N_KV * HD
COL_GA = COL_V + N_KV * HD
COL_GB = COL_GA + D
MOD_ROWS = 8
CTX_MOD_ROW = N_LAT_SEQ

VMEM_LIMIT = 58 * 1024 * 1024
F32 = jnp.float32
BF16 = jnp.bfloat16


def _params(sem, vmem=VMEM_LIMIT):
    return pltpu.CompilerParams(dimension_semantics=sem, vmem_limit_bytes=vmem)


def _mod_row(tok_block, block_tokens):
    ctx_blocks = N_CTX // block_tokens
    per_seq = LAT_LEN // block_tokens
    return jnp.where(tok_block < ctx_blocks, CTX_MOD_ROW, (tok_block - ctx_blocks) // per_seq)


def _silu(x):
    return x * jax.nn.sigmoid(x)


def _rms(x, g):
    return x * lax.rsqrt(jnp.mean(x * x, axis=-1, keepdims=True) + EPS) * g


def _ada_kernel(cond_ref, w_ref, b_ref, o_ref):
    a = _silu(cond_ref[...]).astype(BF16)
    o_ref[...] = jnp.dot(a, w_ref[...].astype(BF16), preferred_element_type=F32) + b_ref[...]


def _ada(cond, w_ada, b_ada):
    tn = 1024
    n = N_MOD * D
    return pl.pallas_call(
        _ada_kernel,
        grid=(n // tn,),
        in_specs=[pl.BlockSpec((MOD_ROWS, D), lambda j: (0, 0)),
                  pl.BlockSpec((D, tn), lambda j: (0, j)),
                  pl.BlockSpec((1, tn), lambda j: (0, j))],
        out_specs=pl.BlockSpec((MOD_ROWS, tn), lambda j: (0, j)),
        out_shape=jax.ShapeDtypeStruct((MOD_ROWS, n), F32),
        compiler_params=_params(("arbitrary",)),
        name="ada",
    )(cond, w_ada, b_ada.reshape(1, n))


def _set_specs(tm, buffers):
    nc = N_CTX // tm
    mode = pl.Buffered(buffers)
    ctx = pl.BlockSpec((tm, D), lambda i, *_: (jnp.minimum(i, nc - 1), 0), pipeline_mode=mode)
    lat = pl.BlockSpec((tm, D), lambda i, *_: (jnp.maximum(i - nc, 0), 0), pipeline_mode=mode)
    return ctx, lat


def _prenorm_kernel(xc_ref, xl_ref, mod_ref, g_ref, h_ref):
    is_ctx = pl.program_id(0) < N_CTX // xc_ref.shape[0]
    x = jnp.where(is_ctx, xc_ref[...], xl_ref[...])
    m = mod_ref[0]
    h = _rms(x, g_ref[...]) * (1.0 + m[1:2, :]) + m[0:1, :]
    h_ref[...] = h.astype(BF16)


def _prenorm(x_ctx, x_lat, mod, n1g):
    tm = 512
    xc_spec, xl_spec = _set_specs(tm, 2)
    return pl.pallas_call(
        _prenorm_kernel,
        grid=(N_TOK // tm,),
        in_specs=[xc_spec, xl_spec,
                  pl.BlockSpec((1, N_MOD, D), lambda i: (_mod_row(i, tm), 0, 0)),
                  pl.BlockSpec((1, D), lambda i: (0, 0))],
        out_specs=pl.BlockSpec((tm, D), lambda i: (i, 0)),
        out_shape=jax.ShapeDtypeStruct((N_TOK, D), BF16),
        compiler_params=_params(("arbitrary",)),
        name="prenorm",
    )(x_ctx, x_lat, mod, n1g)


def _inproj_kernel(h_ref, w_ref, o_ref, wb_ref):
    @pl.when(pl.program_id(1) == 0)
    def _():
        wb_ref[...] = w_ref[...].astype(BF16)

    o_ref[...] = jnp.dot(h_ref[...], wb_ref[...], preferred_element_type=F32)


def _inproj(h, w_in):
    tm, tn = 1024, 1024
    return pl.pallas_call(
        _inproj_kernel,
        grid=(IN_COLS // tn, N_TOK // tm),
        in_specs=[pl.BlockSpec((tm, D), lambda j, i: (i, 0)),
                  pl.BlockSpec((D, tn), lambda j, i: (0, j))],
        out_specs=pl.BlockSpec((tm, tn), lambda j, i: (i, j)),
        out_shape=jax.ShapeDtypeStruct((N_TOK, IN_COLS), F32),
        scratch_shapes=[pltpu.VMEM((D, tn), BF16)],
        compiler_params=_params(("arbitrary", "arbitrary")),
        name="inproj",
    )(h, w_in)


CONV_T = 256
CONV_HALO = 16
CONV_ROWS = 32
SUBLANES = 8
LANES = 128
CONV_ZS = CONV_T + (CONV_K // SUBLANES) * SUBLANES


def _conv_kernel(main_ref, prev_ref, next_ref, w_ref, b_ref, lg_ref, lb_ref, o_ref,
                 z_ref, zs_ref, y_ref):
    i = pl.program_id(0)
    ctx_blocks = N_CTX // CONV_T
    per_seq = LAT_LEN // CONV_T
    pos = (i - ctx_blocks) % per_seq
    is_lat = i >= ctx_blocks
    has_prev = jnp.logical_and(is_lat, pos > 0)
    has_next = jnp.logical_and(is_lat, pos < per_seq - 1)

    def glu(ref):
        v = ref[...]
        return v[:, :CONV_W] * jax.nn.sigmoid(v[:, CONV_W:])

    z_ref[0:CONV_HALO, :] = jnp.where(has_prev, glu(prev_ref), 0.0)
    z_ref[CONV_HALO:CONV_HALO + CONV_T, :] = glu(main_ref)
    z_ref[CONV_HALO + CONV_T:, :] = jnp.where(has_next, glu(next_ref), 0.0)

    z = z_ref[...]
    zs_ref[0] = z[:CONV_ZS]
    for s in range(1, SUBLANES):
        zs_ref[s] = pltpu.roll(z, z.shape[0] - s, axis=0)[:CONV_ZS]

    def channel_group(c, carry):
        cols = pl.ds(pl.multiple_of(c * LANES, LANES), LANES)
        acc = jnp.zeros((CONV_T, LANES), F32)
        for j in range(CONV_K):
            a, s = divmod(CONV_HALO - CONV_PAD + j, SUBLANES)
            acc = acc + zs_ref[s, a * SUBLANES:a * SUBLANES + CONV_T, cols] * w_ref[j:j + 1, cols]
        y_ref[:, cols] = acc + b_ref[:, cols]
        return carry

    lax.fori_loop(0, CONV_W // LANES, channel_group, 0)

    for r in range(CONV_T // CONV_ROWS):
        acc = y_ref[r * CONV_ROWS:(r + 1) * CONV_ROWS, :]
        mu = jnp.mean(acc, axis=-1, keepdims=True)
        cen = acc - mu
        var = jnp.mean(cen * cen, axis=-1, keepdims=True)
        y = cen * lax.rsqrt(var + EPS) * lg_ref[...] + lb_ref[...]
        o_ref[r * CONV_ROWS:(r + 1) * CONV_ROWS, :] = _silu(y).astype(BF16)


def _conv(u, w_dw, b_dw, ln_g, ln_b):
    nblk = N_TOK // CONV_T
    hpb = CONV_T // CONV_HALO
    last = N_TOK // CONV_HALO - 1
    row = lambda i: (0, 0)
    return pl.pallas_call(
        _conv_kernel,
        grid=(nblk,),
        in_specs=[pl.BlockSpec((CONV_T, 2 * CONV_W), lambda i: (i, 0)),
                  pl.BlockSpec((CONV_HALO, 2 * CONV_W), lambda i: (jnp.maximum(i * hpb - 1, 0), 0)),
                  pl.BlockSpec((CONV_HALO, 2 * CONV_W), lambda i: (jnp.minimum((i + 1) * hpb, last), 0)),
                  pl.BlockSpec((CONV_K, CONV_W), row),
                  pl.BlockSpec((1, CONV_W), row),
                  pl.BlockSpec((1, CONV_W), row),
                  pl.BlockSpec((1, CONV_W), row)],
        out_specs=pl.BlockSpec((CONV_T, CONV_W), lambda i: (i, 0)),
        out_shape=jax.ShapeDtypeStruct((N_TOK, CONV_W), BF16),
        scratch_shapes=[pltpu.VMEM((CONV_T + 2 * CONV_HALO, CONV_W), F32),
                        pltpu.VMEM((SUBLANES, CONV_ZS, CONV_W), F32),
                        pltpu.VMEM((CONV_T, CONV_W), F32)],
        compiler_params=_params(("arbitrary",)),
        name="conv",
    )(u, u, u, w_dw, b_dw, ln_g, ln_b)


def _stack_heads(q):
    return jnp.concatenate([q[:, g * HD:(g + 1) * HD] for g in range(Q_GROUP)], axis=0)


def _unstack_heads(o, rows):
    return jnp.concatenate([o[g * rows:(g + 1) * rows, :] for g in range(Q_GROUP)], axis=1)


LOG2E = 1.4426950408889634
Q_SCALE = HD ** -0.5 * LOG2E
QW = Q_GROUP * HD


def _sink_column(sink_ref, kv, rows):
    rid = lax.broadcasted_iota(jnp.int32, (Q_GROUP * rows, 1), 0)
    col = jnp.full((Q_GROUP * rows, 1), sink_ref[kv * Q_GROUP], F32)
    for g in range(1, Q_GROUP):
        col = jnp.where(rid >= g * rows, sink_ref[kv * Q_GROUP + g], col)
    return col * LOG2E


def _softmax_av(s, sk, vals):
    m = jnp.maximum(jnp.max(s, axis=-1, keepdims=True), sk)
    p = jnp.exp2(s - m)
    den = jnp.sum(p, axis=-1, keepdims=True) + jnp.exp2(sk - m)
    o = jnp.dot(p.astype(BF16), vals, preferred_element_type=F32)
    return o / den


_NT = (((1,), (1,)), ((), ()))


def _ctx_attn_kernel(sink_ref, q_ref, k_ref, v_ref, o_ref, sk_ref, sv_ref):
    for kv in range(N_KV):
        k = k_ref[:, kv * HD:(kv + 1) * HD]
        v = v_ref[:, kv * HD:(kv + 1) * HD]
        sk_ref[0, 0, :, kv, :] = k
        sv_ref[0, 0, :, kv, :] = v
        q = (_stack_heads(q_ref[:, kv * QW:(kv + 1) * QW]) * Q_SCALE).astype(BF16)
        s = lax.dot_general(q, k.astype(BF16), _NT, preferred_element_type=F32)
        o = _softmax_av(s, _sink_column(sink_ref, kv, CTX_LEN), v.astype(BF16))
        o_ref[:, kv * QW:(kv + 1) * QW] = _unstack_heads(o, CTX_LEN).astype(BF16)


def _ctx_attn(u, sink):
    qcols, kcols = N_HEADS * HD, N_KV * HD
    state_spec = pl.BlockSpec((1, 1, CTX_LEN, N_KV, HD), lambda b, s: (b, 0, 0, 0, 0))
    state_shape = jax.ShapeDtypeStruct((N_CTX_SEQ, 1, CTX_LEN, N_KV, HD), F32)
    return pl.pallas_call(
        _ctx_attn_kernel,
        grid_spec=pltpu.PrefetchScalarGridSpec(
            num_scalar_prefetch=1,
            grid=(N_CTX_SEQ,),
            in_specs=[pl.BlockSpec((CTX_LEN, qcols), lambda b, s: (b, COL_Q // qcols)),
                      pl.BlockSpec((CTX_LEN, kcols), lambda b, s: (b, COL_K // kcols)),
                      pl.BlockSpec((CTX_LEN, kcols), lambda b, s: (b, COL_V // kcols))],
            out_specs=[pl.BlockSpec((CTX_LEN, qcols), lambda b, s: (b, 0)),
                       state_spec, state_spec]),
        out_shape=[jax.ShapeDtypeStruct((N_CTX, N_HEADS * HD), BF16), state_shape, state_shape],
        compiler_params=_params(("arbitrary",)),
        name="ctx_attn",
    )(sink, u, u, u)


def _rope(x, cos, sin):
    lane = lax.broadcasted_iota(jnp.int32, x.shape, 1)
    fwd = pltpu.roll(x, HD - 32, axis=1)
    bwd = pltpu.roll(x, 32, axis=1)
    partner = jnp.where((lane & 32) == 0, fwd, bwd)
    return x * cos + partner * sin


def _lat_attn_kernel(sink_ref, q_ref, k_ref, v_ref, ck_ref, cv_ref, cos_ref, sin_ref, o_ref):
    kv = pl.program_id(1)
    nblk = LAT_LEN // ABLK
    cos, sin = cos_ref[...], sin_ref[...]
    k_rot = _rope(k_ref[...], cos, sin).astype(BF16)
    v = v_ref[...].astype(BF16)
    ck = ck_ref[...].astype(BF16)
    cv = cv_ref[...].astype(BF16)
    sk = _sink_column(sink_ref, kv, ABLK)
    r = lax.broadcasted_iota(jnp.int32, (Q_GROUP * ABLK, ABLK), 0) & (ABLK - 1)
    c = lax.broadcasted_iota(jnp.int32, (Q_GROUP * ABLK, ABLK), 1)
    open_tile = jnp.zeros((Q_GROUP * ABLK, ABLK), F32)
    below = jnp.where(c >= r, 0.0, NEG)
    above = jnp.where(c <= r, 0.0, NEG)
    ctx_open = jnp.zeros((Q_GROUP * ABLK, PAST), F32)
    bias_first = jnp.concatenate([open_tile, above, ctx_open], axis=1)
    bias_mid = jnp.concatenate([below, open_tile, above, ctx_open], axis=1)
    bias_last = jnp.concatenate([below, open_tile, ctx_open], axis=1)
    for j in range(nblk):
        rows = slice(j * ABLK, (j + 1) * ABLK)
        cos_j, sin_j = cos[rows], sin[rows]
        qs = jnp.concatenate(
            [_rope(q_ref[rows, g * HD:(g + 1) * HD], cos_j, sin_j) * Q_SCALE for g in range(Q_GROUP)],
            axis=0).astype(BF16)
        blocks = range(max(j - 1, 0), min(j + 2, nblk))
        band = slice(blocks[0] * ABLK, (blocks[-1] + 1) * ABLK)
        keys = jnp.concatenate([k_rot[band], ck], axis=0)
        vals = jnp.concatenate([v[band], cv], axis=0)
        bias = bias_first if j == 0 else bias_last if j == nblk - 1 else bias_mid
        s = lax.dot_general(qs, keys, _NT, preferred_element_type=F32) + bias
        o = _softmax_av(s, sk, vals)
        o_ref[rows, :] = _unstack_heads(o, ABLK).astype(BF16)


def _lat_attn(u, sink, ck, cv, cos, sin):
    base = N_CTX // LAT_LEN
    cache_spec = pl.BlockSpec((None, PAST, HD), lambda b, kv, s: (b, 0, kv))
    table_spec = pl.BlockSpec((LAT_LEN, HD), lambda b, kv, s: (0, 0))
    return pl.pallas_call(
        _lat_attn_kernel,
        grid_spec=pltpu.PrefetchScalarGridSpec(
            num_scalar_prefetch=1,
            grid=(N_LAT_SEQ, N_KV),
            in_specs=[pl.BlockSpec((LAT_LEN, QW), lambda b, kv, s: (base + b, COL_Q // QW + kv)),
                      pl.BlockSpec((LAT_LEN, HD), lambda b, kv, s: (base + b, COL_K // HD + kv)),
                      pl.BlockSpec((LAT_LEN, HD), lambda b, kv, s: (base + b, COL_V // HD + kv)),
                      cache_spec, cache_spec, table_spec, table_spec],
            out_specs=pl.BlockSpec((LAT_LEN, QW), lambda b, kv, s: (b, kv))),
        out_shape=jax.ShapeDtypeStruct((N_LAT, N_HEADS * HD), BF16),
        compiler_params=_params(("arbitrary", "arbitrary")),
        name="lat_attn",
    )(sink, u, u, u, ck, cv, cos, sin)


def _rope_tables():
    n = jnp.arange(LAT_LEN)
    row = (n // GRID_W).astype(F32)
    col = (n % GRID_W).astype(F32)
    quarter = HD // 4
    inv = ROPE_BASE ** (-jnp.arange(quarter, dtype=F32) / quarter)
    ang_r = row[:, None] * inv[None, :]
    ang_c = col[:, None] * inv[None, :]
    cos = jnp.concatenate([jnp.cos(ang_r)] * 2 + [jnp.cos(ang_c)] * 2, axis=-1)
    sin = jnp.concatenate([-jnp.sin(ang_r), jnp.sin(ang_r), -jnp.sin(ang_c), jnp.sin(ang_c)], axis=-1)
    return cos, sin


def _mix_kernel(z_ref, ac_ref, al_ref, ga_ref, gb_ref, wc_ref, wa_ref, o_ref, wcb_ref, wab_ref):
    @pl.when(pl.program_id(1) == 0)
    def _():
        wcb_ref[...] = wc_ref[...].astype(BF16)
        wab_ref[...] = wa_ref[...].astype(BF16)

    is_ctx = pl.program_id(1) < N_CTX // z_ref.shape[0]
    att = jnp.where(is_ctx, ac_ref[...], al_ref[...])
    conv_out = jnp.dot(z_ref[...], wcb_ref[...], preferred_element_type=F32)
    attn_out = jnp.dot(att, wab_ref[...], preferred_element_type=F32)
    mixed = jax.nn.sigmoid(ga_ref[...]) * conv_out + jax.nn.sigmoid(gb_ref[...]) * attn_out
    o_ref[...] = mixed.astype(BF16)


def _mix(zc, att_ctx, att_lat, u, w_conv_out, w_attn_out):
    tm, tn = 512, 1024
    nc = N_CTX // tm
    return pl.pallas_call(
        _mix_kernel,
        grid=(D // tn, N_TOK // tm),
        in_specs=[pl.BlockSpec((tm, CONV_W), lambda j, i: (i, 0)),
                  pl.BlockSpec((tm, N_HEADS * HD), lambda j, i: (jnp.minimum(i, nc - 1), 0)),
                  pl.BlockSpec((tm, N_HEADS * HD), lambda j, i: (jnp.maximum(i - nc, 0), 0)),
                  pl.BlockSpec((tm, tn), lambda j, i: (i, COL_GA // tn + j)),
                  pl.BlockSpec((tm, tn), lambda j, i: (i, COL_GB // tn + j)),
                  pl.BlockSpec((CONV_W, tn), lambda j, i: (0, j)),
                  pl.BlockSpec((N_HEADS * HD, tn), lambda j, i: (0, j))],
        out_specs=pl.BlockSpec((tm, tn), lambda j, i: (i, j)),
        out_shape=jax.ShapeDtypeStruct((N_TOK, D), BF16),
        scratch_shapes=[pltpu.VMEM((CONV_W, tn), BF16), pltpu.VMEM((N_HEADS * HD, tn), BF16)],
        compiler_params=_params(("arbitrary", "arbitrary")),
        name="mix",
    )(zc, att_ctx, att_lat, u, u, w_conv_out, w_attn_out)


def _split_bf16(x):
    hi = x.astype(BF16)
    lo = (x - hi.astype(F32)).astype(BF16)
    return hi, lo


def _outproj_kernel(mx_ref, xc_ref, xl_ref, mod_ref, g_ref, w_ref, wr_ref, br_ref,
                    x1_ref, h2_ref, lg_ref):
    is_ctx = pl.program_id(0) < N_CTX // xc_ref.shape[0]
    x = jnp.where(is_ctx, xc_ref[...], xl_ref[...])
    m = mod_ref[0]
    x1 = x + m[2:3, :] * jnp.dot(mx_ref[...], w_ref[...], preferred_element_type=F32)
    x1_ref[...] = x1
    h2 = _rms(x1, g_ref[...]) * (1.0 + m[4:5, :]) + m[3:4, :]
    h2_ref[...] = h2
    h_hi, h_lo = _split_bf16(h2)
    w_hi, w_lo = _split_bf16(wr_ref[...])
    lg = (jnp.dot(h_hi, w_hi, preferred_element_type=F32)
          + jnp.dot(h_hi, w_lo, preferred_element_type=F32)
          + jnp.dot(h_lo, w_hi, preferred_element_type=F32))
    lg_ref[...] = lg + br_ref[...]


def _outproj(mixed, x_ctx, x_lat, mod, n2g, w_out_bf, w_router, b_router):
    tm = 512
    const = lambda i: (0, 0)
    xc_spec, xl_spec = _set_specs(tm, 2)
    return pl.pallas_call(
        _outproj_kernel,
        grid=(N_TOK // tm,),
        in_specs=[pl.BlockSpec((tm, D), lambda i: (i, 0)),
                  xc_spec, xl_spec,
                  pl.BlockSpec((1, N_MOD, D), lambda i: (_mod_row(i, tm), 0, 0)),
                  pl.BlockSpec((1, D), const),
                  pl.BlockSpec((D, D), const, pipeline_mode=pl.Buffered(1)),
                  pl.BlockSpec((D, N_EXP), const),
                  pl.BlockSpec((1, N_EXP), const)],
        out_specs=[pl.BlockSpec((tm, D), lambda i: (i, 0)),
                   pl.BlockSpec((tm, D), lambda i: (i, 0)),
                   pl.BlockSpec((tm, N_EXP), lambda i: (i, 0))],
        out_shape=[jax.ShapeDtypeStruct((N_TOK, D), F32),
                   jax.ShapeDtypeStruct((N_TOK, D), F32),
                   jax.ShapeDtypeStruct((N_TOK, N_EXP), F32)],
        compiler_params=_params(("arbitrary",)),
        name="outproj",
    )(mixed, x_ctx, x_lat, mod, n2g, w_out_bf, w_router, b_router)


SEL_CHUNK = 128


def _prefix_sum(flags, out_ref):
    n = flags.shape[1]
    ri = lax.broadcasted_iota(jnp.int32, (SEL_CHUNK, SEL_CHUNK), 0)
    ci = lax.broadcasted_iota(jnp.int32, (SEL_CHUNK, SEL_CHUNK), 1)
    tri = jnp.where(ri <= ci, 1.0, 0.0).astype(BF16)
    run = jnp.zeros((N_EXP, 1), F32)
    for ch in range(n // SEL_CHUNK):
        blk = flags[:, ch * SEL_CHUNK:(ch + 1) * SEL_CHUNK].astype(BF16)
        within = jnp.dot(blk, tri, preferred_element_type=F32) + run
        out_ref[:, ch * SEL_CHUNK:(ch + 1) * SEL_CHUNK] = within
        run = within[:, SEL_CHUNK - 1:SEL_CHUNK]


def _select_kernel(lg_ref, idx_ref, gate_ref, cs_ref):
    n = lg_ref.shape[1]
    lg = lg_ref[...]
    ex = jnp.exp(lg - jnp.max(lg, axis=0, keepdims=True))
    aff = ex / jnp.sum(ex, axis=0, keepdims=True)

    def bit_step(b, cur):
        cand = cur | (1 << (30 - b))
        cnt = jnp.sum(jnp.where(aff >= pltpu.bitcast(cand, F32), 1.0, 0.0), axis=1, keepdims=True)
        return jnp.where(cnt >= CAP, cand, cur)

    thr = pltpu.bitcast(lax.fori_loop(0, 31, bit_step, jnp.zeros((N_EXP, 1), jnp.int32)), F32)
    above = aff > thr
    tied = aff == thr
    room = CAP - jnp.sum(jnp.where(above, 1.0, 0.0), axis=1, keepdims=True)
    _prefix_sum(jnp.where(tied, 1.0, 0.0), cs_ref)
    sel = jnp.logical_or(above, jnp.logical_and(tied, cs_ref[...] <= room))
    gsel = jnp.where(sel, aff, 0.0)
    _prefix_sum(jnp.where(sel, 1.0, 0.0), cs_ref)

    slot = lax.broadcasted_iota(jnp.int32, (CAP, n), 0).astype(F32)
    g_hi = gsel.astype(BF16).astype(F32)
    rest = gsel - g_hi
    g_mid = rest.astype(BF16).astype(F32)
    g_lo = rest - g_mid
    ones = jnp.ones((1, n), F32)
    pad = jnp.zeros((4, n), F32)
    last_slot = lax.broadcasted_iota(jnp.int32, (1, CAP), 1) == CAP - 1
    on_diag = (lax.broadcasted_iota(jnp.int32, (CAP, CAP), 0)
               == lax.broadcasted_iota(jnp.int32, (CAP, CAP), 1))
    total = jnp.sum(gsel, axis=1, keepdims=True)
    for e in range(N_EXP):
        row = slice(e, e + 1)
        le = jnp.where(cs_ref[row, :] <= slot, 1.0, 0.0).astype(BF16)
        lhs = jnp.concatenate([ones, g_hi[row], g_mid[row], g_lo[row], pad], axis=0).astype(BF16)
        res = lax.dot_general(lhs, le, (((1,), (1,)), ((), ())), preferred_element_type=F32)
        idx_ref[0, row, :] = res[0:1, :].astype(jnp.int32)
        cum = res[1:2, :] + res[2:3, :] + res[3:4, :]
        gate = jnp.where(last_slot, total[row], pltpu.roll(cum, CAP - 1, axis=1)) - cum
        gate_ref[e] = jnp.sum(jnp.where(on_diag, gate, 0.0), axis=1, keepdims=True)


def _select(logits_t):
    return pl.pallas_call(
        _select_kernel,
        grid=(N_SETS,),
        in_specs=[pl.BlockSpec((N_EXP, N_CTX), lambda s: (0, s))],
        out_specs=[pl.BlockSpec((1, N_EXP, CAP), lambda s: (s, 0, 0)),
                   pl.BlockSpec((N_EXP, CAP, 1), lambda s: (0, s, 0))],
        out_shape=[jax.ShapeDtypeStruct((N_SETS, N_EXP, CAP), jnp.int32),
                   jax.ShapeDtypeStruct((N_EXP, N_SETS * CAP, 1), F32)],
        scratch_shapes=[pltpu.VMEM((N_EXP, N_CTX), F32)],
        compiler_params=_params(("arbitrary",)),
        name="select",
    )(logits_t)


MOE_ROWS = N_SETS * CAP
MOE_TF = 256
MOE_NF = D_EXP // MOE_TF
MOE_TN = 512
N_FILL = N_TOK // MOE_ROWS
assert MOE_NF == 4


def _moe_kernel(rows_ref, h_hbm, gate_ref, wg_ref, wu_ref, wd_ref, y_hbm,
                xf_ref, xb_ref, acc_ref, rmw_ref, sem_x, sem_put, sem_get, sem_fill):
    e = pl.program_id(0)
    f = pl.program_id(1)
    e_next = jnp.minimum(e + 1, N_EXP - 1)

    def tok_row(hbm_ref, ex, k):
        return hbm_ref.at[pl.ds(rows_ref[ex * MOE_ROWS + k], 1), :]

    def x_copy(ex, k):
        return pltpu.make_async_copy(tok_row(h_hbm, ex, k), xf_ref.at[pl.ds(k, 1), :], sem_x)

    def put_copy(ex, k):
        return pltpu.make_async_copy(rmw_ref.at[pl.ds(k, 1), :], tok_row(y_hbm, ex, k), sem_put)

    def get_copy(ex, k):
        return pltpu.make_async_copy(tok_row(y_hbm, ex, k), rmw_ref.at[pl.ds(k, 1), :], sem_get)

    def fill_copy(c):
        return pltpu.make_async_copy(rmw_ref, y_hbm.at[pl.ds(c * MOE_ROWS, MOE_ROWS), :], sem_fill)

    def start_rows(copy, ex):
        for k in range(MOE_ROWS):
            copy(ex, k).start(priority=k % 2)

    def wait_rows(sem, src, dst):
        pltpu.make_async_copy(src, dst, sem).wait()

    y_rows = y_hbm.at[pl.ds(0, MOE_ROWS), :]
    h_rows = h_hbm.at[pl.ds(0, MOE_ROWS), :]

    def ffn_step(first=False):
        x = xb_ref[...]
        g = jnp.dot(x, wg_ref[...].astype(BF16), preferred_element_type=F32)
        up = jnp.dot(x, wu_ref[...].astype(BF16), preferred_element_type=F32)
        hid = (_silu(g) * up).astype(BF16)
        wd = wd_ref[...].astype(BF16)
        for c in range(D // MOE_TN):
            cols = slice(c * MOE_TN, (c + 1) * MOE_TN)
            part = jnp.dot(hid, wd[:, cols], preferred_element_type=F32)
            if first:
                acc_ref[:, cols] = part
            else:
                acc_ref[:, cols] += part

    @pl.when(jnp.logical_and(e == 0, f == 0))
    def _():
        rmw_ref[...] = jnp.zeros(rmw_ref.shape, F32)
        for c in range(N_FILL):
            fill_copy(c).start()

        def issue(k, c):
            x_copy(0, k).start()
            return c
        lax.fori_loop(0, MOE_ROWS, issue, 0)

    @pl.when(f == 0)
    def _():
        wait_rows(sem_x, h_rows, xf_ref)
        xb_ref[...] = xf_ref[...].astype(BF16)

    @pl.when(jnp.logical_and(e > 0, f == 0))
    def _():
        start_rows(put_copy, e - 1)
        ffn_step(first=True)

    @pl.when(jnp.logical_and(e == 0, f == 0))
    def _():
        ffn_step(first=True)

    @pl.when(f == 1)
    def _():
        start_rows(x_copy, e_next)
        ffn_step()

    @pl.when(jnp.logical_and(e == 0, f == 2))
    def _():
        for c in range(N_FILL):
            fill_copy(c).wait()

    @pl.when(jnp.logical_and(e > 0, f == 2))
    def _():
        wait_rows(sem_put, rmw_ref, y_rows)

    @pl.when(f == 2)
    def _():
        start_rows(get_copy, e)
        ffn_step()

    @pl.when(f == 3)
    def _():
        ffn_step()
        wait_rows(sem_get, y_rows, rmw_ref)
        rmw_ref[...] = rmw_ref[...] + gate_ref[...] * acc_ref[...]

    @pl.when(jnp.logical_and(e == N_EXP - 1, f == MOE_NF - 1))
    def _():
        start_rows(put_copy, N_EXP - 1)
        wait_rows(sem_put, rmw_ref, y_rows)
        wait_rows(sem_x, h_rows, xf_ref)


def _moe(rows, h2, gates, w_g, w_u, w_d):
    return pl.pallas_call(
        _moe_kernel,
        grid_spec=pltpu.PrefetchScalarGridSpec(
            num_scalar_prefetch=1,
            grid=(N_EXP, MOE_NF),
            in_specs=[pl.BlockSpec(memory_space=pl.ANY),
                      pl.BlockSpec((None, MOE_ROWS, 1), lambda e, f, s: (e, 0, 0)),
                      pl.BlockSpec((None, D, MOE_TF), lambda e, f, s: (e, 0, f)),
                      pl.BlockSpec((None, D, MOE_TF), lambda e, f, s: (e, 0, f)),
                      pl.BlockSpec((None, MOE_TF, D), lambda e, f, s: (e, f, 0))],
            out_specs=pl.BlockSpec(memory_space=pl.ANY),
            scratch_shapes=[pltpu.VMEM((MOE_ROWS, D), F32),
                            pltpu.VMEM((MOE_ROWS, D), BF16),
                            pltpu.VMEM((MOE_ROWS, D), F32),
                            pltpu.VMEM((MOE_ROWS, D), F32),
                            pltpu.SemaphoreType.DMA(()),
                            pltpu.SemaphoreType.DMA(()),
                            pltpu.SemaphoreType.DMA(()),
                            pltpu.SemaphoreType.DMA(())]),
        out_shape=jax.ShapeDtypeStruct((N_TOK, D), F32),
        compiler_params=_params(("arbitrary", "arbitrary")),
        name="moe",
    )(rows, h2, gates, w_g, w_u, w_d)


def _final_kernel(x1_ref, y_ref, mod_ref, g_ref, o_ref):
    o_ref[...] = _rms(x1_ref[...] + mod_ref[0][5:6, :] * y_ref[...], g_ref[...])


def _final(x1, y, tok0, n_tok, mod, fng):
    tm = 512
    blk0 = tok0 // tm
    return pl.pallas_call(
        _final_kernel,
        grid=(n_tok // tm,),
        in_specs=[pl.BlockSpec((tm, D), lambda i: (blk0 + i, 0)),
                  pl.BlockSpec((tm, D), lambda i: (blk0 + i, 0)),
                  pl.BlockSpec((1, N_MOD, D), lambda i: (_mod_row(blk0 + i, tm), 0, 0)),
                  pl.BlockSpec((1, D), lambda i: (0, 0))],
        out_specs=pl.BlockSpec((tm, D), lambda i: (i, 0)),
        out_shape=jax.ShapeDtypeStruct((n_tok, D), F32),
        compiler_params=_params(("arbitrary",)),
        name="final",
    )(x1, y, mod, fng)


def kernel(x_prompt, x_sample, c, cache_k, cache_v, c_ctx, w_ada, b_ada, norm1_g, w_in, conv_dw, conv_dw_b, conv_ln_g, conv_ln_b, w_conv_out, attn_sink, w_attn_out, w_out, norm2_g, w_router, b_router, w_exp_gate, w_exp_up, w_exp_down, final_norm_g):
    x_ctx = x_prompt.reshape(N_CTX, D)
    x_lat = x_sample.reshape(N_LAT, D)
    cond = jnp.concatenate([c, c_ctx[None, :], jnp.zeros((MOD_ROWS - N_LAT_SEQ - 1, D), F32)], axis=0)
    mod = _ada(cond, w_ada[0], b_ada[0]).reshape(MOD_ROWS, N_MOD, D)

    u = _inproj(_prenorm(x_ctx, x_lat, mod, norm1_g), w_in[0])
    zc = _conv(u, conv_dw[0], conv_dw_b, conv_ln_g, conv_ln_b)

    sink = attn_sink[0]
    cos, sin = _rope_tables()
    ck = cache_k[:, 0].reshape(N_LAT_SEQ, PAST, N_KV * HD)
    cv = cache_v[:, 0].reshape(N_LAT_SEQ, PAST, N_KV * HD)
    att_ctx, state_k, state_v = _ctx_attn(u, sink)
    mixed = _mix(zc, att_ctx, _lat_attn(u, sink, ck, cv, cos, sin), u,
                 w_conv_out[0], w_attn_out[0])

    x1, h2, logits = _outproj(mixed, x_ctx, x_lat, mod, norm2_g, w_out[0].astype(BF16),
                              w_router[0], b_router)

    idx, gates = _select(logits.T)
    rows = (idx + (jnp.arange(N_SETS, dtype=jnp.int32) * N_CTX)[:, None, None])
    rows = rows.transpose(1, 0, 2).reshape(N_EXP * MOE_ROWS)
    ffn = _moe(rows, h2, gates, w_exp_gate[0], w_exp_up[0], w_exp_down[0])

    fng = final_norm_g.reshape(1, D)
    y_prompt = _final(x1, ffn, 0, N_CTX, mod, fng).reshape(N_CTX_SEQ, CTX_LEN, D)
    y_sample = _final(x1, ffn, N_CTX, N_LAT, mod, fng).reshape(N_LAT_SEQ, LAT_LEN, D)
    return (y_prompt, y_sample, state_k, state_v)
```

```python
import functools

import jax
import jax.numpy as jnp
from jax import lax
from jax.experimental import pallas as pl
from jax.experimental.pallas import tpu as pltpu

D = 2048
N_CTX_SEQ, CTX_LEN = 16, 256
N_LAT_SEQ, LAT_LEN = 4, 1024
N_CTX = N_CTX_SEQ * CTX_LEN
N_LAT = N_LAT_SEQ * LAT_LEN
N_TOK = N_CTX + N_LAT
PAST = 512
GRID_W = 64
N_HEADS, N_KV, HD = 16, 4, 128
Q_GROUP = N_HEADS // N_KV
WINDOW = 128
ABLK = 128
CONV_W = 1024
CONV_K = 31
CONV_PAD = CONV_K // 2
N_EXP = 16
D_EXP = 1024
CAP = 2 * N_CTX // N_EXP
N_SETS = 2
ROPE_BASE = 10000.0
EPS = 1e-6
N_MOD = 6
NEG = -1e30
IN_COLS = 2 * CONV_W + N_HEADS * HD + 2 * N_KV * HD + 2 * D
COL_Q = 2 * CONV_W
COL_K = COL_Q + N_HEADS * HD
COL_V = COL_K + N_KV * HD
COL_GA = COL_V + N_KV * HD
COL_GB = COL_GA + D
MOD_ROWS = 8
CTX_MOD_ROW = N_LAT_SEQ

VMEM_LIMIT = 58 * 1024 * 1024
F32 = jnp.float32
BF16 = jnp.bfloat16


def _params(sem, vmem=VMEM_LIMIT):
    return pltpu.CompilerParams(dimension_semantics=sem, vmem_limit_bytes=vmem)


def _mod_row(tok_block, block_tokens):
    ctx_blocks = N_CTX // block_tokens
    per_seq = LAT_LEN // block_tokens
    return jnp.where(tok_block < ctx_blocks, CTX_MOD_ROW, (tok_block - ctx_blocks) // per_seq)


def _silu(x):
    return x * jax.nn.sigmoid(x)


def _rms(x, g):
    return x * lax.rsqrt(jnp.mean(x * x, axis=-1, keepdims=True) + EPS) * g


def _ada_kernel(cond_ref, w_ref, b_ref, o_ref):
    a = _silu(cond_ref[...]).astype(BF16)
    o_ref[...] = jnp.dot(a, w_ref[...].astype(BF16), preferred_element_type=F32) + b_ref[...]


def _ada(cond, w_ada, b_ada):
    tn = 1024
    n = N_MOD * D
    return pl.pallas_call(
        _ada_kernel,
        grid=(n // tn,),
        in_specs=[pl.BlockSpec((MOD_ROWS, D), lambda j: (0, 0)),
                  pl.BlockSpec((D, tn), lambda j: (0, j)),
                  pl.BlockSpec((1, tn), lambda j: (0, j))],
        out_specs=pl.BlockSpec((MOD_ROWS, tn), lambda j: (0, j)),
        out_shape=jax.ShapeDtypeStruct((MOD_ROWS, n), F32),
        compiler_params=_params(("arbitrary",)),
        name="ada",
    )(cond, w_ada, b_ada.reshape(1, n))


def _set_specs(tm, buffers):
    nc = N_CTX // tm
    mode = pl.Buffered(buffers)
    ctx = pl.BlockSpec((tm, D), lambda i, *_: (jnp.minimum(i, nc - 1), 0), pipeline_mode=mode)
    lat = pl.BlockSpec((tm, D), lambda i, *_: (jnp.maximum(i - nc, 0), 0), pipeline_mode=mode)
    return ctx, lat


def _prenorm_kernel(xc_ref, xl_ref, mod_ref, g_ref, h_ref):
    is_ctx = pl.program_id(0) < N_CTX // xc_ref.shape[0]
    x = jnp.where(is_ctx, xc_ref[...], xl_ref[...])
    m = mod_ref[0]
    h = _rms(x, g_ref[...]) * (1.0 + m[1:2, :]) + m[0:1, :]
    h_ref[...] = h.astype(BF16)


def _prenorm(x_ctx, x_lat, mod, n1g):
    tm = 512
    xc_spec, xl_spec = _set_specs(tm, 2)
    return pl.pallas_call(
        _prenorm_kernel,
        grid=(N_TOK // tm,),
        in_specs=[xc_spec, xl_spec,
                  pl.BlockSpec((1, N_MOD, D), lambda i: (_mod_row(i, tm), 0, 0)),
                  pl.BlockSpec((1, D), lambda i: (0, 0))],
        out_specs=pl.BlockSpec((tm, D), lambda i: (i, 0)),
        out_shape=jax.ShapeDtypeStruct((N_TOK, D), BF16),
        compiler_params=_params(("arbitrary",)),
        name="prenorm",
    )(x_ctx, x_lat, mod, n1g)


def _inproj_kernel(h_ref, w_ref, o_ref, wb_ref):
    @pl.when(pl.program_id(1) == 0)
    def _():
        wb_ref[...] = w_ref[...].astype(BF16)

    o_ref[...] = jnp.dot(h_ref[...], wb_ref[...], preferred_element_type=F32)


def _inproj(h, w_in):
    tm, tn = 1024, 1024
    return pl.pallas_call(
        _inproj_kernel,
        grid=(IN_COLS // tn, N_TOK // tm),
        in_specs=[pl.BlockSpec((tm, D), lambda j, i: (i, 0)),
                  pl.BlockSpec((D, tn), lambda j, i: (0, j))],
        out_specs=pl.BlockSpec((tm, tn), lambda j, i: (i, j)),
        out_shape=jax.ShapeDtypeStruct((N_TOK, IN_COLS), F32),
        scratch_shapes=[pltpu.VMEM((D, tn), BF16)],
        compiler_params=_params(("arbitrary", "arbitrary")),
        name="inproj",
    )(h, w_in)


CONV_T = 256
CONV_HALO = 16
CONV_ROWS = 32
SUBLANES = 8
LANES = 128
CONV_ZS = CONV_T + (CONV_K // SUBLANES) * SUBLANES


def _conv_kernel(main_ref, prev_ref, next_ref, w_ref, b_ref, lg_ref, lb_ref, o_ref,
                 z_ref, zs_ref, y_ref):
    i = pl.program_id(0)
    ctx_blocks = N_CTX // CONV_T
    per_seq = LAT_LEN // CONV_T
    pos = (i - ctx_blocks) % per_seq
    is_lat = i >= ctx_blocks
    has_prev = jnp.logical_and(is_lat, pos > 0)
    has_next = jnp.logical_and(is_lat, pos < per_seq - 1)

    def glu(ref):
        v = ref[...]
        return v[:, :CONV_W] * jax.nn.sigmoid(v[:, CONV_W:])

    z_ref[0:CONV_HALO, :] = jnp.where(has_prev, glu(prev_ref), 0.0)
    z_ref[CONV_HALO:CONV_HALO + CONV_T, :] = glu(main_ref)
    z_ref[CONV_HALO + CONV_T:, :] = jnp.where(has_next, glu(next_ref), 0.0)

    z = z_ref[...]
    zs_ref[0] = z[:CONV_ZS]
    for s in range(1, SUBLANES):
        zs_ref[s] = pltpu.roll(z, z.shape[0] - s, axis=0)[:CONV_ZS]

    def channel_group(c, carry):
        cols = pl.ds(pl.multiple_of(c * LANES, LANES), LANES)
        acc = jnp.zeros((CONV_T, LANES), F32)
        for j in range(CONV_K):
            a, s = divmod(CONV_HALO - CONV_PAD + j, SUBLANES)
            acc = acc + zs_ref[s, a * SUBLANES:a * SUBLANES + CONV_T, cols] * w_ref[j:j + 1, cols]
        y_ref[:, cols] = acc + b_ref[:, cols]
        return carry

    lax.fori_loop(0, CONV_W // LANES, channel_group, 0)

    for r in range(CONV_T // CONV_ROWS):
        acc = y_ref[r * CONV_ROWS:(r + 1) * CONV_ROWS, :]
        mu = jnp.mean(acc, axis=-1, keepdims=True)
        cen = acc - mu
        var = jnp.mean(cen * cen, axis=-1, keepdims=True)
        y = cen * lax.rsqrt(var + EPS) * lg_ref[...] + lb_ref[...]
        o_ref[r * CONV_ROWS:(r + 1) * CONV_ROWS, :] = _silu(y).astype(BF16)


def _conv(u, w_dw, b_dw, ln_g, ln_b):
    nblk = N_TOK // CONV_T
    hpb = CONV_T // CONV_HALO
    last = N_TOK // CONV_HALO - 1
    row = lambda i: (0, 0)
    return pl.pallas_call(
        _conv_kernel,
        grid=(nblk,),
        in_specs=[pl.BlockSpec((CONV_T, 2 * CONV_W), lambda i: (i, 0)),
                  pl.BlockSpec((CONV_HALO, 2 * CONV_W), lambda i: (jnp.maximum(i * hpb - 1, 0), 0)),
                  pl.BlockSpec((CONV_HALO, 2 * CONV_W), lambda i: (jnp.minimum((i + 1) * hpb, last), 0)),
                  pl.BlockSpec((CONV_K, CONV_W), row),
                  pl.BlockSpec((1, CONV_W), row),
                  pl.BlockSpec((1, CONV_W), row),
                  pl.BlockSpec((1, CONV_W), row)],
        out_specs=pl.BlockSpec((CONV_T, CONV_W), lambda i: (i, 0)),
        out_shape=jax.ShapeDtypeStruct((N_TOK, CONV_W), BF16),
        scratch_shapes=[pltpu.VMEM((CONV_T + 2 * CONV_HALO, CONV_W), F32),
                        pltpu.VMEM((SUBLANES, CONV_ZS, CONV_W), F32),
                        pltpu.VMEM((CONV_T, CONV_W), F32)],
        compiler_params=_params(("arbitrary",)),
        name="conv",
    )(u, u, u, w_dw, b_dw, ln_g, ln_b)


def _stack_heads(q):
    return jnp.concatenate([q[:, g * HD:(g + 1) * HD] for g in range(Q_GROUP)], axis=0)


def _unstack_heads(o, rows):
    return jnp.concatenate([o[g * rows:(g + 1) * rows, :] for g in range(Q_GROUP)], axis=1)


LOG2E = 1.4426950408889634
Q_SCALE = HD ** -0.5 * LOG2E
QW = Q_GROUP * HD


def _sink_column(sink_ref, kv, rows):
    rid = lax.broadcasted_iota(jnp.int32, (Q_GROUP * rows, 1), 0)
    col = jnp.full((Q_GROUP * rows, 1), sink_ref[kv * Q_GROUP], F32)
    for g in range(1, Q_GROUP):
        col = jnp.where(rid >= g * rows, sink_ref[kv * Q_GROUP + g], col)
    return col * LOG2E


def _with_ones(vals):
    return jnp.concatenate([vals, jnp.ones(vals.shape, vals.dtype)], axis=1)


def _softmax_av(s, sk, vals):
    m = jnp.maximum(jnp.max(s, axis=-1, keepdims=True), sk)
    if vals.shape[1] == 2 * HD:
        p = jnp.exp2(s - m).astype(BF16)
        o = jnp.dot(p, vals, preferred_element_type=F32)
        return o[:, :HD] / (o[:, HD:HD + 1] + jnp.exp2(sk - m))
    p = jnp.exp2(s - m)
    den = jnp.sum(p, axis=-1, keepdims=True) + jnp.exp2(sk - m)
    return jnp.dot(p.astype(BF16), vals, preferred_element_type=F32) / den


_NT = (((1,), (1,)), ((), ()))


def _ctx_attn_kernel(sink_ref, q_ref, k_ref, v_ref, o_ref, sk_ref, sv_ref):
    for kv in range(N_KV):
        k = k_ref[:, kv * HD:(kv + 1) * HD]
        v = v_ref[:, kv * HD:(kv + 1) * HD]
        sk_ref[0, 0, :, kv, :] = k
        sv_ref[0, 0, :, kv, :] = v
        q = (_stack_heads(q_ref[:, kv * QW:(kv + 1) * QW]) * Q_SCALE).astype(BF16)
        s = lax.dot_general(q, k.astype(BF16), _NT, preferred_element_type=F32)
        o = _softmax_av(s, _sink_column(sink_ref, kv, CTX_LEN), v.astype(BF16))
        o_ref[:, kv * QW:(kv + 1) * QW] = _unstack_heads(o, CTX_LEN).astype(BF16)


def _ctx_attn(u, sink):
    qcols, kcols = N_HEADS * HD, N_KV * HD
    state_spec = pl.BlockSpec((1, 1, CTX_LEN, N_KV, HD), lambda b, s: (b, 0, 0, 0, 0))
    state_shape = jax.ShapeDtypeStruct((N_CTX_SEQ, 1, CTX_LEN, N_KV, HD), F32)
    return pl.pallas_call(
        _ctx_attn_kernel,
        grid_spec=pltpu.PrefetchScalarGridSpec(
            num_scalar_prefetch=1,
            grid=(N_CTX_SEQ,),
            in_specs=[pl.BlockSpec((CTX_LEN, qcols), lambda b, s: (b, COL_Q // qcols)),
                      pl.BlockSpec((CTX_LEN, kcols), lambda b, s: (b, COL_K // kcols)),
                      pl.BlockSpec((CTX_LEN, kcols), lambda b, s: (b, COL_V // kcols))],
            out_specs=[pl.BlockSpec((CTX_LEN, qcols), lambda b, s: (b, 0)),
                       state_spec, state_spec]),
        out_shape=[jax.ShapeDtypeStruct((N_CTX, N_HEADS * HD), BF16), state_shape, state_shape],
        compiler_params=_params(("arbitrary",)),
        name="ctx_attn",
    )(sink, u, u, u)


def _rope(x, cos, sin):
    lane = lax.broadcasted_iota(jnp.int32, x.shape, 1)
    fwd = pltpu.roll(x, HD - 32, axis=1)
    bwd = pltpu.roll(x, 32, axis=1)
    partner = jnp.where((lane & 32) == 0, fwd, bwd)
    return x * cos + partner * sin


def _lat_attn_kernel(sink_ref, q_ref, k_ref, v_ref, ck_ref, cv_ref, cos_ref, sin_ref, o_ref):
    kv = pl.program_id(1)
    nblk = LAT_LEN // ABLK
    cos, sin = cos_ref[...], sin_ref[...]
    k_rot = _rope(k_ref[...], cos, sin).astype(BF16)
    v = _with_ones(v_ref[...].astype(BF16))
    ck = ck_ref[...].astype(BF16)
    cv = _with_ones(cv_ref[...].astype(BF16))
    sk = _sink_column(sink_ref, kv, ABLK)
    r = lax.broadcasted_iota(jnp.int32, (Q_GROUP * ABLK, ABLK), 0) & (ABLK - 1)
    c = lax.broadcasted_iota(jnp.int32, (Q_GROUP * ABLK, ABLK), 1)
    open_tile = jnp.zeros((Q_GROUP * ABLK, ABLK), F32)
    below = jnp.where(c >= r, 0.0, NEG)
    above = jnp.where(c <= r, 0.0, NEG)
    ctx_open = jnp.zeros((Q_GROUP * ABLK, PAST), F32)
    bias_first = jnp.concatenate([open_tile, above, ctx_open], axis=1)
    bias_mid = jnp.concatenate([below, open_tile, above, ctx_open], axis=1)
    bias_last = jnp.concatenate([below, open_tile, ctx_open], axis=1)
    for j in range(nblk):
        rows = slice(j * ABLK, (j + 1) * ABLK)
        cos_j, sin_j = cos[rows], sin[rows]
        qs = jnp.concatenate(
            [_rope(q_ref[rows, g * HD:(g + 1) * HD], cos_j, sin_j) * Q_SCALE for g in range(Q_GROUP)],
            axis=0).astype(BF16)
        blocks = range(max(j - 1, 0), min(j + 2, nblk))
        band = slice(blocks[0] * ABLK, (blocks[-1] + 1) * ABLK)
        keys = jnp.concatenate([k_rot[band], ck], axis=0)
        vals = jnp.concatenate([v[band], cv], axis=0)
        bias = bias_first if j == 0 else bias_last if j == nblk - 1 else bias_mid
        s = lax.dot_general(qs, keys, _NT, preferred_element_type=F32) + bias
        o = _softmax_av(s, sk, vals)
        o_ref[rows, :] = _unstack_heads(o, ABLK).astype(BF16)


def _lat_attn(u, sink, ck, cv, cos, sin):
    base = N_CTX // LAT_LEN
    cache_spec = pl.BlockSpec((None, PAST, HD), lambda b, kv, s: (b, 0, kv))
    table_spec = pl.BlockSpec((LAT_LEN, HD), lambda b, kv, s: (0, 0))
    return pl.pallas_call(
        _lat_attn_kernel,
        grid_spec=pltpu.PrefetchScalarGridSpec(
            num_scalar_prefetch=1,
            grid=(N_LAT_SEQ, N_KV),
            in_specs=[pl.BlockSpec((LAT_LEN, QW), lambda b, kv, s: (base + b, COL_Q // QW + kv)),
                      pl.BlockSpec((LAT_LEN, HD), lambda b, kv, s: (base + b, COL_K // HD + kv)),
                      pl.BlockSpec((LAT_LEN, HD), lambda b, kv, s: (base + b, COL_V // HD + kv)),
                      cache_spec, cache_spec, table_spec, table_spec],
            out_specs=pl.BlockSpec((LAT_LEN, QW), lambda b, kv, s: (b, kv))),
        out_shape=jax.ShapeDtypeStruct((N_LAT, N_HEADS * HD), BF16),
        compiler_params=_params(("arbitrary", "arbitrary")),
        name="lat_attn",
    )(sink, u, u, u, ck, cv, cos, sin)


def _rope_tables():
    n = jnp.arange(LAT_LEN)
    row = (n // GRID_W).astype(F32)
    col = (n % GRID_W).astype(F32)
    quarter = HD // 4
    inv = ROPE_BASE ** (-jnp.arange(quarter, dtype=F32) / quarter)
    ang_r = row[:, None] * inv[None, :]
    ang_c = col[:, None] * inv[None, :]
    cos = jnp.concatenate([jnp.cos(ang_r)] * 2 + [jnp.cos(ang_c)] * 2, axis=-1)
    sin = jnp.concatenate([-jnp.sin(ang_r), jnp.sin(ang_r), -jnp.sin(ang_c), jnp.sin(ang_c)], axis=-1)
    return cos, sin


def _mix_kernel(z_ref, ac_ref, al_ref, ga_ref, gb_ref, wc_ref, wa_ref, o_ref, wcb_ref, wab_ref):
    @pl.when(pl.program_id(1) == 0)
    def _():
        wcb_ref[...] = wc_ref[...].astype(BF16)
        wab_ref[...] = wa_ref[...].astype(BF16)

    is_ctx = pl.program_id(1) < N_CTX // z_ref.shape[0]
    att = jnp.where(is_ctx, ac_ref[...], al_ref[...])
    conv_out = jnp.dot(z_ref[...], wcb_ref[...], preferred_element_type=F32)
    attn_out = jnp.dot(att, wab_ref[...], preferred_element_type=F32)
    mixed = jax.nn.sigmoid(ga_ref[...]) * conv_out + jax.nn.sigmoid(gb_ref[...]) * attn_out
    o_ref[...] = mixed.astype(BF16)


def _mix(zc, att_ctx, att_lat, u, w_conv_out, w_attn_out):
    tm, tn = 512, 1024
    nc = N_CTX // tm
    return pl.pallas_call(
        _mix_kernel,
        grid=(D // tn, N_TOK // tm),
        in_specs=[pl.BlockSpec((tm, CONV_W), lambda j, i: (i, 0)),
                  pl.BlockSpec((tm, N_HEADS * HD), lambda j, i: (jnp.minimum(i, nc - 1), 0)),
                  pl.BlockSpec((tm, N_HEADS * HD), lambda j, i: (jnp.maximum(i - nc, 0), 0)),
                  pl.BlockSpec((tm, tn), lambda j, i: (i, COL_GA // tn + j)),
                  pl.BlockSpec((tm, tn), lambda j, i: (i, COL_GB // tn + j)),
                  pl.BlockSpec((CONV_W, tn), lambda j, i: (0, j)),
                  pl.BlockSpec((N_HEADS * HD, tn), lambda j, i: (0, j))],
        out_specs=pl.BlockSpec((tm, tn), lambda j, i: (i, j)),
        out_shape=jax.ShapeDtypeStruct((N_TOK, D), BF16),
        scratch_shapes=[pltpu.VMEM((CONV_W, tn), BF16), pltpu.VMEM((N_HEADS * HD, tn), BF16)],
        compiler_params=_params(("arbitrary", "arbitrary")),
        name="mix",
    )(zc, att_ctx, att_lat, u, u, w_conv_out, w_attn_out)


def _split_bf16(x):
    hi = x.astype(BF16)
    lo = (x - hi.astype(F32)).astype(BF16)
    return hi, lo


def _outproj_kernel(mx_ref, xc_ref, xl_ref, mod_ref, g_ref, w_ref, wr_ref, br_ref,
                    x1_ref, h2_ref, lg_ref):
    is_ctx = pl.program_id(0) < N_CTX // xc_ref.shape[0]
    x = jnp.where(is_ctx, xc_ref[...], xl_ref[...])
    m = mod_ref[0]
    x1 = x + m[2:3, :] * jnp.dot(mx_ref[...], w_ref[...], preferred_element_type=F32)
    x1_ref[...] = x1
    h2 = _rms(x1, g_ref[...]) * (1.0 + m[4:5, :]) + m[3:4, :]
    h2_ref[...] = h2
    h_hi, h_lo = _split_bf16(h2)
    w_hi, w_lo = _split_bf16(wr_ref[...])
    lg = (jnp.dot(h_hi, w_hi, preferred_element_type=F32)
          + jnp.dot(h_hi, w_lo, preferred_element_type=F32)
          + jnp.dot(h_lo, w_hi, preferred_element_type=F32))
    lg_ref[...] = lg + br_ref[...]


def _outproj(mixed, x_ctx, x_lat, mod, n2g, w_out_bf, w_router, b_router):
    tm = 512
    const = lambda i: (0, 0)
    xc_spec, xl_spec = _set_specs(tm, 2)
    return pl.pallas_call(
        _outproj_kernel,
        grid=(N_TOK // tm,),
        in_specs=[pl.BlockSpec((tm, D), lambda i: (i, 0)),
                  xc_spec, xl_spec,
                  pl.BlockSpec((1, N_MOD, D), lambda i: (_mod_row(i, tm), 0, 0)),
                  pl.BlockSpec((1, D), const),
                  pl.BlockSpec((D, D), const, pipeline_mode=pl.Buffered(1)),
                  pl.BlockSpec((D, N_EXP), const),
                  pl.BlockSpec((1, N_EXP), const)],
        out_specs=[pl.BlockSpec((tm, D), lambda i: (i, 0)),
                   pl.BlockSpec((tm, D), lambda i: (i, 0)),
                   pl.BlockSpec((tm, N_EXP), lambda i: (i, 0))],
        out_shape=[jax.ShapeDtypeStruct((N_TOK, D), F32),
                   jax.ShapeDtypeStruct((N_TOK, D), F32),
                   jax.ShapeDtypeStruct((N_TOK, N_EXP), F32)],
        compiler_params=_params(("arbitrary",)),
        name="outproj",
    )(mixed, x_ctx, x_lat, mod, n2g, w_out_bf, w_router, b_router)


SEL_CHUNK = 128


def _prefix_sum(flags, out_ref):
    n = flags.shape[1]
    ri = lax.broadcasted_iota(jnp.int32, (SEL_CHUNK, SEL_CHUNK), 0)
    ci = lax.broadcasted_iota(jnp.int32, (SEL_CHUNK, SEL_CHUNK), 1)
    tri = jnp.where(ri <= ci, 1.0, 0.0).astype(BF16)
    run = jnp.zeros((N_EXP, 1), F32)
    for ch in range(n // SEL_CHUNK):
        blk = flags[:, ch * SEL_CHUNK:(ch + 1) * SEL_CHUNK].astype(BF16)
        within = jnp.dot(blk, tri, preferred_element_type=F32) + run
        out_ref[:, ch * SEL_CHUNK:(ch + 1) * SEL_CHUNK] = within
        run = within[:, SEL_CHUNK - 1:SEL_CHUNK]


def _select_kernel(lg_ref, idx_ref, gate_ref, cs_ref):
    n = lg_ref.shape[1]
    lg = lg_ref[...]
    ex = jnp.exp(lg - jnp.max(lg, axis=0, keepdims=True))
    aff = ex / jnp.sum(ex, axis=0, keepdims=True)

    def bit_step(b, cur):
        cand = cur | (1 << (30 - b))
        cnt = jnp.sum(jnp.where(aff >= pltpu.bitcast(cand, F32), 1.0, 0.0), axis=1, keepdims=True)
        return jnp.where(cnt >= CAP, cand, cur)

    thr = pltpu.bitcast(lax.fori_loop(0, 31, bit_step, jnp.zeros((N_EXP, 1), jnp.int32)), F32)
    above = aff > thr
    tied = aff == thr
    room = CAP - jnp.sum(jnp.where(above, 1.0, 0.0), axis=1, keepdims=True)
    _prefix_sum(jnp.where(tied, 1.0, 0.0), cs_ref)
    sel = jnp.logical_or(above, jnp.logical_and(tied, cs_ref[...] <= room))
    gsel = jnp.where(sel, aff, 0.0)
    _prefix_sum(jnp.where(sel, 1.0, 0.0), cs_ref)

    slot = lax.broadcasted_iota(jnp.int32, (CAP, n), 0).astype(F32)
    g_hi = gsel.astype(BF16).astype(F32)
    rest = gsel - g_hi
    g_mid = rest.astype(BF16).astype(F32)
    g_lo = rest - g_mid
    ones = jnp.ones((1, n), F32)
    pad = jnp.zeros((4, n), F32)
    last_slot = lax.broadcasted_iota(jnp.int32, (1, CAP), 1) == CAP - 1
    on_diag = (lax.broadcasted_iota(jnp.int32, (CAP, CAP), 0)
               == lax.broadcasted_iota(jnp.int32, (CAP, CAP), 1))
    total = jnp.sum(gsel, axis=1, keepdims=True)
    for e in range(N_EXP):
        row = slice(e, e + 1)
        le = jnp.where(cs_ref[row, :] <= slot, 1.0, 0.0).astype(BF16)
        lhs = jnp.concatenate([ones, g_hi[row], g_mid[row], g_lo[row], pad], axis=0).astype(BF16)
        res = lax.dot_general(lhs, le, (((1,), (1,)), ((), ())), preferred_element_type=F32)
        idx_ref[0, row, :] = res[0:1, :].astype(jnp.int32)
        cum = res[1:2, :] + res[2:3, :] + res[3:4, :]
        gate = jnp.where(last_slot, total[row], pltpu.roll(cum, CAP - 1, axis=1)) - cum
        gate_ref[e] = jnp.sum(jnp.where(on_diag, gate, 0.0), axis=1, keepdims=True)


def _select(logits_t):
    return pl.pallas_call(
        _select_kernel,
        grid=(N_SETS,),
        in_specs=[pl.BlockSpec((N_EXP, N_CTX), lambda s: (0, s))],
        out_specs=[pl.BlockSpec((1, N_EXP, CAP), lambda s: (s, 0, 0)),
                   pl.BlockSpec((N_EXP, CAP, 1), lambda s: (0, s, 0))],
        out_shape=[jax.ShapeDtypeStruct((N_SETS, N_EXP, CAP), jnp.int32),
                   jax.ShapeDtypeStruct((N_EXP, N_SETS * CAP, 1), F32)],
        scratch_shapes=[pltpu.VMEM((N_EXP, N_CTX), F32)],
        compiler_params=_params(("arbitrary",)),
        name="select",
    )(logits_t)


MOE_ROWS = N_SETS * CAP
MOE_TF = 256
MOE_NF = D_EXP // MOE_TF
MOE_TN = 512
N_FILL = N_TOK // MOE_ROWS
assert MOE_NF == 4


def _moe_kernel(rows_ref, h_hbm, gate_ref, wg_ref, wu_ref, wd_ref, y_hbm,
                xf_ref, xb_ref, acc_ref, rmw_ref, sem_x, sem_put, sem_get, sem_fill):
    e = pl.program_id(0)
    f = pl.program_id(1)
    e_next = jnp.minimum(e + 1, N_EXP - 1)

    def tok_row(hbm_ref, ex, k):
        return hbm_ref.at[pl.ds(rows_ref[ex * MOE_ROWS + k], 1), :]

    def x_copy(ex, k):
        return pltpu.make_async_copy(tok_row(h_hbm, ex, k), xf_ref.at[pl.ds(k, 1), :], sem_x)

    def put_copy(ex, k):
        return pltpu.make_async_copy(rmw_ref.at[pl.ds(k, 1), :], tok_row(y_hbm, ex, k), sem_put)

    def get_copy(ex, k):
        return pltpu.make_async_copy(tok_row(y_hbm, ex, k), rmw_ref.at[pl.ds(k, 1), :], sem_get)

    def fill_copy(c):
        return pltpu.make_async_copy(rmw_ref, y_hbm.at[pl.ds(c * MOE_ROWS, MOE_ROWS), :], sem_fill)

    def start_rows(copy, ex):
        for k in range(MOE_ROWS):
            copy(ex, k).start(priority=k % 2)

    def wait_rows(sem, src, dst):
        pltpu.make_async_copy(src, dst, sem).wait()

    y_rows = y_hbm.at[pl.ds(0, MOE_ROWS), :]
    h_rows = h_hbm.at[pl.ds(0, MOE_ROWS), :]

    def ffn_step(first=False):
        x = xb_ref[...]
        g = jnp.dot(x, wg_ref[...].astype(BF16), preferred_element_type=F32)
        up = jnp.dot(x, wu_ref[...].astype(BF16), preferred_element_type=F32)
        hid = (_silu(g) * up).astype(BF16)
        wd = wd_ref[...].astype(BF16)
        for c in range(D // MOE_TN):
            cols = slice(c * MOE_TN, (c + 1) * MOE_TN)
            part = jnp.dot(hid, wd[:, cols], preferred_element_type=F32)
            if first:
                acc_ref[:, cols] = part
            else:
                acc_ref[:, cols] += part

    @pl.when(jnp.logical_and(e == 0, f == 0))
    def _():
        rmw_ref[...] = jnp.zeros(rmw_ref.shape, F32)
        for c in range(N_FILL):
            fill_copy(c).start()

        def issue(k, c):
            x_copy(0, k).start()
            return c
        lax.fori_loop(0, MOE_ROWS, issue, 0)

    @pl.when(f == 0)
    def _():
        wait_rows(sem_x, h_rows, xf_ref)
        xb_ref[...] = xf_ref[...].astype(BF16)

    @pl.when(jnp.logical_and(e > 0, f == 0))
    def _():
        start_rows(put_copy, e - 1)
        ffn_step(first=True)

    @pl.when(jnp.logical_and(e == 0, f == 0))
    def _():
        ffn_step(first=True)

    @pl.when(f == 1)
    def _():
        start_rows(x_copy, e_next)
        ffn_step()

    @pl.when(jnp.logical_and(e == 0, f == 2))
    def _():
        for c in range(N_FILL):
            fill_copy(c).wait()

    @pl.when(jnp.logical_and(e > 0, f == 2))
    def _():
        wait_rows(sem_put, rmw_ref, y_rows)

    @pl.when(f == 2)
    def _():
        start_rows(get_copy, e)
        ffn_step()

    @pl.when(f == 3)
    def _():
        ffn_step()
        wait_rows(sem_get, y_rows, rmw_ref)
        rmw_ref[...] = rmw_ref[...] + gate_ref[...] * acc_ref[...]

    @pl.when(jnp.logical_and(e == N_EXP - 1, f == MOE_NF - 1))
    def _():
        def issue(k, c):
            put_copy(N_EXP - 1, k).start()
            return c
        lax.fori_loop(0, MOE_ROWS, issue, 0)
        wait_rows(sem_put, rmw_ref, y_rows)
        wait_rows(sem_x, h_rows, xf_ref)


def _moe(rows, h2, gates, w_g, w_u, w_d):
    return pl.pallas_call(
        _moe_kernel,
        grid_spec=pltpu.PrefetchScalarGridSpec(
            num_scalar_prefetch=1,
            grid=(N_EXP, MOE_NF),
            in_specs=[pl.BlockSpec(memory_space=pl.ANY),
                      pl.BlockSpec((None, MOE_ROWS, 1), lambda e, f, s: (e, 0, 0)),
                      pl.BlockSpec((None, D, MOE_TF), lambda e, f, s: (e, 0, f)),
                      pl.BlockSpec((None, D, MOE_TF), lambda e, f, s: (e, 0, f)),
                      pl.BlockSpec((None, MOE_TF, D), lambda e, f, s: (e, f, 0))],
            out_specs=pl.BlockSpec(memory_space=pl.ANY),
            scratch_shapes=[pltpu.VMEM((MOE_ROWS, D), F32),
                            pltpu.VMEM((MOE_ROWS, D), BF16),
                            pltpu.VMEM((MOE_ROWS, D), F32),
                            pltpu.VMEM((MOE_ROWS, D), F32),
                            pltpu.SemaphoreType.DMA(()),
                            pltpu.SemaphoreType.DMA(()),
                            pltpu.SemaphoreType.DMA(()),
                            pltpu.SemaphoreType.DMA(())]),
        out_shape=jax.ShapeDtypeStruct((N_TOK, D), F32),
        compiler_params=_params(("arbitrary", "arbitrary")),
        name="moe",
    )(rows, h2, gates, w_g, w_u, w_d)


def _final_kernel(x1_ref, y_ref, mod_ref, g_ref, o_ref):
    o_ref[...] = _rms(x1_ref[...] + mod_ref[0][5:6, :] * y_ref[...], g_ref[...])


def _final(x1, y, tok0, n_tok, mod, fng):
    tm = 512
    blk0 = tok0 // tm
    return pl.pallas_call(
        _final_kernel,
        grid=(n_tok // tm,),
        in_specs=[pl.BlockSpec((tm, D), lambda i: (blk0 + i, 0)),
                  pl.BlockSpec((tm, D), lambda i: (blk0 + i, 0)),
                  pl.BlockSpec((1, N_MOD, D), lambda i: (_mod_row(blk0 + i, tm), 0, 0)),
                  pl.BlockSpec((1, D), lambda i: (0, 0))],
        out_specs=pl.BlockSpec((tm, D), lambda i: (i, 0)),
        out_shape=jax.ShapeDtypeStruct((n_tok, D), F32),
        compiler_params=_params(("arbitrary",)),
        name="final",
    )(x1, y, mod, fng)


def kernel(x_prompt, x_sample, c, cache_k, cache_v, c_ctx, w_ada, b_ada, norm1_g, w_in, conv_dw, conv_dw_b, conv_ln_g, conv_ln_b, w_conv_out, attn_sink, w_attn_out, w_out, norm2_g, w_router, b_router, w_exp_gate, w_exp_up, w_exp_down, final_norm_g):
    x_ctx = x_prompt.reshape(N_CTX, D)
    x_lat = x_sample.reshape(N_LAT, D)
    cond = jnp.concatenate([c, c_ctx[None, :], jnp.zeros((MOD_ROWS - N_LAT_SEQ - 1, D), F32)], axis=0)
    mod = _ada(cond, w_ada[0], b_ada[0]).reshape(MOD_ROWS, N_MOD, D)

    u = _inproj(_prenorm(x_ctx, x_lat, mod, norm1_g), w_in[0])
    zc = _conv(u, conv_dw[0], conv_dw_b, conv_ln_g, conv_ln_b)

    sink = attn_sink[0]
    cos, sin = _rope_tables()
    ck = cache_k[:, 0].reshape(N_LAT_SEQ, PAST, N_KV * HD)
    cv = cache_v[:, 0].reshape(N_LAT_SEQ, PAST, N_KV * HD)
    att_ctx, state_k, state_v = _ctx_attn(u, sink)
    mixed = _mix(zc, att_ctx, _lat_attn(u, sink, ck, cv, cos, sin), u,
                 w_conv_out[0], w_attn_out[0])

    x1, h2, logits = _outproj(mixed, x_ctx, x_lat, mod, norm2_g, w_out[0].astype(BF16),
                              w_router[0], b_router)

    idx, gates = _select(logits.T)
    rows = (idx + (jnp.arange(N_SETS, dtype=jnp.int32) * N_CTX)[:, None, None])
    rows = rows.transpose(1, 0, 2).reshape(N_EXP * MOE_ROWS)
    ffn = _moe(rows, h2, gates, w_exp_gate[0], w_exp_up[0], w_exp_down[0])

    fng = final_norm_g.reshape(1, D)
    y_prompt = _final(x1, ffn, 0, N_CTX, mod, fng).reshape(N_CTX_SEQ, CTX_LEN, D)
    y_sample = _final(x1, ffn, N_CTX, N_LAT, mod, fng).reshape(N_LAT_SEQ, LAT_LEN, D)
    return (y_prompt, y_sample, state_k, state_v)
```

```python
import functools

import jax
import jax.numpy as jnp
from jax import lax
from jax.experimental import pallas as pl
from jax.experimental.pallas import tpu as pltpu

D = 2048
N_CTX_SEQ, CTX_LEN = 16, 256
N_LAT_SEQ, LAT_LEN = 4, 1024
N_CTX = N_CTX_SEQ * CTX_LEN
N_LAT = N_LAT_SEQ * LAT_LEN
N_TOK = N_CTX + N_LAT
PAST = 512
GRID_W = 64
N_HEADS, N_KV, HD = 16, 4, 128
Q_GROUP = N_HEADS // N_KV
WINDOW = 128
ABLK = 128
CONV_W = 1024
CONV_K = 31
CONV_PAD = CONV_K // 2
N_EXP = 16
D_EXP = 1024
CAP = 2 * N_CTX // N_EXP
N_SETS = 2
ROPE_BASE = 10000.0
EPS = 1e-6
N_MOD = 6
NEG = -1e30
IN_COLS = 2 * CONV_W + N_HEADS * HD + 2 * N_KV * HD + 2 * D
COL_Q = 2 * CONV_W
COL_K = COL_Q + N_HEADS * HD
COL_V = COL_K + N_KV * HD
COL_GA = COL_V + N_KV * HD
COL_GB = COL_GA + D
MOD_ROWS = 8
CTX_MOD_ROW = N_LAT_SEQ

VMEM_LIMIT = 58 * 1024 * 1024
F32 = jnp.float32
BF16 = jnp.bfloat16


def _params(sem, vmem=VMEM_LIMIT):
    return pltpu.CompilerParams(dimension_semantics=sem, vmem_limit_bytes=vmem)


def _mod_row(tok_block, block_tokens):
    ctx_blocks = N_CTX // block_tokens
    per_seq = LAT_LEN // block_tokens
    return jnp.where(tok_block < ctx_blocks, CTX_MOD_ROW, (tok_block - ctx_blocks) // per_seq)


def _silu(x):
    return x * jax.nn.sigmoid(x)


def _rms(x, g):
    return x * lax.rsqrt(jnp.mean(x * x, axis=-1, keepdims=True) + EPS) * g


def _ada_kernel(cond_ref, w_ref, b_ref, o_ref):
    a = _silu(cond_ref[...]).astype(BF16)
    o_ref[...] = jnp.dot(a, w_ref[...].astype(BF16), preferred_element_type=F32) + b_ref[...]


def _ada(cond, w_ada, b_ada):
    tn = 1024
    n = N_MOD * D
    return pl.pallas_call(
        _ada_kernel,
        grid=(n // tn,),
        in_specs=[pl.BlockSpec((MOD_ROWS, D), lambda j: (0, 0)),
                  pl.BlockSpec((D, tn), lambda j: (0, j)),
                  pl.BlockSpec((1, tn), lambda j: (0, j))],
        out_specs=pl.BlockSpec((MOD_ROWS, tn), lambda j: (0, j)),
        out_shape=jax.ShapeDtypeStruct((MOD_ROWS, n), F32),
        compiler_params=_params(("arbitrary",)),
        name="ada",
    )(cond, w_ada, b_ada.reshape(1, n))


def _set_specs(tm, buffers):
    nc = N_CTX // tm
    mode = pl.Buffered(buffers)
    ctx = pl.BlockSpec((tm, D), lambda i, *_: (jnp.minimum(i, nc - 1), 0), pipeline_mode=mode)
    lat = pl.BlockSpec((tm, D), lambda i, *_: (jnp.maximum(i - nc, 0), 0), pipeline_mode=mode)
    return ctx, lat


def _prenorm_kernel(xc_ref, xl_ref, mod_ref, g_ref, h_ref):
    is_ctx = pl.program_id(0) < N_CTX // xc_ref.shape[0]
    x = jnp.where(is_ctx, xc_ref[...], xl_ref[...])
    m = mod_ref[0]
    h = _rms(x, g_ref[...]) * (1.0 + m[1:2, :]) + m[0:1, :]
    h_ref[...] = h.astype(BF16)


def _prenorm(x_ctx, x_lat, mod, n1g):
    tm = 512
    xc_spec, xl_spec = _set_specs(tm, 2)
    return pl.pallas_call(
        _prenorm_kernel,
        grid=(N_TOK // tm,),
        in_specs=[xc_spec, xl_spec,
                  pl.BlockSpec((1, N_MOD, D), lambda i: (_mod_row(i, tm), 0, 0)),
                  pl.BlockSpec((1, D), lambda i: (0, 0))],
        out_specs=pl.BlockSpec((tm, D), lambda i: (i, 0)),
        out_shape=jax.ShapeDtypeStruct((N_TOK, D), BF16),
        compiler_params=_params(("arbitrary",)),
        name="prenorm",
    )(x_ctx, x_lat, mod, n1g)


def _inproj_kernel(h_ref, w_ref, o_ref, wb_ref):
    @pl.when(pl.program_id(1) == 0)
    def _():
        wb_ref[...] = w_ref[...].astype(BF16)

    o_ref[...] = jnp.dot(h_ref[...], wb_ref[...], preferred_element_type=F32)


def _inproj(h, w_in):
    tm, tn = 1024, 1536
    return pl.pallas_call(
        _inproj_kernel,
        grid=(IN_COLS // tn, N_TOK // tm),
        in_specs=[pl.BlockSpec((tm, D), lambda j, i: (i, 0)),
                  pl.BlockSpec((D, tn), lambda j, i: (0, j))],
        out_specs=pl.BlockSpec((tm, tn), lambda j, i: (i, j)),
        out_shape=jax.ShapeDtypeStruct((N_TOK, IN_COLS), F32),
        scratch_shapes=[pltpu.VMEM((D, tn), BF16)],
        compiler_params=_params(("arbitrary", "arbitrary")),
        name="inproj",
    )(h, w_in)


CONV_T = 256
CONV_HALO = 16
CONV_ROWS = 32
SUBLANES = 8
LANES = 128
CONV_ZS = CONV_T + (CONV_K // SUBLANES) * SUBLANES


def _conv_kernel(main_ref, prev_ref, next_ref, w_ref, b_ref, lg_ref, lb_ref, o_ref,
                 z_ref, zs_ref, y_ref):
    i = pl.program_id(0)
    ctx_blocks = N_CTX // CONV_T
    per_seq = LAT_LEN // CONV_T
    pos = (i - ctx_blocks) % per_seq
    is_lat = i >= ctx_blocks
    has_prev = jnp.logical_and(is_lat, pos > 0)
    has_next = jnp.logical_and(is_lat, pos < per_seq - 1)

    def glu(ref):
        v = ref[...]
        return v[:, :CONV_W] * jax.nn.sigmoid(v[:, CONV_W:])

    z_ref[0:CONV_HALO, :] = jnp.where(has_prev, glu(prev_ref), 0.0)
    z_ref[CONV_HALO:CONV_HALO + CONV_T, :] = glu(main_ref)
    z_ref[CONV_HALO + CONV_T:, :] = jnp.where(has_next, glu(next_ref), 0.0)

    z = z_ref[...]
    zs_ref[0] = z[:CONV_ZS]
    for s in range(1, SUBLANES):
        zs_ref[s] = pltpu.roll(z, z.shape[0] - s, axis=0)[:CONV_ZS]

    def channel_group(c, carry):
        cols = pl.ds(pl.multiple_of(c * LANES, LANES), LANES)
        acc = jnp.zeros((CONV_T, LANES), F32)
        for j in range(CONV_K):
            a, s = divmod(CONV_HALO - CONV_PAD + j, SUBLANES)
            acc = acc + zs_ref[s, a * SUBLANES:a * SUBLANES + CONV_T, cols] * w_ref[j:j + 1, cols]
        y_ref[:, cols] = acc + b_ref[:, cols]
        return carry

    lax.fori_loop(0, CONV_W // LANES, channel_group, 0)

    for r in range(CONV_T // CONV_ROWS):
        acc = y_ref[r * CONV_ROWS:(r + 1) * CONV_ROWS, :]
        mu = jnp.mean(acc, axis=-1, keepdims=True)
        cen = acc - mu
        var = jnp.mean(cen * cen, axis=-1, keepdims=True)
        y = cen * lax.rsqrt(var + EPS) * lg_ref[...] + lb_ref[...]
        o_ref[r * CONV_ROWS:(r + 1) * CONV_ROWS, :] = _silu(y).astype(BF16)


def _conv(u, w_dw, b_dw, ln_g, ln_b):
    nblk = N_TOK // CONV_T
    hpb = CONV_T // CONV_HALO
    last = N_TOK // CONV_HALO - 1
    row = lambda i: (0, 0)
    return pl.pallas_call(
        _conv_kernel,
        grid=(nblk,),
        in_specs=[pl.BlockSpec((CONV_T, 2 * CONV_W), lambda i: (i, 0)),
                  pl.BlockSpec((CONV_HALO, 2 * CONV_W), lambda i: (jnp.maximum(i * hpb - 1, 0), 0)),
                  pl.BlockSpec((CONV_HALO, 2 * CONV_W), lambda i: (jnp.minimum((i + 1) * hpb, last), 0)),
                  pl.BlockSpec((CONV_K, CONV_W), row),
                  pl.BlockSpec((1, CONV_W), row),
                  pl.BlockSpec((1, CONV_W), row),
                  pl.BlockSpec((1, CONV_W), row)],
        out_specs=pl.BlockSpec((CONV_T, CONV_W), lambda i: (i, 0)),
        out_shape=jax.ShapeDtypeStruct((N_TOK, CONV_W), BF16),
        scratch_shapes=[pltpu.VMEM((CONV_T + 2 * CONV_HALO, CONV_W), F32),
                        pltpu.VMEM((SUBLANES, CONV_ZS, CONV_W), F32),
                        pltpu.VMEM((CONV_T, CONV_W), F32)],
        compiler_params=_params(("arbitrary",)),
        name="conv",
    )(u, u, u, w_dw, b_dw, ln_g, ln_b)


def _stack_heads(q):
    return jnp.concatenate([q[:, g * HD:(g + 1) * HD] for g in range(Q_GROUP)], axis=0)


def _unstack_heads(o, rows):
    return jnp.concatenate([o[g * rows:(g + 1) * rows, :] for g in range(Q_GROUP)], axis=1)


LOG2E = 1.4426950408889634
Q_SCALE = HD ** -0.5 * LOG2E
QW = Q_GROUP * HD


def _sink_column(sink_ref, kv, rows):
    rid = lax.broadcasted_iota(jnp.int32, (Q_GROUP * rows, 1), 0)
    col = jnp.full((Q_GROUP * rows, 1), sink_ref[kv * Q_GROUP], F32)
    for g in range(1, Q_GROUP):
        col = jnp.where(rid >= g * rows, sink_ref[kv * Q_GROUP + g], col)
    return col * LOG2E


def _with_ones(vals):
    return jnp.concatenate([vals, jnp.ones(vals.shape, vals.dtype)], axis=1)


def _softmax_av(s, sk, vals):
    m = jnp.maximum(jnp.max(s, axis=-1, keepdims=True), sk)
    if vals.shape[1] == 2 * HD:
        p = jnp.exp2(s - m).astype(BF16)
        o = jnp.dot(p, vals, preferred_element_type=F32)
        return o[:, :HD] / (o[:, HD:HD + 1] + jnp.exp2(sk - m))
    p = jnp.exp2(s - m)
    den = jnp.sum(p, axis=-1, keepdims=True) + jnp.exp2(sk - m)
    return jnp.dot(p.astype(BF16), vals, preferred_element_type=F32) / den


_NT = (((1,), (1,)), ((), ()))


def _ctx_attn_kernel(sink_ref, q_ref, k_ref, v_ref, o_ref, sk_ref, sv_ref):
    for kv in range(N_KV):
        k = k_ref[:, kv * HD:(kv + 1) * HD]
        v = v_ref[:, kv * HD:(kv + 1) * HD]
        sk_ref[0, 0, :, kv, :] = k
        sv_ref[0, 0, :, kv, :] = v
        q = (_stack_heads(q_ref[:, kv * QW:(kv + 1) * QW]) * Q_SCALE).astype(BF16)
        s = lax.dot_general(q, k.astype(BF16), _NT, preferred_element_type=F32)
        o = _softmax_av(s, _sink_column(sink_ref, kv, CTX_LEN), v.astype(BF16))
        o_ref[:, kv * QW:(kv + 1) * QW] = _unstack_heads(o, CTX_LEN).astype(BF16)


def _ctx_attn(u, sink):
    qcols, kcols = N_HEADS * HD, N_KV * HD
    state_spec = pl.BlockSpec((1, 1, CTX_LEN, N_KV, HD), lambda b, s: (b, 0, 0, 0, 0))
    state_shape = jax.ShapeDtypeStruct((N_CTX_SEQ, 1, CTX_LEN, N_KV, HD), F32)
    return pl.pallas_call(
        _ctx_attn_kernel,
        grid_spec=pltpu.PrefetchScalarGridSpec(
            num_scalar_prefetch=1,
            grid=(N_CTX_SEQ,),
            in_specs=[pl.BlockSpec((CTX_LEN, qcols), lambda b, s: (b, COL_Q // qcols)),
                      pl.BlockSpec((CTX_LEN, kcols), lambda b, s: (b, COL_K // kcols)),
                      pl.BlockSpec((CTX_LEN, kcols), lambda b, s: (b, COL_V // kcols))],
            out_specs=[pl.BlockSpec((CTX_LEN, qcols), lambda b, s: (b, 0)),
                       state_spec, state_spec]),
        out_shape=[jax.ShapeDtypeStruct((N_CTX, N_HEADS * HD), BF16), state_shape, state_shape],
        compiler_params=_params(("arbitrary",)),
        name="ctx_attn",
    )(sink, u, u, u)


def _rope(x, cos, sin):
    lane = lax.broadcasted_iota(jnp.int32, x.shape, 1)
    fwd = pltpu.roll(x, HD - 32, axis=1)
    bwd = pltpu.roll(x, 32, axis=1)
    partner = jnp.where((lane & 32) == 0, fwd, bwd)
    return x * cos + partner * sin


def _lat_attn_kernel(sink_ref, q_ref, k_ref, v_ref, ck_ref, cv_ref, cos_ref, sin_ref, o_ref):
    kv = pl.program_id(1)
    nblk = LAT_LEN // ABLK
    cos, sin = cos_ref[...], sin_ref[...]
    k_rot = _rope(k_ref[...], cos, sin).astype(BF16)
    v = _with_ones(v_ref[...].astype(BF16))
    ck = ck_ref[...].astype(BF16)
    cv = _with_ones(cv_ref[...].astype(BF16))
    sk = _sink_column(sink_ref, kv, ABLK)
    r = lax.broadcasted_iota(jnp.int32, (Q_GROUP * ABLK, ABLK), 0) & (ABLK - 1)
    c = lax.broadcasted_iota(jnp.int32, (Q_GROUP * ABLK, ABLK), 1)
    open_tile = jnp.zeros((Q_GROUP * ABLK, ABLK), F32)
    below = jnp.where(c >= r, 0.0, NEG)
    above = jnp.where(c <= r, 0.0, NEG)
    ctx_open = jnp.zeros((Q_GROUP * ABLK, PAST), F32)
    bias_first = jnp.concatenate([open_tile, above, ctx_open], axis=1)
    bias_mid = jnp.concatenate([below, open_tile, above, ctx_open], axis=1)
    bias_last = jnp.concatenate([below, open_tile, ctx_open], axis=1)
    for j in range(nblk):
        rows = slice(j * ABLK, (j + 1) * ABLK)
        cos_j, sin_j = cos[rows], sin[rows]
        qs = jnp.concatenate(
            [_rope(q_ref[rows, g * HD:(g + 1) * HD], cos_j, sin_j) * Q_SCALE for g in range(Q_GROUP)],
            axis=0).astype(BF16)
        blocks = range(max(j - 1, 0), min(j + 2, nblk))
        band = slice(blocks[0] * ABLK, (blocks[-1] + 1) * ABLK)
        keys = jnp.concatenate([k_rot[band], ck], axis=0)
        vals = jnp.concatenate([v[band], cv], axis=0)
        bias = bias_first if j == 0 else bias_last if j == nblk - 1 else bias_mid
        s = lax.dot_general(qs, keys, _NT, preferred_element_type=F32) + bias
        o = _softmax_av(s, sk, vals)
        o_ref[rows, :] = _unstack_heads(o, ABLK).astype(BF16)


def _lat_attn(u, sink, ck, cv, cos, sin):
    base = N_CTX // LAT_LEN
    cache_spec = pl.BlockSpec((None, PAST, HD), lambda b, kv, s: (b, 0, kv))
    table_spec = pl.BlockSpec((LAT_LEN, HD), lambda b, kv, s: (0, 0))
    return pl.pallas_call(
        _lat_attn_kernel,
        grid_spec=pltpu.PrefetchScalarGridSpec(
            num_scalar_prefetch=1,
            grid=(N_LAT_SEQ, N_KV),
            in_specs=[pl.BlockSpec((LAT_LEN, QW), lambda b, kv, s: (base + b, COL_Q // QW + kv)),
                      pl.BlockSpec((LAT_LEN, HD), lambda b, kv, s: (base + b, COL_K // HD + kv)),
                      pl.BlockSpec((LAT_LEN, HD), lambda b, kv, s: (base + b, COL_V // HD + kv)),
                      cache_spec, cache_spec, table_spec, table_spec],
            out_specs=pl.BlockSpec((LAT_LEN, QW), lambda b, kv, s: (b, kv))),
        out_shape=jax.ShapeDtypeStruct((N_LAT, N_HEADS * HD), BF16),
        compiler_params=_params(("arbitrary", "arbitrary")),
        name="lat_attn",
    )(sink, u, u, u, ck, cv, cos, sin)


def _rope_tables():
    n = jnp.arange(LAT_LEN)
    row = (n // GRID_W).astype(F32)
    col = (n % GRID_W).astype(F32)
    quarter = HD // 4
    inv = ROPE_BASE ** (-jnp.arange(quarter, dtype=F32) / quarter)
    ang_r = row[:, None] * inv[None, :]
    ang_c = col[:, None] * inv[None, :]
    cos = jnp.concatenate([jnp.cos(ang_r)] * 2 + [jnp.cos(ang_c)] * 2, axis=-1)
    sin = jnp.concatenate([-jnp.sin(ang_r), jnp.sin(ang_r), -jnp.sin(ang_c), jnp.sin(ang_c)], axis=-1)
    return cos, sin


def _mix_kernel(z_ref, ac_ref, al_ref, ga_ref, gb_ref, wc_ref, wa_ref, o_ref, wcb_ref, wab_ref):
    @pl.when(pl.program_id(1) == 0)
    def _():
        wcb_ref[...] = wc_ref[...].astype(BF16)
        wab_ref[...] = wa_ref[...].astype(BF16)

    is_ctx = pl.program_id(1) < N_CTX // z_ref.shape[0]
    att = jnp.where(is_ctx, ac_ref[...], al_ref[...])
    conv_out = jnp.dot(z_ref[...], wcb_ref[...], preferred_element_type=F32)
    attn_out = jnp.dot(att, wab_ref[...], preferred_element_type=F32)
    mixed = jax.nn.sigmoid(ga_ref[...]) * conv_out + jax.nn.sigmoid(gb_ref[...]) * attn_out
    o_ref[...] = mixed.astype(BF16)


def _mix(zc, att_ctx, att_lat, u, w_conv_out, w_attn_out):
    tm, tn = 512, 1024
    nc = N_CTX // tm
    return pl.pallas_call(
        _mix_kernel,
        grid=(D // tn, N_TOK // tm),
        in_specs=[pl.BlockSpec((tm, CONV_W), lambda j, i: (i, 0)),
                  pl.BlockSpec((tm, N_HEADS * HD), lambda j, i: (jnp.minimum(i, nc - 1), 0)),
                  pl.BlockSpec((tm, N_HEADS * HD), lambda j, i: (jnp.maximum(i - nc, 0), 0)),
                  pl.BlockSpec((tm, tn), lambda j, i: (i, COL_GA // tn + j)),
                  pl.BlockSpec((tm, tn), lambda j, i: (i, COL_GB // tn + j)),
                  pl.BlockSpec((CONV_W, tn), lambda j, i: (0, j)),
                  pl.BlockSpec((N_HEADS * HD, tn), lambda j, i: (0, j))],
        out_specs=pl.BlockSpec((tm, tn), lambda j, i: (i, j)),
        out_shape=jax.ShapeDtypeStruct((N_TOK, D), BF16),
        scratch_shapes=[pltpu.VMEM((CONV_W, tn), BF16), pltpu.VMEM((N_HEADS * HD, tn), BF16)],
        compiler_params=_params(("arbitrary", "arbitrary")),
        name="mix",
    )(zc, att_ctx, att_lat, u, u, w_conv_out, w_attn_out)


def _split_bf16(x):
    hi = x.astype(BF16)
    lo = (x - hi.astype(F32)).astype(BF16)
    return hi, lo


def _outproj_kernel(mx_ref, xc_ref, xl_ref, mod_ref, g_ref, w_ref, wr_ref, br_ref,
                    x1_ref, h2_ref, lg_ref):
    is_ctx = pl.program_id(0) < N_CTX // xc_ref.shape[0]
    x = jnp.where(is_ctx, xc_ref[...], xl_ref[...])
    m = mod_ref[0]
    x1 = x + m[2:3, :] * jnp.dot(mx_ref[...], w_ref[...], preferred_element_type=F32)
    x1_ref[...] = x1
    h2 = _rms(x1, g_ref[...]) * (1.0 + m[4:5, :]) + m[3:4, :]
    h2_ref[...] = h2
    h_hi, h_lo = _split_bf16(h2)
    w_hi, w_lo = _split_bf16(wr_ref[...])
    lg = (jnp.dot(h_hi, w_hi, preferred_element_type=F32)
          + jnp.dot(h_hi, w_lo, preferred_element_type=F32)
          + jnp.dot(h_lo, w_hi, preferred_element_type=F32))
    lg_ref[...] = lg + br_ref[...]


def _outproj(mixed, x_ctx, x_lat, mod, n2g, w_out_bf, w_router, b_router):
    tm = 512
    const = lambda i: (0, 0)
    xc_spec, xl_spec = _set_specs(tm, 2)
    return pl.pallas_call(
        _outproj_kernel,
        grid=(N_TOK // tm,),
        in_specs=[pl.BlockSpec((tm, D), lambda i: (i, 0)),
                  xc_spec, xl_spec,
                  pl.BlockSpec((1, N_MOD, D), lambda i: (_mod_row(i, tm), 0, 0)),
                  pl.BlockSpec((1, D), const),
                  pl.BlockSpec((D, D), const, pipeline_mode=pl.Buffered(1)),
                  pl.BlockSpec((D, N_EXP), const),
                  pl.BlockSpec((1, N_EXP), const)],
        out_specs=[pl.BlockSpec((tm, D), lambda i: (i, 0)),
                   pl.BlockSpec((tm, D), lambda i: (i, 0)),
                   pl.BlockSpec((tm, N_EXP), lambda i: (i, 0))],
        out_shape=[jax.ShapeDtypeStruct((N_TOK, D), F32),
                   jax.ShapeDtypeStruct((N_TOK, D), F32),
                   jax.ShapeDtypeStruct((N_TOK, N_EXP), F32)],
        compiler_params=_params(("arbitrary",)),
        name="outproj",
    )(mixed, x_ctx, x_lat, mod, n2g, w_out_bf, w_router, b_router)


SEL_CHUNK = 128


def _prefix_sum(flags, out_ref):
    n = flags.shape[1]
    ri = lax.broadcasted_iota(jnp.int32, (SEL_CHUNK, SEL_CHUNK), 0)
    ci = lax.broadcasted_iota(jnp.int32, (SEL_CHUNK, SEL_CHUNK), 1)
    tri = jnp.where(ri <= ci, 1.0, 0.0).astype(BF16)
    run = jnp.zeros((N_EXP, 1), F32)
    for ch in range(n // SEL_CHUNK):
        blk = flags[:, ch * SEL_CHUNK:(ch + 1) * SEL_CHUNK].astype(BF16)
        within = jnp.dot(blk, tri, preferred_element_type=F32) + run
        out_ref[:, ch * SEL_CHUNK:(ch + 1) * SEL_CHUNK] = within
        run = within[:, SEL_CHUNK - 1:SEL_CHUNK]


def _select_kernel(lg_ref, idx_ref, gate_ref, cs_ref):
    n = lg_ref.shape[1]
    lg = lg_ref[...]
    ex = jnp.exp(lg - jnp.max(lg, axis=0, keepdims=True))
    aff = ex / jnp.sum(ex, axis=0, keepdims=True)

    def bit_step(b, cur):
        cand = cur | (1 << (30 - b))
        cnt = jnp.sum(jnp.where(aff >= pltpu.bitcast(cand, F32), 1.0, 0.0), axis=1, keepdims=True)
        return jnp.where(cnt >= CAP, cand, cur)

    thr = pltpu.bitcast(lax.fori_loop(0, 31, bit_step, jnp.zeros((N_EXP, 1), jnp.int32)), F32)
    above = aff > thr
    tied = aff == thr
    room = CAP - jnp.sum(jnp.where(above, 1.0, 0.0), axis=1, keepdims=True)
    _prefix_sum(jnp.where(tied, 1.0, 0.0), cs_ref)
    sel = jnp.logical_or(above, jnp.logical_and(tied, cs_ref[...] <= room))
    gsel = jnp.where(sel, aff, 0.0)
    _prefix_sum(jnp.where(sel, 1.0, 0.0), cs_ref)

    slot = lax.broadcasted_iota(jnp.int32, (CAP, n), 0).astype(F32)
    g_hi = gsel.astype(BF16).astype(F32)
    rest = gsel - g_hi
    g_mid = rest.astype(BF16).astype(F32)
    g_lo = rest - g_mid
    ones = jnp.ones((1, n), F32)
    pad = jnp.zeros((4, n), F32)
    last_slot = lax.broadcasted_iota(jnp.int32, (1, CAP), 1) == CAP - 1
    on_diag = (lax.broadcasted_iota(jnp.int32, (CAP, CAP), 0)
               == lax.broadcasted_iota(jnp.int32, (CAP, CAP), 1))
    total = jnp.sum(gsel, axis=1, keepdims=True)
    for e in range(N_EXP):
        row = slice(e, e + 1)
        le = jnp.where(cs_ref[row, :] <= slot, 1.0, 0.0).astype(BF16)
        lhs = jnp.concatenate([ones, g_hi[row], g_mid[row], g_lo[row], pad], axis=0).astype(BF16)
        res = lax.dot_general(lhs, le, (((1,), (1,)), ((), ())), preferred_element_type=F32)
        idx_ref[0, row, :] = res[0:1, :].astype(jnp.int32)
        cum = res[1:2, :] + res[2:3, :] + res[3:4, :]
        gate = jnp.where(last_slot, total[row], pltpu.roll(cum, CAP - 1, axis=1)) - cum
        gate_ref[e] = jnp.sum(jnp.where(on_diag, gate, 0.0), axis=1, keepdims=True)


def _select(logits_t):
    return pl.pallas_call(
        _select_kernel,
        grid=(N_SETS,),
        in_specs=[pl.BlockSpec((N_EXP, N_CTX), lambda s: (0, s))],
        out_specs=[pl.BlockSpec((1, N_EXP, CAP), lambda s: (s, 0, 0)),
                   pl.BlockSpec((N_EXP, CAP, 1), lambda s: (0, s, 0))],
        out_shape=[jax.ShapeDtypeStruct((N_SETS, N_EXP, CAP), jnp.int32),
                   jax.ShapeDtypeStruct((N_EXP, N_SETS * CAP, 1), F32)],
        scratch_shapes=[pltpu.VMEM((N_EXP, N_CTX), F32)],
        compiler_params=_params(("arbitrary",)),
        name="select",
    )(logits_t)


MOE_ROWS = N_SETS * CAP
MOE_TF = 256
MOE_NF = D_EXP // MOE_TF
MOE_TN = 512
N_FILL = N_TOK // MOE_ROWS
assert MOE_NF == 4


def _moe_kernel(rows_ref, h_hbm, gate_ref, wg_ref, wu_ref, wd_ref, y_hbm,
                xf_ref, xb_ref, acc_ref, rmw_ref, sem_x, sem_put, sem_get, sem_fill):
    e = pl.program_id(0)
    f = pl.program_id(1)
    e_next = jnp.minimum(e + 1, N_EXP - 1)

    def tok_row(hbm_ref, ex, k):
        return hbm_ref.at[pl.ds(rows_ref[ex * MOE_ROWS + k], 1), :]

    def x_copy(ex, k):
        return pltpu.make_async_copy(tok_row(h_hbm, ex, k), xf_ref.at[pl.ds(k, 1), :], sem_x)

    def put_copy(ex, k):
        return pltpu.make_async_copy(rmw_ref.at[pl.ds(k, 1), :], tok_row(y_hbm, ex, k), sem_put)

    def get_copy(ex, k):
        return pltpu.make_async_copy(tok_row(y_hbm, ex, k), rmw_ref.at[pl.ds(k, 1), :], sem_get)

    def fill_copy(c):
        return pltpu.make_async_copy(rmw_ref, y_hbm.at[pl.ds(c * MOE_ROWS, MOE_ROWS), :], sem_fill)

    def start_rows(copy, ex):
        for k in range(MOE_ROWS):
            copy(ex, k).start(priority=k % 2)

    def wait_rows(sem, src, dst):
        pltpu.make_async_copy(src, dst, sem).wait()

    y_rows = y_hbm.at[pl.ds(0, MOE_ROWS), :]
    h_rows = h_hbm.at[pl.ds(0, MOE_ROWS), :]

    def ffn_step(first=False):
        x = xb_ref[...]
        g = jnp.dot(x, wg_ref[...].astype(BF16), preferred_element_type=F32)
        up = jnp.dot(x, wu_ref[...].astype(BF16), preferred_element_type=F32)
        hid = (_silu(g) * up).astype(BF16)
        wd = wd_ref[...].astype(BF16)
        for c in range(D // MOE_TN):
            cols = slice(c * MOE_TN, (c + 1) * MOE_TN)
            part = jnp.dot(hid, wd[:, cols], preferred_element_type=F32)
            if first:
                acc_ref[:, cols] = part
            else:
                acc_ref[:, cols] += part

    @pl.when(jnp.logical_and(e == 0, f == 0))
    def _():
        rmw_ref[...] = jnp.zeros(rmw_ref.shape, F32)
        for c in range(N_FILL):
            fill_copy(c).start()

        def issue(k, c):
            x_copy(0, k).start()
            return c
        lax.fori_loop(0, MOE_ROWS, issue, 0)

    @pl.when(f == 0)
    def _():
        wait_rows(sem_x, h_rows, xf_ref)
        xb_ref[...] = xf_ref[...].astype(BF16)

    @pl.when(jnp.logical_and(e > 0, f == 0))
    def _():
        start_rows(put_copy, e - 1)
        ffn_step(first=True)

    @pl.when(jnp.logical_and(e == 0, f == 0))
    def _():
        ffn_step(first=True)

    @pl.when(f == 1)
    def _():
        start_rows(x_copy, e_next)
        ffn_step()

    @pl.when(jnp.logical_and(e == 0, f == 2))
    def _():
        for c in range(N_FILL):
            fill_copy(c).wait()

    @pl.when(jnp.logical_and(e > 0, f == 2))
    def _():
        wait_rows(sem_put, rmw_ref, y_rows)

    @pl.when(f == 2)
    def _():
        start_rows(get_copy, e)
        ffn_step()

    @pl.when(f == 3)
    def _():
        ffn_step()
        wait_rows(sem_get, y_rows, rmw_ref)
        rmw_ref[...] = rmw_ref[...] + gate_ref[...] * acc_ref[...]

    @pl.when(jnp.logical_and(e == N_EXP - 1, f == MOE_NF - 1))
    def _():
        def issue(k, c):
            put_copy(N_EXP - 1, k).start()
            return c
        lax.fori_loop(0, MOE_ROWS, issue, 0)
        wait_rows(sem_put, rmw_ref, y_rows)
        wait_rows(sem_x, h_rows, xf_ref)


def _moe(rows, h2, gates, w_g, w_u, w_d):
    return pl.pallas_call(
        _moe_kernel,
        grid_spec=pltpu.PrefetchScalarGridSpec(
            num_scalar_prefetch=1,
            grid=(N_EXP, MOE_NF),
            in_specs=[pl.BlockSpec(memory_space=pl.ANY),
                      pl.BlockSpec((None, MOE_ROWS, 1), lambda e, f, s: (e, 0, 0)),
                      pl.BlockSpec((None, D, MOE_TF), lambda e, f, s: (e, 0, f)),
                      pl.BlockSpec((None, D, MOE_TF), lambda e, f, s: (e, 0, f)),
                      pl.BlockSpec((None, MOE_TF, D), lambda e, f, s: (e, f, 0))],
            out_specs=pl.BlockSpec(memory_space=pl.ANY),
            scratch_shapes=[pltpu.VMEM((MOE_ROWS, D), F32),
                            pltpu.VMEM((MOE_ROWS, D), BF16),
                            pltpu.VMEM((MOE_ROWS, D), F32),
                            pltpu.VMEM((MOE_ROWS, D), F32),
                            pltpu.SemaphoreType.DMA(()),
                            pltpu.SemaphoreType.DMA(()),
                            pltpu.SemaphoreType.DMA(()),
                            pltpu.SemaphoreType.DMA(())]),
        out_shape=jax.ShapeDtypeStruct((N_TOK, D), F32),
        compiler_params=_params(("arbitrary", "arbitrary")),
        name="moe",
    )(rows, h2, gates, w_g, w_u, w_d)


def _final_kernel(x1_ref, y_ref, mod_ref, g_ref, o_ref):
    o_ref[...] = _rms(x1_ref[...] + mod_ref[0][5:6, :] * y_ref[...], g_ref[...])


def _final(x1, y, tok0, n_tok, mod, fng):
    tm = 512
    blk0 = tok0 // tm
    return pl.pallas_call(
        _final_kernel,
        grid=(n_tok // tm,),
        in_specs=[pl.BlockSpec((tm, D), lambda i: (blk0 + i, 0)),
                  pl.BlockSpec((tm, D), lambda i: (blk0 + i, 0)),
                  pl.BlockSpec((1, N_MOD, D), lambda i: (_mod_row(blk0 + i, tm), 0, 0)),
                  pl.BlockSpec((1, D), lambda i: (0, 0))],
        out_specs=pl.BlockSpec((tm, D), lambda i: (i, 0)),
        out_shape=jax.ShapeDtypeStruct((n_tok, D), F32),
        compiler_params=_params(("arbitrary",)),
        name="final",
    )(x1, y, mod, fng)


def kernel(x_prompt, x_sample, c, cache_k, cache_v, c_ctx, w_ada, b_ada, norm1_g, w_in, conv_dw, conv_dw_b, conv_ln_g, conv_ln_b, w_conv_out, attn_sink, w_attn_out, w_out, norm2_g, w_router, b_router, w_exp_gate, w_exp_up, w_exp_down, final_norm_g):
    x_ctx = x_prompt.reshape(N_CTX, D)
    x_lat = x_sample.reshape(N_LAT, D)
    cond = jnp.concatenate([c, c_ctx[None, :], jnp.zeros((MOD_ROWS - N_LAT_SEQ - 1, D), F32)], axis=0)
    mod = _ada(cond, w_ada[0], b_ada[0]).reshape(MOD_ROWS, N_MOD, D)

    u = _inproj(_prenorm(x_ctx, x_lat, mod, norm1_g), w_in[0])
    zc = _conv(u, conv_dw[0], conv_dw_b, conv_ln_g, conv_ln_b)

    sink = attn_sink[0]
    cos, sin = _rope_tables()
    ck = cache_k[:, 0].reshape(N_LAT_SEQ, PAST, N_KV * HD)
    cv = cache_v[:, 0].reshape(N_LAT_SEQ, PAST, N_KV * HD)
    att_ctx, state_k, state_v = _ctx_attn(u, sink)
    mixed = _mix(zc, att_ctx, _lat_attn(u, sink, ck, cv, cos, sin), u,
                 w_conv_out[0], w_attn_out[0])

    x1, h2, logits = _outproj(mixed, x_ctx, x_lat, mod, norm2_g, w_out[0].astype(BF16),
                              w_router[0], b_router)

    idx, gates = _select(logits.T)
    rows = (idx + (jnp.arange(N_SETS, dtype=jnp.int32) * N_CTX)[:, None, None])
    rows = rows.transpose(1, 0, 2).reshape(N_EXP * MOE_ROWS)
    ffn = _moe(rows, h2, gates, w_exp_gate[0], w_exp_up[0], w_exp_down[0])

    fng = final_norm_g.reshape(1, D)
    y_prompt = _final(x1, ffn, 0, N_CTX, mod, fng).reshape(N_CTX_SEQ, CTX_LEN, D)
    y_sample = _final(x1, ffn, N_CTX, N_LAT, mod, fng).reshape(N_LAT_SEQ, LAT_LEN, D)
    return (y_prompt, y_sample, state_k, state_v)
```

```python
import functools

import jax
import jax.numpy as jnp
from jax import lax
from jax.experimental import pallas as pl
from jax.experimental.pallas import tpu as pltpu

D = 2048
N_CTX_SEQ, CTX_LEN = 16, 256
N_LAT_SEQ, LAT_LEN = 4, 1024
N_CTX = N_CTX_SEQ * CTX_LEN
N_LAT = N_LAT_SEQ * LAT_LEN
N_TOK = N_CTX + N_LAT
PAST = 512
GRID_W = 64
N_HEADS, N_KV, HD = 16, 4, 128
Q_GROUP = N_HEADS // N_KV
WINDOW = 128
ABLK = 128
CONV_W = 1024
CONV_K = 31
CONV_PAD = CONV_K // 2
N_EXP = 16
D_EXP = 1024
CAP = 2 * N_CTX // N_EXP
N_SETS = 2
ROPE_BASE = 10000.0
EPS = 1e-6
N_MOD = 6
NEG = -1e30
IN_COLS = 2 * CONV_W + N_HEADS * HD + 2 * N_KV * HD + 2 * D
COL_Q = 2 * CONV_W
COL_K = COL_Q + N_HEADS * HD
COL_V = COL_K + N_KV * HD
COL_GA = COL_V + N_KV * HD
COL_GB = COL_GA + D
MOD_ROWS = 8
CTX_MOD_ROW = N_LAT_SEQ

VMEM_LIMIT = 58 * 1024 * 1024
F32 = jnp.float32
BF16 = jnp.bfloat16


def _params(sem, vmem=VMEM_LIMIT):
    return pltpu.CompilerParams(dimension_semantics=sem, vmem_limit_bytes=vmem)


def _mod_row(tok_block, block_tokens):
    ctx_blocks = N_CTX // block_tokens
    per_seq = LAT_LEN // block_tokens
    return jnp.where(tok_block < ctx_blocks, CTX_MOD_ROW, (tok_block - ctx_blocks) // per_seq)


def _silu(x):
    return x * jax.nn.sigmoid(x)


def _rms(x, g):
    return x * lax.rsqrt(jnp.mean(x * x, axis=-1, keepdims=True) + EPS) * g


def _ada_kernel(cond_ref, w_ref, b_ref, o_ref):
    a = _silu(cond_ref[...]).astype(BF16)
    o_ref[...] = jnp.dot(a, w_ref[...].astype(BF16), preferred_element_type=F32) + b_ref[...]


def _ada(cond, w_ada, b_ada):
    tn = 2048
    n = N_MOD * D
    return pl.pallas_call(
        _ada_kernel,
        grid=(n // tn,),
        in_specs=[pl.BlockSpec((MOD_ROWS, D), lambda j: (0, 0)),
                  pl.BlockSpec((D, tn), lambda j: (0, j)),
                  pl.BlockSpec((1, tn), lambda j: (0, j))],
        out_specs=pl.BlockSpec((MOD_ROWS, tn), lambda j: (0, j)),
        out_shape=jax.ShapeDtypeStruct((MOD_ROWS, n), F32),
        compiler_params=_params(("arbitrary",)),
        name="ada",
    )(cond, w_ada, b_ada.reshape(1, n))


def _set_specs(tm, buffers):
    nc = N_CTX // tm
    mode = pl.Buffered(buffers)
    ctx = pl.BlockSpec((tm, D), lambda i, *_: (jnp.minimum(i, nc - 1), 0), pipeline_mode=mode)
    lat = pl.BlockSpec((tm, D), lambda i, *_: (jnp.maximum(i - nc, 0), 0), pipeline_mode=mode)
    return ctx, lat


def _prenorm_kernel(xc_ref, xl_ref, mod_ref, g_ref, h_ref):
    is_ctx = pl.program_id(0) < N_CTX // xc_ref.shape[0]
    x = jnp.where(is_ctx, xc_ref[...], xl_ref[...])
    m = mod_ref[0]
    h = _rms(x, g_ref[...]) * (1.0 + m[1:2, :]) + m[0:1, :]
    h_ref[...] = h.astype(BF16)


def _prenorm(x_ctx, x_lat, mod, n1g):
    tm = 512
    xc_spec, xl_spec = _set_specs(tm, 2)
    return pl.pallas_call(
        _prenorm_kernel,
        grid=(N_TOK // tm,),
        in_specs=[xc_spec, xl_spec,
                  pl.BlockSpec((1, N_MOD, D), lambda i: (_mod_row(i, tm), 0, 0)),
                  pl.BlockSpec((1, D), lambda i: (0, 0))],
        out_specs=pl.BlockSpec((tm, D), lambda i: (i, 0)),
        out_shape=jax.ShapeDtypeStruct((N_TOK, D), BF16),
        compiler_params=_params(("arbitrary",)),
        name="prenorm",
    )(x_ctx, x_lat, mod, n1g)


def _inproj_kernel(h_ref, w_ref, o_ref, wb_ref):
    @pl.when(pl.program_id(1) == 0)
    def _():
        wb_ref[...] = w_ref[...].astype(BF16)

    o_ref[...] = jnp.dot(h_ref[...], wb_ref[...], preferred_element_type=F32)


def _inproj(h, w_in):
    tm, tn = 1024, 1536
    return pl.pallas_call(
        _inproj_kernel,
        grid=(IN_COLS // tn, N_TOK // tm),
        in_specs=[pl.BlockSpec((tm, D), lambda j, i: (i, 0)),
                  pl.BlockSpec((D, tn), lambda j, i: (0, j))],
        out_specs=pl.BlockSpec((tm, tn), lambda j, i: (i, j)),
        out_shape=jax.ShapeDtypeStruct((N_TOK, IN_COLS), F32),
        scratch_shapes=[pltpu.VMEM((D, tn), BF16)],
        compiler_params=_params(("arbitrary", "arbitrary")),
        name="inproj",
    )(h, w_in)


CONV_T = 256
CONV_HALO = 16
CONV_ROWS = 32
SUBLANES = 8
LANES = 128
CONV_ZS = CONV_T + (CONV_K // SUBLANES) * SUBLANES


def _conv_kernel(main_ref, prev_ref, next_ref, w_ref, b_ref, lg_ref, lb_ref, o_ref,
                 z_ref, zs_ref, y_ref):
    i = pl.program_id(0)
    ctx_blocks = N_CTX // CONV_T
    per_seq = LAT_LEN // CONV_T
    pos = (i - ctx_blocks) % per_seq
    is_lat = i >= ctx_blocks
    has_prev = jnp.logical_and(is_lat, pos > 0)
    has_next = jnp.logical_and(is_lat, pos < per_seq - 1)

    def glu(ref):
        v = ref[...]
        return v[:, :CONV_W] * jax.nn.sigmoid(v[:, CONV_W:])

    z_ref[0:CONV_HALO, :] = jnp.where(has_prev, glu(prev_ref), 0.0)
    z_ref[CONV_HALO:CONV_HALO + CONV_T, :] = glu(main_ref)
    z_ref[CONV_HALO + CONV_T:, :] = jnp.where(has_next, glu(next_ref), 0.0)

    z = z_ref[...]
    zs_ref[0] = z[:CONV_ZS]
    for s in range(1, SUBLANES):
        zs_ref[s] = pltpu.roll(z, z.shape[0] - s, axis=0)[:CONV_ZS]

    def channel_group(c, carry):
        cols = pl.ds(pl.multiple_of(c * LANES, LANES), LANES)
        acc = jnp.zeros((CONV_T, LANES), F32)
        for j in range(CONV_K):
            a, s = divmod(CONV_HALO - CONV_PAD + j, SUBLANES)
            acc = acc + zs_ref[s, a * SUBLANES:a * SUBLANES + CONV_T, cols] * w_ref[j:j + 1, cols]
        y_ref[:, cols] = acc + b_ref[:, cols]
        return carry

    lax.fori_loop(0, CONV_W // LANES, channel_group, 0)

    for r in range(CONV_T // CONV_ROWS):
        acc = y_ref[r * CONV_ROWS:(r + 1) * CONV_ROWS, :]
        mu = jnp.mean(acc, axis=-1, keepdims=True)
        cen = acc - mu
        var = jnp.mean(cen * cen, axis=-1, keepdims=True)
        y = cen * lax.rsqrt(var + EPS) * lg_ref[...] + lb_ref[...]
        o_ref[r * CONV_ROWS:(r + 1) * CONV_ROWS, :] = _silu(y).astype(BF16)


def _conv(u, w_dw, b_dw, ln_g, ln_b):
    nblk = N_TOK // CONV_T
    hpb = CONV_T // CONV_HALO
    last = N_TOK // CONV_HALO - 1
    row = lambda i: (0, 0)
    return pl.pallas_call(
        _conv_kernel,
        grid=(nblk,),
        in_specs=[pl.BlockSpec((CONV_T, 2 * CONV_W), lambda i: (i, 0)),
                  pl.BlockSpec((CONV_HALO, 2 * CONV_W), lambda i: (jnp.maximum(i * hpb - 1, 0), 0)),
                  pl.BlockSpec((CONV_HALO, 2 * CONV_W), lambda i: (jnp.minimum((i + 1) * hpb, last), 0)),
                  pl.BlockSpec((CONV_K, CONV_W), row),
                  pl.BlockSpec((1, CONV_W), row),
                  pl.BlockSpec((1, CONV_W), row),
                  pl.BlockSpec((1, CONV_W), row)],
        out_specs=pl.BlockSpec((CONV_T, CONV_W), lambda i: (i, 0)),
        out_shape=jax.ShapeDtypeStruct((N_TOK, CONV_W), BF16),
        scratch_shapes=[pltpu.VMEM((CONV_T + 2 * CONV_HALO, CONV_W), F32),
                        pltpu.VMEM((SUBLANES, CONV_ZS, CONV_W), F32),
                        pltpu.VMEM((CONV_T, CONV_W), F32)],
        compiler_params=_params(("arbitrary",)),
        name="conv",
    )(u, u, u, w_dw, b_dw, ln_g, ln_b)


def _stack_heads(q):
    return jnp.concatenate([q[:, g * HD:(g + 1) * HD] for g in range(Q_GROUP)], axis=0)


def _unstack_heads(o, rows):
    return jnp.concatenate([o[g * rows:(g + 1) * rows, :] for g in range(Q_GROUP)], axis=1)


LOG2E = 1.4426950408889634
Q_SCALE = HD ** -0.5 * LOG2E
QW = Q_GROUP * HD


def _sink_column(sink_ref, kv, rows):
    rid = lax.broadcasted_iota(jnp.int32, (Q_GROUP * rows, 1), 0)
    col = jnp.full((Q_GROUP * rows, 1), sink_ref[kv * Q_GROUP], F32)
    for g in range(1, Q_GROUP):
        col = jnp.where(rid >= g * rows, sink_ref[kv * Q_GROUP + g], col)
    return col * LOG2E


def _with_ones(vals):
    return jnp.concatenate([vals, jnp.ones(vals.shape, vals.dtype)], axis=1)


def _softmax_av(s, sk, vals):
    m = jnp.maximum(jnp.max(s, axis=-1, keepdims=True), sk)
    if vals.shape[1] == 2 * HD:
        p = jnp.exp2(s - m).astype(BF16)
        o = jnp.dot(p, vals, preferred_element_type=F32)
        return o[:, :HD] / (o[:, HD:HD + 1] + jnp.exp2(sk - m))
    p = jnp.exp2(s - m)
    den = jnp.sum(p, axis=-1, keepdims=True) + jnp.exp2(sk - m)
    return jnp.dot(p.astype(BF16), vals, preferred_element_type=F32) / den


_NT = (((1,), (1,)), ((), ()))


CTX_SEQ_STEP = 2


def _ctx_attn_kernel(sink_ref, q_ref, k_ref, v_ref, o_ref, sk_ref, sv_ref):
    for sq in range(CTX_SEQ_STEP):
        rows = slice(sq * CTX_LEN, (sq + 1) * CTX_LEN)
        for kv in range(N_KV):
            k = k_ref[rows, kv * HD:(kv + 1) * HD]
            v = v_ref[rows, kv * HD:(kv + 1) * HD]
            sk_ref[sq, 0, :, kv, :] = k
            sv_ref[sq, 0, :, kv, :] = v
            q = (_stack_heads(q_ref[rows, kv * QW:(kv + 1) * QW]) * Q_SCALE).astype(BF16)
            s = lax.dot_general(q, k.astype(BF16), _NT, preferred_element_type=F32)
            o = _softmax_av(s, _sink_column(sink_ref, kv, CTX_LEN), v.astype(BF16))
            o_ref[rows, kv * QW:(kv + 1) * QW] = _unstack_heads(o, CTX_LEN).astype(BF16)


def _ctx_attn(u, sink):
    qcols, kcols = N_HEADS * HD, N_KV * HD
    rows = CTX_SEQ_STEP * CTX_LEN
    state_spec = pl.BlockSpec((CTX_SEQ_STEP, 1, CTX_LEN, N_KV, HD), lambda b, s: (b, 0, 0, 0, 0))
    state_shape = jax.ShapeDtypeStruct((N_CTX_SEQ, 1, CTX_LEN, N_KV, HD), F32)
    return pl.pallas_call(
        _ctx_attn_kernel,
        grid_spec=pltpu.PrefetchScalarGridSpec(
            num_scalar_prefetch=1,
            grid=(N_CTX_SEQ // CTX_SEQ_STEP,),
            in_specs=[pl.BlockSpec((rows, qcols), lambda b, s: (b, COL_Q // qcols)),
                      pl.BlockSpec((rows, kcols), lambda b, s: (b, COL_K // kcols)),
                      pl.BlockSpec((rows, kcols), lambda b, s: (b, COL_V // kcols))],
            out_specs=[pl.BlockSpec((rows, qcols), lambda b, s: (b, 0)),
                       state_spec, state_spec]),
        out_shape=[jax.ShapeDtypeStruct((N_CTX, N_HEADS * HD), BF16), state_shape, state_shape],
        compiler_params=_params(("arbitrary",)),
        name="ctx_attn",
    )(sink, u, u, u)


def _rope(x, cos, sin):
    lane = lax.broadcasted_iota(jnp.int32, x.shape, 1)
    fwd = pltpu.roll(x, HD - 32, axis=1)
    bwd = pltpu.roll(x, 32, axis=1)
    partner = jnp.where((lane & 32) == 0, fwd, bwd)
    return x * cos + partner * sin


LAT_KV_STEP = 2


def _lat_attn_kernel(sink_ref, q_ref, k_ref, v_ref, ck_ref, cv_ref, cos_ref, sin_ref, o_ref):
    nblk = LAT_LEN // ABLK
    cos, sin = cos_ref[...], sin_ref[...]
    r = lax.broadcasted_iota(jnp.int32, (Q_GROUP * ABLK, ABLK), 0) & (ABLK - 1)
    c = lax.broadcasted_iota(jnp.int32, (Q_GROUP * ABLK, ABLK), 1)
    open_tile = jnp.zeros((Q_GROUP * ABLK, ABLK), F32)
    below = jnp.where(c >= r, 0.0, NEG)
    above = jnp.where(c <= r, 0.0, NEG)
    ctx_open = jnp.zeros((Q_GROUP * ABLK, PAST), F32)
    bias_first = jnp.concatenate([open_tile, above, ctx_open], axis=1)
    bias_mid = jnp.concatenate([below, open_tile, above, ctx_open], axis=1)
    bias_last = jnp.concatenate([below, open_tile, ctx_open], axis=1)
    for h in range(LAT_KV_STEP):
        kcols = slice(h * HD, (h + 1) * HD)
        k_rot = _rope(k_ref[:, kcols], cos, sin).astype(BF16)
        v = _with_ones(v_ref[:, kcols].astype(BF16))
        ck = ck_ref[:, kcols].astype(BF16)
        cv = _with_ones(cv_ref[:, kcols].astype(BF16))
        sk = _sink_column(sink_ref, pl.program_id(1) * LAT_KV_STEP + h, ABLK)
        for j in range(nblk):
            rows = slice(j * ABLK, (j + 1) * ABLK)
            cos_j, sin_j = cos[rows], sin[rows]
            qs = jnp.concatenate(
                [_rope(q_ref[rows, h * QW + g * HD:h * QW + (g + 1) * HD], cos_j, sin_j) * Q_SCALE
                 for g in range(Q_GROUP)], axis=0).astype(BF16)
            blocks = range(max(j - 1, 0), min(j + 2, nblk))
            band = slice(blocks[0] * ABLK, (blocks[-1] + 1) * ABLK)
            keys = jnp.concatenate([k_rot[band], ck], axis=0)
            vals = jnp.concatenate([v[band], cv], axis=0)
            bias = bias_first if j == 0 else bias_last if j == nblk - 1 else bias_mid
            s = lax.dot_general(qs, keys, _NT, preferred_element_type=F32) + bias
            o = _softmax_av(s, sk, vals)
            o_ref[rows, h * QW:(h + 1) * QW] = _unstack_heads(o, ABLK).astype(BF16)


def _lat_attn(u, sink, ck, cv, cos, sin):
    base = N_CTX // LAT_LEN
    qw, kw = LAT_KV_STEP * QW, LAT_KV_STEP * HD
    cache_spec = pl.BlockSpec((None, PAST, kw), lambda b, kv, s: (b, 0, kv))
    table_spec = pl.BlockSpec((LAT_LEN, HD), lambda b, kv, s: (0, 0))
    return pl.pallas_call(
        _lat_attn_kernel,
        grid_spec=pltpu.PrefetchScalarGridSpec(
            num_scalar_prefetch=1,
            grid=(N_LAT_SEQ, N_KV // LAT_KV_STEP),
            in_specs=[pl.BlockSpec((LAT_LEN, qw), lambda b, kv, s: (base + b, COL_Q // qw + kv)),
                      pl.BlockSpec((LAT_LEN, kw), lambda b, kv, s: (base + b, COL_K // kw + kv)),
                      pl.BlockSpec((LAT_LEN, kw), lambda b, kv, s: (base + b, COL_V // kw + kv)),
                      cache_spec, cache_spec, table_spec, table_spec],
            out_specs=pl.BlockSpec((LAT_LEN, qw), lambda b, kv, s: (b, kv))),
        out_shape=jax.ShapeDtypeStruct((N_LAT, N_HEADS * HD), BF16),
        compiler_params=_params(("arbitrary", "arbitrary")),
        name="lat_attn",
    )(sink, u, u, u, ck, cv, cos, sin)


def _rope_tables():
    n = jnp.arange(LAT_LEN)
    row = (n // GRID_W).astype(F32)
    col = (n % GRID_W).astype(F32)
    quarter = HD // 4
    inv = ROPE_BASE ** (-jnp.arange(quarter, dtype=F32) / quarter)
    ang_r = row[:, None] * inv[None, :]
    ang_c = col[:, None] * inv[None, :]
    cos = jnp.concatenate([jnp.cos(ang_r)] * 2 + [jnp.cos(ang_c)] * 2, axis=-1)
    sin = jnp.concatenate([-jnp.sin(ang_r), jnp.sin(ang_r), -jnp.sin(ang_c), jnp.sin(ang_c)], axis=-1)
    return cos, sin


def _mix_kernel(z_ref, ac_ref, al_ref, ga_ref, gb_ref, wc_ref, wa_ref, o_ref, wcb_ref, wab_ref):
    @pl.when(pl.program_id(1) == 0)
    def _():
        wcb_ref[...] = wc_ref[...].astype(BF16)
        wab_ref[...] = wa_ref[...].astype(BF16)

    is_ctx = pl.program_id(1) < N_CTX // z_ref.shape[0]
    att = jnp.where(is_ctx, ac_ref[...], al_ref[...])
    conv_out = jnp.dot(z_ref[...], wcb_ref[...], preferred_element_type=F32)
    attn_out = jnp.dot(att, wab_ref[...], preferred_element_type=F32)
    mixed = jax.nn.sigmoid(ga_ref[...]) * conv_out + jax.nn.sigmoid(gb_ref[...]) * attn_out
    o_ref[...] = mixed.astype(BF16)


def _mix(zc, att_ctx, att_lat, u, w_conv_out, w_attn_out):
    tm, tn = 512, 1024
    nc = N_CTX // tm
    return pl.pallas_call(
        _mix_kernel,
        grid=(D // tn, N_TOK // tm),
        in_specs=[pl.BlockSpec((tm, CONV_W), lambda j, i: (i, 0)),
                  pl.BlockSpec((tm, N_HEADS * HD), lambda j, i: (jnp.minimum(i, nc - 1), 0)),
                  pl.BlockSpec((tm, N_HEADS * HD), lambda j, i: (jnp.maximum(i - nc, 0), 0)),
                  pl.BlockSpec((tm, tn), lambda j, i: (i, COL_GA // tn + j)),
                  pl.BlockSpec((tm, tn), lambda j, i: (i, COL_GB // tn + j)),
                  pl.BlockSpec((CONV_W, tn), lambda j, i: (0, j)),
                  pl.BlockSpec((N_HEADS * HD, tn), lambda j, i: (0, j))],
        out_specs=pl.BlockSpec((tm, tn), lambda j, i: (i, j)),
        out_shape=jax.ShapeDtypeStruct((N_TOK, D), BF16),
        scratch_shapes=[pltpu.VMEM((CONV_W, tn), BF16), pltpu.VMEM((N_HEADS * HD, tn), BF16)],
        compiler_params=_params(("arbitrary", "arbitrary")),
        name="mix",
    )(zc, att_ctx, att_lat, u, u, w_conv_out, w_attn_out)


def _split_bf16(x):
    hi = x.astype(BF16)
    lo = (x - hi.astype(F32)).astype(BF16)
    return hi, lo


def _outproj_kernel(mx_ref, xc_ref, xl_ref, mod_ref, g_ref, w_ref, wr_ref, br_ref,
                    x1_ref, h2_ref, lg_ref):
    is_ctx = pl.program_id(0) < N_CTX // xc_ref.shape[0]
    x = jnp.where(is_ctx, xc_ref[...], xl_ref[...])
    m = mod_ref[0]
    x1 = x + m[2:3, :] * jnp.dot(mx_ref[...], w_ref[...], preferred_element_type=F32)
    x1_ref[...] = x1
    h2 = _rms(x1, g_ref[...]) * (1.0 + m[4:5, :]) + m[3:4, :]
    h2_ref[...] = h2
    h_hi, h_lo = _split_bf16(h2)
    w_hi, w_lo = _split_bf16(wr_ref[...])
    lg = (jnp.dot(h_hi, w_hi, preferred_element_type=F32)
          + jnp.dot(h_hi, w_lo, preferred_element_type=F32)
          + jnp.dot(h_lo, w_hi, preferred_element_type=F32))
    lg_ref[...] = lg + br_ref[...]


def _outproj(mixed, x_ctx, x_lat, mod, n2g, w_out_bf, w_router, b_router):
    tm = 512
    const = lambda i: (0, 0)
    xc_spec, xl_spec = _set_specs(tm, 2)
    return pl.pallas_call(
        _outproj_kernel,
        grid=(N_TOK // tm,),
        in_specs=[pl.BlockSpec((tm, D), lambda i: (i, 0)),
                  xc_spec, xl_spec,
                  pl.BlockSpec((1, N_MOD, D), lambda i: (_mod_row(i, tm), 0, 0)),
                  pl.BlockSpec((1, D), const),
                  pl.BlockSpec((D, D), const, pipeline_mode=pl.Buffered(1)),
                  pl.BlockSpec((D, N_EXP), const),
                  pl.BlockSpec((1, N_EXP), const)],
        out_specs=[pl.BlockSpec((tm, D), lambda i: (i, 0)),
                   pl.BlockSpec((tm, D), lambda i: (i, 0)),
                   pl.BlockSpec((tm, N_EXP), lambda i: (i, 0))],
        out_shape=[jax.ShapeDtypeStruct((N_TOK, D), F32),
                   jax.ShapeDtypeStruct((N_TOK, D), F32),
                   jax.ShapeDtypeStruct((N_TOK, N_EXP), F32)],
        compiler_params=_params(("arbitrary",)),
        name="outproj",
    )(mixed, x_ctx, x_lat, mod, n2g, w_out_bf, w_router, b_router)


SEL_CHUNK = 128


def _prefix_sum(flags, out_ref):
    n = flags.shape[1]
    ri = lax.broadcasted_iota(jnp.int32, (SEL_CHUNK, SEL_CHUNK), 0)
    ci = lax.broadcasted_iota(jnp.int32, (SEL_CHUNK, SEL_CHUNK), 1)
    tri = jnp.where(ri <= ci, 1.0, 0.0).astype(BF16)
    run = jnp.zeros((N_EXP, 1), F32)
    for ch in range(n // SEL_CHUNK):
        blk = flags[:, ch * SEL_CHUNK:(ch + 1) * SEL_CHUNK].astype(BF16)
        within = jnp.dot(blk, tri, preferred_element_type=F32) + run
        out_ref[:, ch * SEL_CHUNK:(ch + 1) * SEL_CHUNK] = within
        run = within[:, SEL_CHUNK - 1:SEL_CHUNK]


def _select_kernel(lg_ref, idx_ref, gate_ref, cs_ref):
    n = lg_ref.shape[1]
    lg = lg_ref[...]
    ex = jnp.exp(lg - jnp.max(lg, axis=0, keepdims=True))
    aff = ex / jnp.sum(ex, axis=0, keepdims=True)

    def bit_step(b, cur):
        cand = cur | (1 << (30 - b))
        cnt = jnp.sum(jnp.where(aff >= pltpu.bitcast(cand, F32), 1.0, 0.0), axis=1, keepdims=True)
        return jnp.where(cnt >= CAP, cand, cur)

    thr = pltpu.bitcast(lax.fori_loop(0, 31, bit_step, jnp.zeros((N_EXP, 1), jnp.int32)), F32)
    above = aff > thr
    tied = aff == thr
    room = CAP - jnp.sum(jnp.where(above, 1.0, 0.0), axis=1, keepdims=True)
    _prefix_sum(jnp.where(tied, 1.0, 0.0), cs_ref)
    sel = jnp.logical_or(above, jnp.logical_and(tied, cs_ref[...] <= room))
    gsel = jnp.where(sel, aff, 0.0)
    _prefix_sum(jnp.where(sel, 1.0, 0.0), cs_ref)

    slot = lax.broadcasted_iota(jnp.int32, (CAP, n), 0).astype(F32)
    g_hi = gsel.astype(BF16).astype(F32)
    rest = gsel - g_hi
    g_mid = rest.astype(BF16).astype(F32)
    g_lo = rest - g_mid
    ones = jnp.ones((1, n), F32)
    pad = jnp.zeros((4, n), F32)
    last_slot = lax.broadcasted_iota(jnp.int32, (1, CAP), 1) == CAP - 1
    on_diag = (lax.broadcasted_iota(jnp.int32, (CAP, CAP), 0)
               == lax.broadcasted_iota(jnp.int32, (CAP, CAP), 1))
    total = jnp.sum(gsel, axis=1, keepdims=True)
    for e in range(N_EXP):
        row = slice(e, e + 1)
        le = jnp.where(cs_ref[row, :] <= slot, 1.0, 0.0).astype(BF16)
        lhs = jnp.concatenate([ones, g_hi[row], g_mid[row], g_lo[row], pad], axis=0).astype(BF16)
        res = lax.dot_general(lhs, le, (((1,), (1,)), ((), ())), preferred_element_type=F32)
        idx_ref[0, row, :] = res[0:1, :].astype(jnp.int32)
        cum = res[1:2, :] + res[2:3, :] + res[3:4, :]
        gate = jnp.where(last_slot, total[row], pltpu.roll(cum, CAP - 1, axis=1)) - cum
        gate_ref[e] = jnp.sum(jnp.where(on_diag, gate, 0.0), axis=1, keepdims=True)


def _select(logits_t):
    return pl.pallas_call(
        _select_kernel,
        grid=(N_SETS,),
        in_specs=[pl.BlockSpec((N_EXP, N_CTX), lambda s: (0, s))],
        out_specs=[pl.BlockSpec((1, N_EXP, CAP), lambda s: (s, 0, 0)),
                   pl.BlockSpec((N_EXP, CAP, 1), lambda s: (0, s, 0))],
        out_shape=[jax.ShapeDtypeStruct((N_SETS, N_EXP, CAP), jnp.int32),
                   jax.ShapeDtypeStruct((N_EXP, N_SETS * CAP, 1), F32)],
        scratch_shapes=[pltpu.VMEM((N_EXP, N_CTX), F32)],
        compiler_params=_params(("arbitrary",)),
        name="select",
    )(logits_t)


MOE_ROWS = N_SETS * CAP
MOE_TF = 256
MOE_NF = D_EXP // MOE_TF
MOE_TN = 512
N_FILL = N_TOK // MOE_ROWS
assert MOE_NF == 4


def _moe_kernel(rows_ref, h_hbm, gate_ref, wg_ref, wu_ref, wd_ref, y_hbm,
                xf_ref, xb_ref, acc_ref, rmw_ref, sem_x, sem_put, sem_get, sem_fill):
    e = pl.program_id(0)
    f = pl.program_id(1)
    e_next = jnp.minimum(e + 1, N_EXP - 1)

    def tok_row(hbm_ref, ex, k):
        return hbm_ref.at[pl.ds(rows_ref[ex * MOE_ROWS + k], 1), :]

    def x_copy(ex, k):
        return pltpu.make_async_copy(tok_row(h_hbm, ex, k), xf_ref.at[pl.ds(k, 1), :], sem_x)

    def put_copy(ex, k):
        return pltpu.make_async_copy(rmw_ref.at[pl.ds(k, 1), :], tok_row(y_hbm, ex, k), sem_put)

    def get_copy(ex, k):
        return pltpu.make_async_copy(tok_row(y_hbm, ex, k), rmw_ref.at[pl.ds(k, 1), :], sem_get)

    def fill_copy(c):
        return pltpu.make_async_copy(rmw_ref, y_hbm.at[pl.ds(c * MOE_ROWS, MOE_ROWS), :], sem_fill)

    def start_rows(copy, ex):
        for k in range(MOE_ROWS):
            copy(ex, k).start(priority=k % 2)

    def wait_rows(sem, src, dst):
        pltpu.make_async_copy(src, dst, sem).wait()

    y_rows = y_hbm.at[pl.ds(0, MOE_ROWS), :]
    h_rows = h_hbm.at[pl.ds(0, MOE_ROWS), :]

    def ffn_step(first=False):
        x = xb_ref[...]
        g = jnp.dot(x, wg_ref[...].astype(BF16), preferred_element_type=F32)
        up = jnp.dot(x, wu_ref[...].astype(BF16), preferred_element_type=F32)
        hid = (_silu(g) * up).astype(BF16)
        wd = wd_ref[...].astype(BF16)
        for c in range(D // MOE_TN):
            cols = slice(c * MOE_TN, (c + 1) * MOE_TN)
            part = jnp.dot(hid, wd[:, cols], preferred_element_type=F32)
            if first:
                acc_ref[:, cols] = part
            else:
                acc_ref[:, cols] += part

    @pl.when(jnp.logical_and(e == 0, f == 0))
    def _():
        rmw_ref[...] = jnp.zeros(rmw_ref.shape, F32)
        for c in range(N_FILL):
            fill_copy(c).start()

        def issue(k, c):
            x_copy(0, k).start()
            return c
        lax.fori_loop(0, MOE_ROWS, issue, 0)

    @pl.when(f == 0)
    def _():
        wait_rows(sem_x, h_rows, xf_ref)
        xb_ref[...] = xf_ref[...].astype(BF16)

    @pl.when(jnp.logical_and(e > 0, f == 0))
    def _():
        start_rows(put_copy, e - 1)
        ffn_step(first=True)

    @pl.when(jnp.logical_and(e == 0, f == 0))
    def _():
        ffn_step(first=True)

    @pl.when(f == 1)
    def _():
        start_rows(x_copy, e_next)
        ffn_step()

    @pl.when(jnp.logical_and(e == 0, f == 2))
    def _():
        for c in range(N_FILL):
            fill_copy(c).wait()

    @pl.when(jnp.logical_and(e > 0, f == 2))
    def _():
        wait_rows(sem_put, rmw_ref, y_rows)

    @pl.when(f == 2)
    def _():
        start_rows(get_copy, e)
        ffn_step()

    @pl.when(f == 3)
    def _():
        ffn_step()
        wait_rows(sem_get, y_rows, rmw_ref)
        rmw_ref[...] = rmw_ref[...] + gate_ref[...] * acc_ref[...]

    @pl.when(jnp.logical_and(e == N_EXP - 1, f == MOE_NF - 1))
    def _():
        def issue(k, c):
            put_copy(N_EXP - 1, k).start()
            return c
        lax.fori_loop(0, MOE_ROWS, issue, 0)
        wait_rows(sem_put, rmw_ref, y_rows)
        wait_rows(sem_x, h_rows, xf_ref)


def _moe(rows, h2, gates, w_g, w_u, w_d):
    return pl.pallas_call(
        _moe_kernel,
        grid_spec=pltpu.PrefetchScalarGridSpec(
            num_scalar_prefetch=1,
            grid=(N_EXP, MOE_NF),
            in_specs=[pl.BlockSpec(memory_space=pl.ANY),
                      pl.BlockSpec((None, MOE_ROWS, 1), lambda e, f, s: (e, 0, 0)),
                      pl.BlockSpec((None, D, MOE_TF), lambda e, f, s: (e, 0, f)),
                      pl.BlockSpec((None, D, MOE_TF), lambda e, f, s: (e, 0, f)),
                      pl.BlockSpec((None, MOE_TF, D), lambda e, f, s: (e, f, 0))],
            out_specs=pl.BlockSpec(memory_space=pl.ANY),
            scratch_shapes=[pltpu.VMEM((MOE_ROWS, D), F32),
                            pltpu.VMEM((MOE_ROWS, D), BF16),
                            pltpu.VMEM((MOE_ROWS, D), F32),
                            pltpu.VMEM((MOE_ROWS, D), F32),
                            pltpu.SemaphoreType.DMA(()),
                            pltpu.SemaphoreType.DMA(()),
                            pltpu.SemaphoreType.DMA(()),
                            pltpu.SemaphoreType.DMA(())]),
        out_shape=jax.ShapeDtypeStruct((N_TOK, D), F32),
        compiler_params=_params(("arbitrary", "arbitrary")),
        name="moe",
    )(rows, h2, gates, w_g, w_u, w_d)


def _final_kernel(x1_ref, y_ref, mod_ref, g_ref, o_ref):
    o_ref[...] = _rms(x1_ref[...] + mod_ref[0][5:6, :] * y_ref[...], g_ref[...])


def _final(x1, y, tok0, n_tok, mod, fng):
    tm = 512
    blk0 = tok0 // tm
    return pl.pallas_call(
        _final_kernel,
        grid=(n_tok // tm,),
        in_specs=[pl.BlockSpec((tm, D), lambda i: (blk0 + i, 0)),
                  pl.BlockSpec((tm, D), lambda i: (blk0 + i, 0)),
                  pl.BlockSpec((1, N_MOD, D), lambda i: (_mod_row(blk0 + i, tm), 0, 0)),
                  pl.BlockSpec((1, D), lambda i: (0, 0))],
        out_specs=pl.BlockSpec((tm, D), lambda i: (i, 0)),
        out_shape=jax.ShapeDtypeStruct((n_tok, D), F32),
        compiler_params=_params(("arbitrary",)),
        name="final",
    )(x1, y, mod, fng)


def kernel(x_prompt, x_sample, c, cache_k, cache_v, c_ctx, w_ada, b_ada, norm1_g, w_in, conv_dw, conv_dw_b, conv_ln_g, conv_ln_b, w_conv_out, attn_sink, w_attn_out, w_out, norm2_g, w_router, b_router, w_exp_gate, w_exp_up, w_exp_down, final_norm_g):
    x_ctx = x_prompt.reshape(N_CTX, D)
    x_lat = x_sample.reshape(N_LAT, D)
    cond = jnp.concatenate([c, c_ctx[None, :], jnp.zeros((MOD_ROWS - N_LAT_SEQ - 1, D), F32)], axis=0)
    mod = _ada(cond, w_ada[0], b_ada[0]).reshape(MOD_ROWS, N_MOD, D)

    u = _inproj(_prenorm(x_ctx, x_lat, mod, norm1_g), w_in[0])
    zc = _conv(u, conv_dw[0], conv_dw_b, conv_ln_g, conv_ln_b)

    sink = attn_sink[0]
    cos, sin = _rope_tables()
    ck = cache_k[:, 0].reshape(N_LAT_SEQ, PAST, N_KV * HD)
    cv = cache_v[:, 0].reshape(N_LAT_SEQ, PAST, N_KV * HD)
    att_ctx, state_k, state_v = _ctx_attn(u, sink)
    mixed = _mix(zc, att_ctx, _lat_attn(u, sink, ck, cv, cos, sin), u,
                 w_conv_out[0], w_attn_out[0])

    x1, h2, logits = _outproj(mixed, x_ctx, x_lat, mod, norm2_g, w_out[0].astype(BF16),
                              w_router[0], b_router)

    idx, gates = _select(logits.T)
    rows = (idx + (jnp.arange(N_SETS, dtype=jnp.int32) * N_CTX)[:, None, None])
    rows = rows.transpose(1, 0, 2).reshape(N_EXP * MOE_ROWS)
    ffn = _moe(rows, h2, gates, w_exp_gate[0], w_exp_up[0], w_exp_down[0])

    fng = final_norm_g.reshape(1, D)
    y_prompt = _final(x1, ffn, 0, N_CTX, mod, fng).reshape(N_CTX_SEQ, CTX_LEN, D)
    y_sample = _final(x1, ffn, N_CTX, N_LAT, mod, fng).reshape(N_LAT_SEQ, LAT_LEN, D)
    return (y_prompt, y_sample, state_k, state_v)
```

```python
import jax
import jax.numpy as jnp
from jax import lax
from jax.experimental import pallas as pl
from jax.experimental.pallas import tpu as pltpu

D = 2048
N_CTX_SEQ, CTX_LEN = 16, 256
N_LAT_SEQ, LAT_LEN = 4, 1024
N_CTX = N_CTX_SEQ * CTX_LEN
N_LAT = N_LAT_SEQ * LAT_LEN
N_TOK = N_CTX + N_LAT
PAST = 512
GRID_W = 64
N_HEADS, N_KV, HD = 16, 4, 128
Q_GROUP = N_HEADS // N_KV
WINDOW = 128
ABLK = 128
CONV_W = 1024
CONV_K = 31
CONV_PAD = CONV_K // 2
N_EXP = 16
D_EXP = 1024
CAP = 2 * N_CTX // N_EXP
N_SETS = 2
ROPE_BASE = 10000.0
EPS = 1e-6
N_MOD = 6
NEG = -1e30
IN_COLS = 2 * CONV_W + N_HEADS * HD + 2 * N_KV * HD + 2 * D
COL_Q = 2 * CONV_W
COL_K = COL_Q + N_HEADS * HD
COL_V = COL_K + N_KV * HD
COL_GA = COL_V + N_KV * HD
COL_GB = COL_GA + D
MOD_ROWS = 8
CTX_MOD_ROW = N_LAT_SEQ

VMEM_LIMIT = 58 * 1024 * 1024
F32 = jnp.float32
BF16 = jnp.bfloat16


def _params(sem, vmem=VMEM_LIMIT):
    return pltpu.CompilerParams(dimension_semantics=sem, vmem_limit_bytes=vmem)


def _mod_row(tok_block, block_tokens):
    ctx_blocks = N_CTX // block_tokens
    per_seq = LAT_LEN // block_tokens
    return jnp.where(tok_block < ctx_blocks, CTX_MOD_ROW, (tok_block - ctx_blocks) // per_seq)


def _silu(x):
    return x * jax.nn.sigmoid(x)


def _rms(x, g):
    return x * lax.rsqrt(jnp.mean(x * x, axis=-1, keepdims=True) + EPS) * g


def _ada_kernel(cond_ref, w_ref, b_ref, o_ref):
    a = _silu(cond_ref[...]).astype(BF16)
    o_ref[...] = jnp.dot(a, w_ref[...].astype(BF16), preferred_element_type=F32) + b_ref[...]


def _ada(cond, w_ada, b_ada):
    tn = 1024
    n = N_MOD * D
    return pl.pallas_call(
        _ada_kernel,
        grid=(n // tn,),
        in_specs=[pl.BlockSpec((MOD_ROWS, D), lambda j: (0, 0)),
                  pl.BlockSpec((D, tn), lambda j: (0, j)),
                  pl.BlockSpec((1, tn), lambda j: (0, j))],
        out_specs=pl.BlockSpec((MOD_ROWS, tn), lambda j: (0, j)),
        out_shape=jax.ShapeDtypeStruct((MOD_ROWS, n), F32),
        compiler_params=_params(("arbitrary",)),
        name="ada",
    )(cond, w_ada, b_ada.reshape(1, n))


def _set_specs(tm, buffers):
    nc = N_CTX // tm
    mode = pl.Buffered(buffers)
    ctx = pl.BlockSpec((tm, D), lambda i, *_: (jnp.minimum(i, nc - 1), 0), pipeline_mode=mode)
    lat = pl.BlockSpec((tm, D), lambda i, *_: (jnp.maximum(i - nc, 0), 0), pipeline_mode=mode)
    return ctx, lat


def _prenorm_kernel(xc_ref, xl_ref, mod_ref, g_ref, h_ref):
    is_ctx = pl.program_id(0) < N_CTX // xc_ref.shape[0]
    x = jnp.where(is_ctx, xc_ref[...], xl_ref[...])
    m = mod_ref[0]
    h = _rms(x, g_ref[...]) * (1.0 + m[1:2, :]) + m[0:1, :]
    h_ref[...] = h.astype(BF16)


def _prenorm(x_ctx, x_lat, mod, n1g):
    tm = 512
    xc_spec, xl_spec = _set_specs(tm, 2)
    return pl.pallas_call(
        _prenorm_kernel,
        grid=(N_TOK // tm,),
        in_specs=[xc_spec, xl_spec,
                  pl.BlockSpec((1, N_MOD, D), lambda i: (_mod_row(i, tm), 0, 0)),
                  pl.BlockSpec((1, D), lambda i: (0, 0))],
        out_specs=pl.BlockSpec((tm, D), lambda i: (i, 0)),
        out_shape=jax.ShapeDtypeStruct((N_TOK, D), BF16),
        compiler_params=_params(("arbitrary",)),
        name="prenorm",
    )(x_ctx, x_lat, mod, n1g)


def _inproj_kernel(h_ref, w_ref, o_ref, wb_ref):
    @pl.when(pl.program_id(1) == 0)
    def _():
        wb_ref[...] = w_ref[...].astype(BF16)

    o_ref[...] = jnp.dot(h_ref[...], wb_ref[...], preferred_element_type=F32)


def _inproj(h, w_in):
    tm, tn = 1024, 1536
    return pl.pallas_call(
        _inproj_kernel,
        grid=(IN_COLS // tn, N_TOK // tm),
        in_specs=[pl.BlockSpec((tm, D), lambda j, i: (i, 0)),
                  pl.BlockSpec((D, tn), lambda j, i: (0, j))],
        out_specs=pl.BlockSpec((tm, tn), lambda j, i: (i, j)),
        out_shape=jax.ShapeDtypeStruct((N_TOK, IN_COLS), F32),
        scratch_shapes=[pltpu.VMEM((D, tn), BF16)],
        compiler_params=_params(("arbitrary", "arbitrary")),
        name="inproj",
    )(h, w_in)


CONV_T = 256
CONV_HALO = 16
CONV_ROWS = 32
SUBLANES = 8
LANES = 128
CONV_ZS = CONV_T + (CONV_K // SUBLANES) * SUBLANES


def _conv_kernel(main_ref, prev_ref, next_ref, w_ref, b_ref, lg_ref, lb_ref, o_ref,
                 z_ref, zs_ref, y_ref):
    i = pl.program_id(0)
    ctx_blocks = N_CTX // CONV_T
    per_seq = LAT_LEN // CONV_T
    pos = (i - ctx_blocks) % per_seq
    is_lat = i >= ctx_blocks
    has_prev = jnp.logical_and(is_lat, pos > 0)
    has_next = jnp.logical_and(is_lat, pos < per_seq - 1)

    def glu(ref):
        v = ref[...]
        return v[:, :CONV_W] * jax.nn.sigmoid(v[:, CONV_W:])

    z_ref[0:CONV_HALO, :] = jnp.where(has_prev, glu(prev_ref), 0.0)
    z_ref[CONV_HALO:CONV_HALO + CONV_T, :] = glu(main_ref)
    z_ref[CONV_HALO + CONV_T:, :] = jnp.where(has_next, glu(next_ref), 0.0)

    z = z_ref[...]
    zs_ref[0] = z[:CONV_ZS]
    for s in range(1, SUBLANES):
        zs_ref[s] = pltpu.roll(z, z.shape[0] - s, axis=0)[:CONV_ZS]

    def channel_group(c, carry):
        cols = pl.ds(pl.multiple_of(c * LANES, LANES), LANES)
        acc = jnp.zeros((CONV_T, LANES), F32)
        for j in range(CONV_K):
            a, s = divmod(CONV_HALO - CONV_PAD + j, SUBLANES)
            acc = acc + zs_ref[s, a * SUBLANES:a * SUBLANES + CONV_T, cols] * w_ref[j:j + 1, cols]
        y_ref[:, cols] = acc + b_ref[:, cols]
        return carry

    lax.fori_loop(0, CONV_W // LANES, channel_group, 0)

    for r in range(CONV_T // CONV_ROWS):
        acc = y_ref[r * CONV_ROWS:(r + 1) * CONV_ROWS, :]
        mu = jnp.mean(acc, axis=-1, keepdims=True)
        cen = acc - mu
        var = jnp.mean(cen * cen, axis=-1, keepdims=True)
        y = cen * lax.rsqrt(var + EPS) * lg_ref[...] + lb_ref[...]
        o_ref[r * CONV_ROWS:(r + 1) * CONV_ROWS, :] = _silu(y).astype(BF16)


def _conv(u, w_dw, b_dw, ln_g, ln_b):
    nblk = N_TOK // CONV_T
    hpb = CONV_T // CONV_HALO
    last = N_TOK // CONV_HALO - 1
    row = lambda i: (0, 0)
    return pl.pallas_call(
        _conv_kernel,
        grid=(nblk,),
        in_specs=[pl.BlockSpec((CONV_T, 2 * CONV_W), lambda i: (i, 0)),
                  pl.BlockSpec((CONV_HALO, 2 * CONV_W), lambda i: (jnp.maximum(i * hpb - 1, 0), 0)),
                  pl.BlockSpec((CONV_HALO, 2 * CONV_W), lambda i: (jnp.minimum((i + 1) * hpb, last), 0)),
                  pl.BlockSpec((CONV_K, CONV_W), row),
                  pl.BlockSpec((1, CONV_W), row),
                  pl.BlockSpec((1, CONV_W), row),
                  pl.BlockSpec((1, CONV_W), row)],
        out_specs=pl.BlockSpec((CONV_T, CONV_W), lambda i: (i, 0)),
        out_shape=jax.ShapeDtypeStruct((N_TOK, CONV_W), BF16),
        scratch_shapes=[pltpu.VMEM((CONV_T + 2 * CONV_HALO, CONV_W), F32),
                        pltpu.VMEM((SUBLANES, CONV_ZS, CONV_W), F32),
                        pltpu.VMEM((CONV_T, CONV_W), F32)],
        compiler_params=_params(("arbitrary",)),
        name="conv",
    )(u, u, u, w_dw, b_dw, ln_g, ln_b)


def _stack_heads(q):
    return jnp.concatenate([q[:, g * HD:(g + 1) * HD] for g in range(Q_GROUP)], axis=0)


def _unstack_heads(o, rows):
    return jnp.concatenate([o[g * rows:(g + 1) * rows, :] for g in range(Q_GROUP)], axis=1)


LOG2E = 1.4426950408889634
Q_SCALE = HD ** -0.5 * LOG2E
QW = Q_GROUP * HD


def _sink_column(sink_ref, kv, rows):
    rid = lax.broadcasted_iota(jnp.int32, (Q_GROUP * rows, 1), 0)
    col = jnp.full((Q_GROUP * rows, 1), sink_ref[kv * Q_GROUP], F32)
    for g in range(1, Q_GROUP):
        col = jnp.where(rid >= g * rows, sink_ref[kv * Q_GROUP + g], col)
    return col * LOG2E


def _with_ones(vals):
    return jnp.concatenate([vals, jnp.ones(vals.shape, vals.dtype)], axis=1)


def _softmax_av(s, sk, vals):
    m = jnp.maximum(jnp.max(s, axis=-1, keepdims=True), sk)
    if vals.shape[1] == 2 * HD:
        p = jnp.exp2(s - m).astype(BF16)
        o = jnp.dot(p, vals, preferred_element_type=F32)
        return o[:, :HD] / (o[:, HD:HD + 1] + jnp.exp2(sk - m))
    p = jnp.exp2(s - m)
    den = jnp.sum(p, axis=-1, keepdims=True) + jnp.exp2(sk - m)
    return jnp.dot(p.astype(BF16), vals, preferred_element_type=F32) / den


_NT = (((1,), (1,)), ((), ()))


def _ctx_attn_kernel(sink_ref, q_ref, k_ref, v_ref, o_ref, sk_ref, sv_ref):
    for kv in range(N_KV):
        k = k_ref[:, kv * HD:(kv + 1) * HD]
        v = v_ref[:, kv * HD:(kv + 1) * HD]
        sk_ref[0, 0, :, kv, :] = k
        sv_ref[0, 0, :, kv, :] = v
        q = (_stack_heads(q_ref[:, kv * QW:(kv + 1) * QW]) * Q_SCALE).astype(BF16)
        s = lax.dot_general(q, k.astype(BF16), _NT, preferred_element_type=F32)
        o = _softmax_av(s, _sink_column(sink_ref, kv, CTX_LEN), v.astype(BF16))
        o_ref[:, kv * QW:(kv + 1) * QW] = _unstack_heads(o, CTX_LEN).astype(BF16)


def _ctx_attn(u, sink):
    qcols, kcols = N_HEADS * HD, N_KV * HD
    state_spec = pl.BlockSpec((1, 1, CTX_LEN, N_KV, HD), lambda b, s: (b, 0, 0, 0, 0))
    state_shape = jax.ShapeDtypeStruct((N_CTX_SEQ, 1, CTX_LEN, N_KV, HD), F32)
    return pl.pallas_call(
        _ctx_attn_kernel,
        grid_spec=pltpu.PrefetchScalarGridSpec(
            num_scalar_prefetch=1,
            grid=(N_CTX_SEQ,),
            in_specs=[pl.BlockSpec((CTX_LEN, qcols), lambda b, s: (b, COL_Q // qcols)),
                      pl.BlockSpec((CTX_LEN, kcols), lambda b, s: (b, COL_K // kcols)),
                      pl.BlockSpec((CTX_LEN, kcols), lambda b, s: (b, COL_V // kcols))],
            out_specs=[pl.BlockSpec((CTX_LEN, qcols), lambda b, s: (b, 0)),
                       state_spec, state_spec]),
        out_shape=[jax.ShapeDtypeStruct((N_CTX, N_HEADS * HD), BF16), state_shape, state_shape],
        compiler_params=_params(("arbitrary",)),
        name="ctx_attn",
    )(sink, u, u, u)


def _rope(x, cos, sin):
    lane = lax.broadcasted_iota(jnp.int32, x.shape, 1)
    fwd = pltpu.roll(x, HD - 32, axis=1)
    bwd = pltpu.roll(x, 32, axis=1)
    partner = jnp.where((lane & 32) == 0, fwd, bwd)
    return x * cos + partner * sin


def _lat_attn_kernel(sink_ref, q_ref, k_ref, v_ref, ck_ref, cv_ref, cos_ref, sin_ref, o_ref):
    kv = pl.program_id(1)
    nblk = LAT_LEN // ABLK
    cos, sin = cos_ref[...], sin_ref[...]
    k_rot = _rope(k_ref[...], cos, sin).astype(BF16)
    v = _with_ones(v_ref[...].astype(BF16))
    ck = ck_ref[...].astype(BF16)
    cv = _with_ones(cv_ref[...].astype(BF16))
    sk = _sink_column(sink_ref, kv, ABLK)
    r = lax.broadcasted_iota(jnp.int32, (Q_GROUP * ABLK, ABLK), 0) & (ABLK - 1)
    c = lax.broadcasted_iota(jnp.int32, (Q_GROUP * ABLK, ABLK), 1)
    open_tile = jnp.zeros((Q_GROUP * ABLK, ABLK), F32)
    below = jnp.where(c >= r, 0.0, NEG)
    above = jnp.where(c <= r, 0.0, NEG)
    ctx_open = jnp.zeros((Q_GROUP * ABLK, PAST), F32)
    bias_first = jnp.concatenate([open_tile, above, ctx_open], axis=1)
    bias_mid = jnp.concatenate([below, open_tile, above, ctx_open], axis=1)
    bias_last = jnp.concatenate([below, open_tile, ctx_open], axis=1)
    for j in range(nblk):
        rows = slice(j * ABLK, (j + 1) * ABLK)
        cos_j, sin_j = cos[rows], sin[rows]
        qs = jnp.concatenate(
            [_rope(q_ref[rows, g * HD:(g + 1) * HD], cos_j, sin_j) * Q_SCALE for g in range(Q_GROUP)],
            axis=0).astype(BF16)
        blocks = range(max(j - 1, 0), min(j + 2, nblk))
        band = slice(blocks[0] * ABLK, (blocks[-1] + 1) * ABLK)
        keys = jnp.concatenate([k_rot[band], ck], axis=0)
        vals = jnp.concatenate([v[band], cv], axis=0)
        bias = bias_first if j == 0 else bias_last if j == nblk - 1 else bias_mid
        s = lax.dot_general(qs, keys, _NT, preferred_element_type=F32) + bias
        o = _softmax_av(s, sk, vals)
        o_ref[rows, :] = _unstack_heads(o, ABLK).astype(BF16)


def _lat_attn(u, sink, ck, cv, cos, sin):
    base = N_CTX // LAT_LEN
    cache_spec = pl.BlockSpec((None, PAST, HD), lambda b, kv, s: (b, 0, kv))
    table_spec = pl.BlockSpec((LAT_LEN, HD), lambda b, kv, s: (0, 0))
    return pl.pallas_call(
        _lat_attn_kernel,
        grid_spec=pltpu.PrefetchScalarGridSpec(
            num_scalar_prefetch=1,
            grid=(N_LAT_SEQ, N_KV),
            in_specs=[pl.BlockSpec((LAT_LEN, QW), lambda b, kv, s: (base + b, COL_Q // QW + kv)),
                      pl.BlockSpec((LAT_LEN, HD), lambda b, kv, s: (base + b, COL_K // HD + kv)),
                      pl.BlockSpec((LAT_LEN, HD), lambda b, kv, s: (base + b, COL_V // HD + kv)),
                      cache_spec, cache_spec, table_spec, table_spec],
            out_specs=pl.BlockSpec((LAT_LEN, QW), lambda b, kv, s: (b, kv))),
        out_shape=jax.ShapeDtypeStruct((N_LAT, N_HEADS * HD), BF16),
        compiler_params=_params(("arbitrary", "arbitrary")),
        name="lat_attn",
    )(sink, u, u, u, ck, cv, cos, sin)


def _rope_tables():
    n = jnp.arange(LAT_LEN)
    row = (n // GRID_W).astype(F32)
    col = (n % GRID_W).astype(F32)
    quarter = HD // 4
    inv = ROPE_BASE ** (-jnp.arange(quarter, dtype=F32) / quarter)
    ang_r = row[:, None] * inv[None, :]
    ang_c = col[:, None] * inv[None, :]
    cos = jnp.concatenate([jnp.cos(ang_r)] * 2 + [jnp.cos(ang_c)] * 2, axis=-1)
    sin = jnp.concatenate([-jnp.sin(ang_r), jnp.sin(ang_r), -jnp.sin(ang_c), jnp.sin(ang_c)], axis=-1)
    return cos, sin


def _mix_kernel(z_ref, ac_ref, al_ref, ga_ref, gb_ref, wc_ref, wa_ref, o_ref, wcb_ref, wab_ref):
    @pl.when(pl.program_id(1) == 0)
    def _():
        wcb_ref[...] = wc_ref[...].astype(BF16)
        wab_ref[...] = wa_ref[...].astype(BF16)

    is_ctx = pl.program_id(1) < N_CTX // z_ref.shape[0]
    att = jnp.where(is_ctx, ac_ref[...], al_ref[...])
    conv_out = jnp.dot(z_ref[...], wcb_ref[...], preferred_element_type=F32)
    attn_out = jnp.dot(att, wab_ref[...], preferred_element_type=F32)
    mixed = jax.nn.sigmoid(ga_ref[...]) * conv_out + jax.nn.sigmoid(gb_ref[...]) * attn_out
    o_ref[...] = mixed.astype(BF16)


def _mix(zc, att_ctx, att_lat, u, w_conv_out, w_attn_out):
    tm, tn = 512, 1024
    nc = N_CTX // tm
    return pl.pallas_call(
        _mix_kernel,
        grid=(D // tn, N_TOK // tm),
        in_specs=[pl.BlockSpec((tm, CONV_W), lambda j, i: (i, 0)),
                  pl.BlockSpec((tm, N_HEADS * HD), lambda j, i: (jnp.minimum(i, nc - 1), 0)),
                  pl.BlockSpec((tm, N_HEADS * HD), lambda j, i: (jnp.maximum(i - nc, 0), 0)),
                  pl.BlockSpec((tm, tn), lambda j, i: (i, COL_GA // tn + j)),
                  pl.BlockSpec((tm, tn), lambda j, i: (i, COL_GB // tn + j)),
                  pl.BlockSpec((CONV_W, tn), lambda j, i: (0, j)),
                  pl.BlockSpec((N_HEADS * HD, tn), lambda j, i: (0, j))],
        out_specs=pl.BlockSpec((tm, tn), lambda j, i: (i, j)),
        out_shape=jax.ShapeDtypeStruct((N_TOK, D), BF16),
        scratch_shapes=[pltpu.VMEM((CONV_W, tn), BF16), pltpu.VMEM((N_HEADS * HD, tn), BF16)],
        compiler_params=_params(("arbitrary", "arbitrary")),
        name="mix",
    )(zc, att_ctx, att_lat, u, u, w_conv_out, w_attn_out)


def _split_bf16(x):
    hi = x.astype(BF16)
    lo = (x - hi.astype(F32)).astype(BF16)
    return hi, lo


def _outproj_kernel(mx_ref, xc_ref, xl_ref, mod_ref, g_ref, w_ref, wr_ref, br_ref,
                    x1_ref, h2_ref, lg_ref):
    is_ctx = pl.program_id(0) < N_CTX // xc_ref.shape[0]
    x = jnp.where(is_ctx, xc_ref[...], xl_ref[...])
    m = mod_ref[0]
    x1 = x + m[2:3, :] * jnp.dot(mx_ref[...], w_ref[...], preferred_element_type=F32)
    x1_ref[...] = x1
    h2 = _rms(x1, g_ref[...]) * (1.0 + m[4:5, :]) + m[3:4, :]
    h2_ref[...] = h2
    h_hi, h_lo = _split_bf16(h2)
    w_hi, w_lo = _split_bf16(wr_ref[...])
    lg = (jnp.dot(h_hi, w_hi, preferred_element_type=F32)
          + jnp.dot(h_hi, w_lo, preferred_element_type=F32)
          + jnp.dot(h_lo, w_hi, preferred_element_type=F32))
    lg_ref[...] = lg + br_ref[...]


def _outproj(mixed, x_ctx, x_lat, mod, n2g, w_out_bf, w_router, b_router):
    tm = 512
    const = lambda i: (0, 0)
    xc_spec, xl_spec = _set_specs(tm, 2)
    return pl.pallas_call(
        _outproj_kernel,
        grid=(N_TOK // tm,),
        in_specs=[pl.BlockSpec((tm, D), lambda i: (i, 0)),
                  xc_spec, xl_spec,
                  pl.BlockSpec((1, N_MOD, D), lambda i: (_mod_row(i, tm), 0, 0)),
                  pl.BlockSpec((1, D), const),
                  pl.BlockSpec((D, D), const, pipeline_mode=pl.Buffered(1)),
                  pl.BlockSpec((D, N_EXP), const),
                  pl.BlockSpec((1, N_EXP), const)],
        out_specs=[pl.BlockSpec((tm, D), lambda i: (i, 0)),
                   pl.BlockSpec((tm, D), lambda i: (i, 0)),
                   pl.BlockSpec((tm, N_EXP), lambda i: (i, 0))],
        out_shape=[jax.ShapeDtypeStruct((N_TOK, D), F32),
                   jax.ShapeDtypeStruct((N_TOK, D), F32),
                   jax.ShapeDtypeStruct((N_TOK, N_EXP), F32)],
        compiler_params=_params(("arbitrary",)),
        name="outproj",
    )(mixed, x_ctx, x_lat, mod, n2g, w_out_bf, w_router, b_router)


SEL_CHUNK = 128


def _prefix_sum(flags, out_ref):
    n = flags.shape[1]
    ri = lax.broadcasted_iota(jnp.int32, (SEL_CHUNK, SEL_CHUNK), 0)
    ci = lax.broadcasted_iota(jnp.int32, (SEL_CHUNK, SEL_CHUNK), 1)
    tri = jnp.where(ri <= ci, 1.0, 0.0).astype(BF16)
    run = jnp.zeros((N_EXP, 1), F32)
    for ch in range(n // SEL_CHUNK):
        blk = flags[:, ch * SEL_CHUNK:(ch + 1) * SEL_CHUNK].astype(BF16)
        within = jnp.dot(blk, tri, preferred_element_type=F32) + run
        out_ref[:, ch * SEL_CHUNK:(ch + 1) * SEL_CHUNK] = within
        run = within[:, SEL_CHUNK - 1:SEL_CHUNK]


def _select_kernel(lg_ref, idx_ref, gate_ref, cs_ref):
    n = lg_ref.shape[1]
    lg = lg_ref[...]
    ex = jnp.exp(lg - jnp.max(lg, axis=0, keepdims=True))
    aff = ex / jnp.sum(ex, axis=0, keepdims=True)

    def bit_step(b, cur):
        cand = cur | (1 << (30 - b))
        cnt = jnp.sum(jnp.where(aff >= pltpu.bitcast(cand, F32), 1.0, 0.0), axis=1, keepdims=True)
        return jnp.where(cnt >= CAP, cand, cur)

    thr = pltpu.bitcast(lax.fori_loop(0, 31, bit_step, jnp.zeros((N_EXP, 1), jnp.int32)), F32)
    above = aff > thr
    tied = aff == thr
    room = CAP - jnp.sum(jnp.where(above, 1.0, 0.0), axis=1, keepdims=True)
    _prefix_sum(jnp.where(tied, 1.0, 0.0), cs_ref)
    sel = jnp.logical_or(above, jnp.logical_and(tied, cs_ref[...] <= room))
    gsel = jnp.where(sel, aff, 0.0)
    _prefix_sum(jnp.where(sel, 1.0, 0.0), cs_ref)

    slot = lax.broadcasted_iota(jnp.int32, (CAP, n), 0).astype(F32)
    g_hi = gsel.astype(BF16).astype(F32)
    rest = gsel - g_hi
    g_mid = rest.astype(BF16).astype(F32)
    g_lo = rest - g_mid
    ones = jnp.ones((1, n), F32)
    pad = jnp.zeros((4, n), F32)
    last_slot = lax.broadcasted_iota(jnp.int32, (1, CAP), 1) == CAP - 1
    on_diag = (lax.broadcasted_iota(jnp.int32, (CAP, CAP), 0)
               == lax.broadcasted_iota(jnp.int32, (CAP, CAP), 1))
    total = jnp.sum(gsel, axis=1, keepdims=True)
    for e in range(N_EXP):
        row = slice(e, e + 1)
        le = jnp.where(cs_ref[row, :] <= slot, 1.0, 0.0).astype(BF16)
        lhs = jnp.concatenate([ones, g_hi[row], g_mid[row], g_lo[row], pad], axis=0).astype(BF16)
        res = lax.dot_general(lhs, le, (((1,), (1,)), ((), ())), preferred_element_type=F32)
        idx_ref[0, row, :] = res[0:1, :].astype(jnp.int32)
        cum = res[1:2, :] + res[2:3, :] + res[3:4, :]
        gate = jnp.where(last_slot, total[row], pltpu.roll(cum, CAP - 1, axis=1)) - cum
        gate_ref[e] = jnp.sum(jnp.where(on_diag, gate, 0.0), axis=1, keepdims=True)


def _select(logits_t):
    return pl.pallas_call(
        _select_kernel,
        grid=(N_SETS,),
        in_specs=[pl.BlockSpec((N_EXP, N_CTX), lambda s: (0, s))],
        out_specs=[pl.BlockSpec((1, N_EXP, CAP), lambda s: (s, 0, 0)),
                   pl.BlockSpec((N_EXP, CAP, 1), lambda s: (0, s, 0))],
        out_shape=[jax.ShapeDtypeStruct((N_SETS, N_EXP, CAP), jnp.int32),
                   jax.ShapeDtypeStruct((N_EXP, N_SETS * CAP, 1), F32)],
        scratch_shapes=[pltpu.VMEM((N_EXP, N_CTX), F32)],
        compiler_params=_params(("arbitrary",)),
        name="select",
    )(logits_t)


MOE_ROWS = N_SETS * CAP
MOE_TF = 256
MOE_NF = D_EXP // MOE_TF
MOE_TN = 512
N_FILL = N_TOK // MOE_ROWS
assert MOE_NF == 4


def _moe_kernel(rows_ref, h_hbm, gate_ref, wg_ref, wu_ref, wd_ref, y_hbm,
                xf_ref, xb_ref, acc_ref, rmw_ref, sem_x, sem_put, sem_get, sem_fill):
    e = pl.program_id(0)
    f = pl.program_id(1)
    e_next = jnp.minimum(e + 1, N_EXP - 1)

    def tok_row(hbm_ref, ex, k):
        return hbm_ref.at[pl.ds(rows_ref[ex * MOE_ROWS + k], 1), :]

    def x_copy(ex, k):
        return pltpu.make_async_copy(tok_row(h_hbm, ex, k), xf_ref.at[pl.ds(k, 1), :], sem_x)

    def put_copy(ex, k):
        return pltpu.make_async_copy(rmw_ref.at[pl.ds(k, 1), :], tok_row(y_hbm, ex, k), sem_put)

    def get_copy(ex, k):
        return pltpu.make_async_copy(tok_row(y_hbm, ex, k), rmw_ref.at[pl.ds(k, 1), :], sem_get)

    def fill_copy(c):
        return pltpu.make_async_copy(rmw_ref, y_hbm.at[pl.ds(c * MOE_ROWS, MOE_ROWS), :], sem_fill)

    def start_rows(copy, ex):
        for k in range(MOE_ROWS):
            copy(ex, k).start(priority=k % 2)

    def wait_rows(sem, src, dst):
        pltpu.make_async_copy(src, dst, sem).wait()

    y_rows = y_hbm.at[pl.ds(0, MOE_ROWS), :]
    h_rows = h_hbm.at[pl.ds(0, MOE_ROWS), :]

    def ffn_step(first=False):
        x = xb_ref[...]
        g = jnp.dot(x, wg_ref[...].astype(BF16), preferred_element_type=F32)
        up = jnp.dot(x, wu_ref[...].astype(BF16), preferred_element_type=F32)
        hid = (_silu(g) * up).astype(BF16)
        wd = wd_ref[...].astype(BF16)
        for c in range(D // MOE_TN):
            cols = slice(c * MOE_TN, (c + 1) * MOE_TN)
            part = jnp.dot(hid, wd[:, cols], preferred_element_type=F32)
            if first:
                acc_ref[:, cols] = part
            else:
                acc_ref[:, cols] += part

    @pl.when(jnp.logical_and(e == 0, f == 0))
    def _():
        rmw_ref[...] = jnp.zeros(rmw_ref.shape, F32)
        for c in range(N_FILL):
            fill_copy(c).start()

        def issue(k, c):
            x_copy(0, k).start()
            return c
        lax.fori_loop(0, MOE_ROWS, issue, 0)

    @pl.when(f == 0)
    def _():
        wait_rows(sem_x, h_rows, xf_ref)
        xb_ref[...] = xf_ref[...].astype(BF16)

    @pl.when(jnp.logical_and(e > 0, f == 0))
    def _():
        start_rows(put_copy, e - 1)
        ffn_step(first=True)

    @pl.when(jnp.logical_and(e == 0, f == 0))
    def _():
        ffn_step(first=True)

    @pl.when(f == 1)
    def _():
        start_rows(x_copy, e_next)
        ffn_step()

    @pl.when(jnp.logical_and(e > 0, f == 2))
    def _():
        wait_rows(sem_put, rmw_ref, y_rows)
        start_rows(get_copy, e)
        ffn_step()

    @pl.when(jnp.logical_and(e == 0, f == 2))
    def _():
        ffn_step()

    @pl.when(jnp.logical_and(e > 0, f == 3))
    def _():
        ffn_step()

    @pl.when(jnp.logical_and(e == 0, f == 3))
    def _():
        for c in range(N_FILL):
            fill_copy(c).wait()
        start_rows(get_copy, 0)
        ffn_step()

    @pl.when(f == 3)
    def _():
        wait_rows(sem_get, y_rows, rmw_ref)
        rmw_ref[...] = rmw_ref[...] + gate_ref[...] * acc_ref[...]

    @pl.when(jnp.logical_and(e == N_EXP - 1, f == MOE_NF - 1))
    def _():
        def issue(k, c):
            put_copy(N_EXP - 1, k).start()
            return c
        lax.fori_loop(0, MOE_ROWS, issue, 0)
        wait_rows(sem_put, rmw_ref, y_rows)
        wait_rows(sem_x, h_rows, xf_ref)


def _moe(rows, h2, gates, w_g, w_u, w_d):
    return pl.pallas_call(
        _moe_kernel,
        grid_spec=pltpu.PrefetchScalarGridSpec(
            num_scalar_prefetch=1,
            grid=(N_EXP, MOE_NF),
            in_specs=[pl.BlockSpec(memory_space=pl.ANY),
                      pl.BlockSpec((None, MOE_ROWS, 1), lambda e, f, s: (e, 0, 0)),
                      pl.BlockSpec((None, D, MOE_TF), lambda e, f, s: (e, 0, f)),
                      pl.BlockSpec((None, D, MOE_TF), lambda e, f, s: (e, 0, f)),
                      pl.BlockSpec((None, MOE_TF, D), lambda e, f, s: (e, f, 0))],
            out_specs=pl.BlockSpec(memory_space=pl.ANY),
            scratch_shapes=[pltpu.VMEM((MOE_ROWS, D), F32),
                            pltpu.VMEM((MOE_ROWS, D), BF16),
                            pltpu.VMEM((MOE_ROWS, D), F32),
                            pltpu.VMEM((MOE_ROWS, D), F32),
                            pltpu.SemaphoreType.DMA(()),
                            pltpu.SemaphoreType.DMA(()),
                            pltpu.SemaphoreType.DMA(()),
                            pltpu.SemaphoreType.DMA(())]),
        out_shape=jax.ShapeDtypeStruct((N_TOK, D), F32),
        compiler_params=_params(("arbitrary", "arbitrary")),
        name="moe",
    )(rows, h2, gates, w_g, w_u, w_d)


def _final_kernel(x1_ref, y_ref, mod_ref, g_ref, o_ref):
    o_ref[...] = _rms(x1_ref[...] + mod_ref[0][5:6, :] * y_ref[...], g_ref[...])


def _final(x1, y, tok0, n_tok, mod, fng):
    tm = 512
    blk0 = tok0 // tm
    return pl.pallas_call(
        _final_kernel,
        grid=(n_tok // tm,),
        in_specs=[pl.BlockSpec((tm, D), lambda i: (blk0 + i, 0)),
                  pl.BlockSpec((tm, D), lambda i: (blk0 + i, 0)),
                  pl.BlockSpec((1, N_MOD, D), lambda i: (_mod_row(blk0 + i, tm), 0, 0)),
                  pl.BlockSpec((1, D), lambda i: (0, 0))],
        out_specs=pl.BlockSpec((tm, D), lambda i: (i, 0)),
        out_shape=jax.ShapeDtypeStruct((n_tok, D), F32),
        compiler_params=_params(("arbitrary",)),
        name="final",
    )(x1, y, mod, fng)


def kernel(x_prompt, x_sample, c, cache_k, cache_v, c_ctx, w_ada, b_ada, norm1_g, w_in, conv_dw, conv_dw_b, conv_ln_g, conv_ln_b, w_conv_out, attn_sink, w_attn_out, w_out, norm2_g, w_router, b_router, w_exp_gate, w_exp_up, w_exp_down, final_norm_g):
    x_ctx = x_prompt.reshape(N_CTX, D)
    x_lat = x_sample.reshape(N_LAT, D)
    cond = jnp.concatenate([c, c_ctx[None, :], jnp.zeros((MOD_ROWS - N_LAT_SEQ - 1, D), F32)], axis=0)
    mod = _ada(cond, w_ada[0], b_ada[0]).reshape(MOD_ROWS, N_MOD, D)

    u = _inproj(_prenorm(x_ctx, x_lat, mod, norm1_g), w_in[0])
    zc = _conv(u, conv_dw[0], conv_dw_b, conv_ln_g, conv_ln_b)

    sink = attn_sink[0]
    cos, sin = _rope_tables()
    ck = cache_k[:, 0].reshape(N_LAT_SEQ, PAST, N_KV * HD)
    cv = cache_v[:, 0].reshape(N_LAT_SEQ, PAST, N_KV * HD)
    att_ctx, state_k, state_v = _ctx_attn(u, sink)
    mixed = _mix(zc, att_ctx, _lat_attn(u, sink, ck, cv, cos, sin), u,
                 w_conv_out[0], w_attn_out[0])

    x1, h2, logits = _outproj(mixed, x_ctx, x_lat, mod, norm2_g, w_out[0].astype(BF16),
                              w_router[0], b_router)

    idx, gates = _select(logits.T)
    rows = (idx + (jnp.arange(N_SETS, dtype=jnp.int32) * N_CTX)[:, None, None])
    rows = rows.transpose(1, 0, 2).reshape(N_EXP * MOE_ROWS)
    ffn = _moe(rows, h2, gates, w_exp_gate[0], w_exp_up[0], w_exp_down[0])

    fng = final_norm_g.reshape(1, D)
    y_prompt = _final(x1, ffn, 0, N_CTX, mod, fng).reshape(N_CTX_SEQ, CTX_LEN, D)
    y_sample = _final(x1, ffn, N_CTX, N_LAT, mod, fng).reshape(N_LAT_SEQ, LAT_LEN, D)
    return (y_prompt, y_sample, state_k, state_v)
```

```python
import functools

import jax
import jax.numpy as jnp
from jax import lax
from jax.experimental import pallas as pl
from jax.experimental.pallas import tpu as pltpu

D = 2048
N_CTX_SEQ, CTX_LEN = 16, 256
N_LAT_SEQ, LAT_LEN = 4, 1024
N_CTX = N_CTX_SEQ * CTX_LEN
N_LAT = N_LAT_SEQ * LAT_LEN
N_TOK = N_CTX + N_LAT
PAST = 512
GRID_W = 64
N_HEADS, N_KV, HD = 16, 4, 128
Q_GROUP = N_HEADS // N_KV
WINDOW = 128
ABLK = 128
CONV_W = 1024
CONV_K = 31
CONV_PAD = CONV_K // 2
N_EXP = 16
D_EXP = 1024
CAP = 2 * N_CTX // N_EXP
N_SETS = 2
ROPE_BASE = 10000.0
EPS = 1e-6
N_MOD = 6
NEG = -1e30
IN_COLS = 2 * CONV_W + N_HEADS * HD + 2 * N_KV * HD + 2 * D
COL_Q = 2 * CONV_W
COL_K = COL_Q + N_HEADS * HD
COL_V = COL_K + N_KV * HD
COL_GA = COL_V + N_KV * HD
COL_GB = COL_GA + D
MOD_ROWS = 8
CTX_MOD_ROW = N_LAT_SEQ

VMEM_LIMIT = 58 * 1024 * 1024
F32 = jnp.float32
BF16 = jnp.bfloat16


def _params(sem, vmem=VMEM_LIMIT):
    return pltpu.CompilerParams(dimension_semantics=sem, vmem_limit_bytes=vmem)


def _mod_row(tok_block, block_tokens):
    ctx_blocks = N_CTX // block_tokens
    per_seq = LAT_LEN // block_tokens
    return jnp.where(tok_block < ctx_blocks, CTX_MOD_ROW, (tok_block - ctx_blocks) // per_seq)


def _silu(x):
    return x * jax.nn.sigmoid(x)


def _rms(x, g):
    return x * lax.rsqrt(jnp.mean(x * x, axis=-1, keepdims=True) + EPS) * g


def _ada_kernel(cond_ref, w_ref, b_ref, o_ref):
    a = _silu(cond_ref[...]).astype(BF16)
    o_ref[...] = jnp.dot(a, w_ref[...].astype(BF16), preferred_element_type=F32) + b_ref[...]


def _ada(cond, w_ada, b_ada):
    tn = 1024
    n = N_MOD * D
    return pl.pallas_call(
        _ada_kernel,
        grid=(n // tn,),
        in_specs=[pl.BlockSpec((MOD_ROWS, D), lambda j: (0, 0)),
                  pl.BlockSpec((D, tn), lambda j: (0, j)),
                  pl.BlockSpec((1, tn), lambda j: (0, j))],
        out_specs=pl.BlockSpec((MOD_ROWS, tn), lambda j: (0, j)),
        out_shape=jax.ShapeDtypeStruct((MOD_ROWS, n), F32),
        compiler_params=_params(("arbitrary",)),
        name="ada",
    )(cond, w_ada, b_ada.reshape(1, n))


def _set_specs(tm, buffers):
    nc = N_CTX // tm
    mode = pl.Buffered(buffers)
    ctx = pl.BlockSpec((tm, D), lambda i, *_: (jnp.minimum(i, nc - 1), 0), pipeline_mode=mode)
    lat = pl.BlockSpec((tm, D), lambda i, *_: (jnp.maximum(i - nc, 0), 0), pipeline_mode=mode)
    return ctx, lat


def _prenorm_kernel(xc_ref, xl_ref, mod_ref, g_ref, h_ref):
    is_ctx = pl.program_id(0) < N_CTX // xc_ref.shape[0]
    x = jnp.where(is_ctx, xc_ref[...], xl_ref[...])
    m = mod_ref[0]
    h = _rms(x, g_ref[...]) * (1.0 + m[1:2, :]) + m[0:1, :]
    h_ref[...] = h.astype(BF16)


def _prenorm(x_ctx, x_lat, mod, n1g):
    tm = 512
    xc_spec, xl_spec = _set_specs(tm, 2)
    return pl.pallas_call(
        _prenorm_kernel,
        grid=(N_TOK // tm,),
        in_specs=[xc_spec, xl_spec,
                  pl.BlockSpec((1, N_MOD, D), lambda i: (_mod_row(i, tm), 0, 0)),
                  pl.BlockSpec((1, D), lambda i: (0, 0))],
        out_specs=pl.BlockSpec((tm, D), lambda i: (i, 0)),
        out_shape=jax.ShapeDtypeStruct((N_TOK, D), BF16),
        compiler_params=_params(("arbitrary",)),
        name="prenorm",
    )(x_ctx, x_lat, mod, n1g)


def _inproj_kernel(h_ref, w_ref, o_ref, wb_ref):
    @pl.when(pl.program_id(1) == 0)
    def _():
        wb_ref[...] = w_ref[...].astype(BF16)

    o_ref[...] = jnp.dot(h_ref[...], wb_ref[...], preferred_element_type=F32)


def _inproj(h, w_in):
    tm, tn = 1024, 1536
    return pl.pallas_call(
        _inproj_kernel,
        grid=(IN_COLS // tn, N_TOK // tm),
        in_specs=[pl.BlockSpec((tm, D), lambda j, i: (i, 0)),
                  pl.BlockSpec((D, tn), lambda j, i: (0, j))],
        out_specs=pl.BlockSpec((tm, tn), lambda j, i: (i, j)),
        out_shape=jax.ShapeDtypeStruct((N_TOK, IN_COLS), F32),
        scratch_shapes=[pltpu.VMEM((D, tn), BF16)],
        compiler_params=_params(("arbitrary", "arbitrary")),
        name="inproj",
    )(h, w_in)


CONV_T = 256
CONV_HALO = 16
CONV_ROWS = 32
SUBLANES = 8
LANES = 128
CONV_ZS = CONV_T + (CONV_K // SUBLANES) * SUBLANES


def _conv_kernel(main_ref, prev_ref, next_ref, w_ref, b_ref, lg_ref, lb_ref, o_ref,
                 z_ref, zs_ref, y_ref):
    i = pl.program_id(0)
    ctx_blocks = N_CTX // CONV_T
    per_seq = LAT_LEN // CONV_T
    pos = (i - ctx_blocks) % per_seq
    is_lat = i >= ctx_blocks
    has_prev = jnp.logical_and(is_lat, pos > 0)
    has_next = jnp.logical_and(is_lat, pos < per_seq - 1)

    def glu(ref):
        v = ref[...]
        return v[:, :CONV_W] * jax.nn.sigmoid(v[:, CONV_W:])

    z_ref[0:CONV_HALO, :] = jnp.where(has_prev, glu(prev_ref), 0.0)
    z_ref[CONV_HALO:CONV_HALO + CONV_T, :] = glu(main_ref)
    z_ref[CONV_HALO + CONV_T:, :] = jnp.where(has_next, glu(next_ref), 0.0)

    z = z_ref[...]
    zs_ref[0] = z[:CONV_ZS]
    for s in range(1, SUBLANES):
        zs_ref[s] = pltpu.roll(z, z.shape[0] - s, axis=0)[:CONV_ZS]

    def channel_group(c, carry):
        cols = pl.ds(pl.multiple_of(c * LANES, LANES), LANES)
        acc = jnp.zeros((CONV_T, LANES), F32)
        for j in range(CONV_K):
            a, s = divmod(CONV_HALO - CONV_PAD + j, SUBLANES)
            acc = acc + zs_ref[s, a * SUBLANES:a * SUBLANES + CONV_T, cols] * w_ref[j:j + 1, cols]
        y_ref[:, cols] = acc + b_ref[:, cols]
        return carry

    lax.fori_loop(0, CONV_W // LANES, channel_group, 0)

    for r in range(CONV_T // CONV_ROWS):
        acc = y_ref[r * CONV_ROWS:(r + 1) * CONV_ROWS, :]
        mu = jnp.mean(acc, axis=-1, keepdims=True)
        cen = acc - mu
        var = jnp.mean(cen * cen, axis=-1, keepdims=True)
        y = cen * lax.rsqrt(var + EPS) * lg_ref[...] + lb_ref[...]
        o_ref[r * CONV_ROWS:(r + 1) * CONV_ROWS, :] = _silu(y).astype(BF16)


def _conv(u, w_dw, b_dw, ln_g, ln_b):
    nblk = N_TOK // CONV_T
    hpb = CONV_T // CONV_HALO
    last = N_TOK // CONV_HALO - 1
    row = lambda i: (0, 0)
    return pl.pallas_call(
        _conv_kernel,
        grid=(nblk,),
        in_specs=[pl.BlockSpec((CONV_T, 2 * CONV_W), lambda i: (i, 0)),
                  pl.BlockSpec((CONV_HALO, 2 * CONV_W), lambda i: (jnp.maximum(i * hpb - 1, 0), 0)),
                  pl.BlockSpec((CONV_HALO, 2 * CONV_W), lambda i: (jnp.minimum((i + 1) * hpb, last), 0)),
                  pl.BlockSpec((CONV_K, CONV_W), row),
                  pl.BlockSpec((1, CONV_W), row),
                  pl.BlockSpec((1, CONV_W), row),
                  pl.BlockSpec((1, CONV_W), row)],
        out_specs=pl.BlockSpec((CONV_T, CONV_W), lambda i: (i, 0)),
        out_shape=jax.ShapeDtypeStruct((N_TOK, CONV_W), BF16),
        scratch_shapes=[pltpu.VMEM((CONV_T + 2 * CONV_HALO, CONV_W), F32),
                        pltpu.VMEM((SUBLANES, CONV_ZS, CONV_W), F32),
                        pltpu.VMEM((CONV_T, CONV_W), F32)],
        compiler_params=_params(("arbitrary",)),
        name="conv",
    )(u, u, u, w_dw, b_dw, ln_g, ln_b)


def _stack_heads(q):
    return jnp.concatenate([q[:, g * HD:(g + 1) * HD] for g in range(Q_GROUP)], axis=0)


def _unstack_heads(o, rows):
    return jnp.concatenate([o[g * rows:(g + 1) * rows, :] for g in range(Q_GROUP)], axis=1)


LOG2E = 1.4426950408889634
Q_SCALE = HD ** -0.5 * LOG2E
QW = Q_GROUP * HD


def _sink_column(sink_ref, kv, rows):
    rid = lax.broadcasted_iota(jnp.int32, (Q_GROUP * rows, 1), 0)
    col = jnp.full((Q_GROUP * rows, 1), sink_ref[kv * Q_GROUP], F32)
    for g in range(1, Q_GROUP):
        col = jnp.where(rid >= g * rows, sink_ref[kv * Q_GROUP + g], col)
    return col * LOG2E


def _with_ones(vals):
    return jnp.concatenate([vals, jnp.ones(vals.shape, vals.dtype)], axis=1)


def _softmax_av(s, sk, vals):
    m = jnp.maximum(jnp.max(s, axis=-1, keepdims=True), sk)
    if vals.shape[1] == 2 * HD:
        p = jnp.exp2(s - m).astype(BF16)
        o = jnp.dot(p, vals, preferred_element_type=F32)
        return o[:, :HD] / (o[:, HD:HD + 1] + jnp.exp2(sk - m))
    p = jnp.exp2(s - m)
    den = jnp.sum(p, axis=-1, keepdims=True) + jnp.exp2(sk - m)
    return jnp.dot(p.astype(BF16), vals, preferred_element_type=F32) / den


_NT = (((1,), (1,)), ((), ()))


def _ctx_attn_kernel(sink_ref, q_ref, k_ref, v_ref, o_ref, sk_ref, sv_ref):
    for kv in range(N_KV):
        k = k_ref[:, kv * HD:(kv + 1) * HD]
        v = v_ref[:, kv * HD:(kv + 1) * HD]
        sk_ref[0, 0, :, kv, :] = k
        sv_ref[0, 0, :, kv, :] = v
        q = (_stack_heads(q_ref[:, kv * QW:(kv + 1) * QW]) * Q_SCALE).astype(BF16)
        s = lax.dot_general(q, k.astype(BF16), _NT, preferred_element_type=F32)
        o = _softmax_av(s, _sink_column(sink_ref, kv, CTX_LEN), v.astype(BF16))
        o_ref[:, kv * QW:(kv + 1) * QW] = _unstack_heads(o, CTX_LEN).astype(BF16)


def _ctx_attn(u, sink):
    qcols, kcols = N_HEADS * HD, N_KV * HD
    state_spec = pl.BlockSpec((1, 1, CTX_LEN, N_KV, HD), lambda b, s: (b, 0, 0, 0, 0))
    state_shape = jax.ShapeDtypeStruct((N_CTX_SEQ, 1, CTX_LEN, N_KV, HD), F32)
    return pl.pallas_call(
        _ctx_attn_kernel,
        grid_spec=pltpu.PrefetchScalarGridSpec(
            num_scalar_prefetch=1,
            grid=(N_CTX_SEQ,),
            in_specs=[pl.BlockSpec((CTX_LEN, qcols), lambda b, s: (b, COL_Q // qcols)),
                      pl.BlockSpec((CTX_LEN, kcols), lambda b, s: (b, COL_K // kcols)),
                      pl.BlockSpec((CTX_LEN, kcols), lambda b, s: (b, COL_V // kcols))],
            out_specs=[pl.BlockSpec((CTX_LEN, qcols), lambda b, s: (b, 0)),
                       state_spec, state_spec]),
        out_shape=[jax.ShapeDtypeStruct((N_CTX, N_HEADS * HD), BF16), state_shape, state_shape],
        compiler_params=_params(("arbitrary",)),
        name="ctx_attn",
    )(sink, u, u, u)


def _rope(x, cos, sin):
    lane = lax.broadcasted_iota(jnp.int32, x.shape, 1)
    fwd = pltpu.roll(x, HD - 32, axis=1)
    bwd = pltpu.roll(x, 32, axis=1)
    partner = jnp.where((lane & 32) == 0, fwd, bwd)
    return x * cos + partner * sin


def _lat_attn_kernel(sink_ref, q_ref, k_ref, v_ref, ck_ref, cv_ref, cos_ref, sin_ref, o_ref):
    kv = pl.program_id(1)
    nblk = LAT_LEN // ABLK
    cos, sin = cos_ref[...], sin_ref[...]
    k_rot = _rope(k_ref[...], cos, sin).astype(BF16)
    v = _with_ones(v_ref[...].astype(BF16))
    ck = ck_ref[:, kv, :].astype(BF16)
    cv = _with_ones(cv_ref[:, kv, :].astype(BF16))
    sk = _sink_column(sink_ref, kv, ABLK)
    r = lax.broadcasted_iota(jnp.int32, (Q_GROUP * ABLK, ABLK), 0) & (ABLK - 1)
    c = lax.broadcasted_iota(jnp.int32, (Q_GROUP * ABLK, ABLK), 1)
    open_tile = jnp.zeros((Q_GROUP * ABLK, ABLK), F32)
    below = jnp.where(c >= r, 0.0, NEG)
    above = jnp.where(c <= r, 0.0, NEG)
    ctx_open = jnp.zeros((Q_GROUP * ABLK, PAST), F32)
    bias_first = jnp.concatenate([open_tile, above, ctx_open], axis=1)
    bias_mid = jnp.concatenate([below, open_tile, above, ctx_open], axis=1)
    bias_last = jnp.concatenate([below, open_tile, ctx_open], axis=1)
    for j in range(nblk):
        rows = slice(j * ABLK, (j + 1) * ABLK)
        cos_j, sin_j = cos[rows], sin[rows]
        qs = jnp.concatenate(
            [_rope(q_ref[rows, g * HD:(g + 1) * HD], cos_j, sin_j) * Q_SCALE for g in range(Q_GROUP)],
            axis=0).astype(BF16)
        blocks = range(max(j - 1, 0), min(j + 2, nblk))
        band = slice(blocks[0] * ABLK, (blocks[-1] + 1) * ABLK)
        keys = jnp.concatenate([k_rot[band], ck], axis=0)
        vals = jnp.concatenate([v[band], cv], axis=0)
        bias = bias_first if j == 0 else bias_last if j == nblk - 1 else bias_mid
        s = lax.dot_general(qs, keys, _NT, preferred_element_type=F32) + bias
        o = _softmax_av(s, sk, vals)
        o_ref[rows, :] = _unstack_heads(o, ABLK).astype(BF16)


def _lat_attn(u, sink, ck, cv, cos, sin):
    base = N_CTX // LAT_LEN
    cache_spec = pl.BlockSpec((None, None, PAST, N_KV, HD), lambda b, kv, s: (b, 0, 0, 0, 0))
    table_spec = pl.BlockSpec((LAT_LEN, HD), lambda b, kv, s: (0, 0))
    return pl.pallas_call(
        _lat_attn_kernel,
        grid_spec=pltpu.PrefetchScalarGridSpec(
            num_scalar_prefetch=1,
            grid=(N_LAT_SEQ, N_KV),
            in_specs=[pl.BlockSpec((LAT_LEN, QW), lambda b, kv, s: (base + b, COL_Q // QW + kv)),
                      pl.BlockSpec((LAT_LEN, HD), lambda b, kv, s: (base + b, COL_K // HD + kv)),
                      pl.BlockSpec((LAT_LEN, HD), lambda b, kv, s: (base + b, COL_V // HD + kv)),
                      cache_spec, cache_spec, table_spec, table_spec],
            out_specs=pl.BlockSpec((LAT_LEN, QW), lambda b, kv, s: (b, kv))),
        out_shape=jax.ShapeDtypeStruct((N_LAT, N_HEADS * HD), BF16),
        compiler_params=_params(("arbitrary", "arbitrary")),
        name="lat_attn",
    )(sink, u, u, u, ck, cv, cos, sin)


def _rope_tables():
    n = jnp.arange(LAT_LEN)
    row = (n // GRID_W).astype(F32)
    col = (n % GRID_W).astype(F32)
    quarter = HD // 4
    inv = ROPE_BASE ** (-jnp.arange(quarter, dtype=F32) / quarter)
    ang_r = row[:, None] * inv[None, :]
    ang_c = col[:, None] * inv[None, :]
    cos = jnp.concatenate([jnp.cos(ang_r)] * 2 + [jnp.cos(ang_c)] * 2, axis=-1)
    sin = jnp.concatenate([-jnp.sin(ang_r), jnp.sin(ang_r), -jnp.sin(ang_c), jnp.sin(ang_c)], axis=-1)
    return cos, sin


def _mix_kernel(z_ref, ac_ref, al_ref, ga_ref, gb_ref, wc_ref, wa_ref, o_ref, wcb_ref, wab_ref):
    @pl.when(pl.program_id(1) == 0)
    def _():
        wcb_ref[...] = wc_ref[...].astype(BF16)
        wab_ref[...] = wa_ref[...].astype(BF16)

    is_ctx = pl.program_id(1) < N_CTX // z_ref.shape[0]
    att = jnp.where(is_ctx, ac_ref[...], al_ref[...])
    conv_out = jnp.dot(z_ref[...], wcb_ref[...], preferred_element_type=F32)
    attn_out = jnp.dot(att, wab_ref[...], preferred_element_type=F32)
    mixed = jax.nn.sigmoid(ga_ref[...]) * conv_out + jax.nn.sigmoid(gb_ref[...]) * attn_out
    o_ref[...] = mixed.astype(BF16)


def _mix(zc, att_ctx, att_lat, u, w_conv_out, w_attn_out):
    tm, tn = 512, 1024
    nc = N_CTX // tm
    return pl.pallas_call(
        _mix_kernel,
        grid=(D // tn, N_TOK // tm),
        in_specs=[pl.BlockSpec((tm, CONV_W), lambda j, i: (i, 0)),
                  pl.BlockSpec((tm, N_HEADS * HD), lambda j, i: (jnp.minimum(i, nc - 1), 0)),
                  pl.BlockSpec((tm, N_HEADS * HD), lambda j, i: (jnp.maximum(i - nc, 0), 0)),
                  pl.BlockSpec((tm, tn), lambda j, i: (i, COL_GA // tn + j)),
                  pl.BlockSpec((tm, tn), lambda j, i: (i, COL_GB // tn + j)),
                  pl.BlockSpec((CONV_W, tn), lambda j, i: (0, j)),
                  pl.BlockSpec((N_HEADS * HD, tn), lambda j, i: (0, j))],
        out_specs=pl.BlockSpec((tm, tn), lambda j, i: (i, j)),
        out_shape=jax.ShapeDtypeStruct((N_TOK, D), BF16),
        scratch_shapes=[pltpu.VMEM((CONV_W, tn), BF16), pltpu.VMEM((N_HEADS * HD, tn), BF16)],
        compiler_params=_params(("arbitrary", "arbitrary")),
        name="mix",
    )(zc, att_ctx, att_lat, u, u, w_conv_out, w_attn_out)


def _split_bf16(x):
    hi = x.astype(BF16)
    lo = (x - hi.astype(F32)).astype(BF16)
    return hi, lo


def _outproj_kernel(mx_ref, xc_ref, xl_ref, mod_ref, g_ref, w_ref, wr_ref, br_ref,
                    x1_ref, h2_ref, lg_ref):
    is_ctx = pl.program_id(0) < N_CTX // xc_ref.shape[0]
    x = jnp.where(is_ctx, xc_ref[...], xl_ref[...])
    m = mod_ref[0]
    x1 = x + m[2:3, :] * jnp.dot(mx_ref[...], w_ref[...], preferred_element_type=F32)
    x1_ref[...] = x1
    h2 = _rms(x1, g_ref[...]) * (1.0 + m[4:5, :]) + m[3:4, :]
    h2_ref[...] = h2
    h_hi, h_lo = _split_bf16(h2)
    w_hi, w_lo = _split_bf16(wr_ref[...])
    lg = (jnp.dot(h_hi, w_hi, preferred_element_type=F32)
          + jnp.dot(h_hi, w_lo, preferred_element_type=F32)
          + jnp.dot(h_lo, w_hi, preferred_element_type=F32))
    lg_ref[...] = lg + br_ref[...]


def _outproj(mixed, x_ctx, x_lat, mod, n2g, w_out_bf, w_router, b_router):
    tm = 512
    const = lambda i: (0, 0)
    xc_spec, xl_spec = _set_specs(tm, 2)
    return pl.pallas_call(
        _outproj_kernel,
        grid=(N_TOK // tm,),
        in_specs=[pl.BlockSpec((tm, D), lambda i: (i, 0)),
                  xc_spec, xl_spec,
                  pl.BlockSpec((1, N_MOD, D), lambda i: (_mod_row(i, tm), 0, 0)),
                  pl.BlockSpec((1, D), const),
                  pl.BlockSpec((D, D), const, pipeline_mode=pl.Buffered(1)),
                  pl.BlockSpec((D, N_EXP), const),
                  pl.BlockSpec((1, N_EXP), const)],
        out_specs=[pl.BlockSpec((tm, D), lambda i: (i, 0)),
                   pl.BlockSpec((tm, D), lambda i: (i, 0)),
                   pl.BlockSpec((tm, N_EXP), lambda i: (i, 0))],
        out_shape=[jax.ShapeDtypeStruct((N_TOK, D), F32),
                   jax.ShapeDtypeStruct((N_TOK, D), F32),
                   jax.ShapeDtypeStruct((N_TOK, N_EXP), F32)],
        compiler_params=_params(("arbitrary",)),
        name="outproj",
    )(mixed, x_ctx, x_lat, mod, n2g, w_out_bf, w_router, b_router)


SEL_CHUNK = 128


def _prefix_sum(flags, out_ref):
    n = flags.shape[1]
    ri = lax.broadcasted_iota(jnp.int32, (SEL_CHUNK, SEL_CHUNK), 0)
    ci = lax.broadcasted_iota(jnp.int32, (SEL_CHUNK, SEL_CHUNK), 1)
    tri = jnp.where(ri <= ci, 1.0, 0.0).astype(BF16)
    run = jnp.zeros((N_EXP, 1), F32)
    for ch in range(n // SEL_CHUNK):
        blk = flags[:, ch * SEL_CHUNK:(ch + 1) * SEL_CHUNK].astype(BF16)
        within = jnp.dot(blk, tri, preferred_element_type=F32) + run
        out_ref[:, ch * SEL_CHUNK:(ch + 1) * SEL_CHUNK] = within
        run = within[:, SEL_CHUNK - 1:SEL_CHUNK]


def _select_kernel(lg_ref, idx_ref, gate_ref, cs_ref):
    n = lg_ref.shape[1]
    lg = lg_ref[...]
    ex = jnp.exp(lg - jnp.max(lg, axis=0, keepdims=True))
    aff = ex / jnp.sum(ex, axis=0, keepdims=True)

    def bit_step(b, cur):
        cand = cur | (1 << (30 - b))
        cnt = jnp.sum(jnp.where(aff >= pltpu.bitcast(cand, F32), 1.0, 0.0), axis=1, keepdims=True)
        return jnp.where(cnt >= CAP, cand, cur)

    thr = pltpu.bitcast(lax.fori_loop(0, 31, bit_step, jnp.zeros((N_EXP, 1), jnp.int32)), F32)
    above = aff > thr
    tied = aff == thr
    room = CAP - jnp.sum(jnp.where(above, 1.0, 0.0), axis=1, keepdims=True)
    _prefix_sum(jnp.where(tied, 1.0, 0.0), cs_ref)
    sel = jnp.logical_or(above, jnp.logical_and(tied, cs_ref[...] <= room))
    gsel = jnp.where(sel, aff, 0.0)
    _prefix_sum(jnp.where(sel, 1.0, 0.0), cs_ref)

    slot = lax.broadcasted_iota(jnp.int32, (CAP, n), 0).astype(F32)
    g_hi = gsel.astype(BF16).astype(F32)
    rest = gsel - g_hi
    g_mid = rest.astype(BF16).astype(F32)
    g_lo = rest - g_mid
    ones = jnp.ones((1, n), F32)
    pad = jnp.zeros((4, n), F32)
    last_slot = lax.broadcasted_iota(jnp.int32, (1, CAP), 1) == CAP - 1
    on_diag = (lax.broadcasted_iota(jnp.int32, (CAP, CAP), 0)
               == lax.broadcasted_iota(jnp.int32, (CAP, CAP), 1))
    total = jnp.sum(gsel, axis=1, keepdims=True)
    for e in range(N_EXP):
        row = slice(e, e + 1)
        le = jnp.where(cs_ref[row, :] <= slot, 1.0, 0.0).astype(BF16)
        lhs = jnp.concatenate([ones, g_hi[row], g_mid[row], g_lo[row], pad], axis=0).astype(BF16)
        res = lax.dot_general(lhs, le, (((1,), (1,)), ((), ())), preferred_element_type=F32)
        idx_ref[0, row, :] = res[0:1, :].astype(jnp.int32)
        cum = res[1:2, :] + res[2:3, :] + res[3:4, :]
        gate = jnp.where(last_slot, total[row], pltpu.roll(cum, CAP - 1, axis=1)) - cum
        gate_ref[e] = jnp.sum(jnp.where(on_diag, gate, 0.0), axis=1, keepdims=True)


def _select(logits_t):
    return pl.pallas_call(
        _select_kernel,
        grid=(N_SETS,),
        in_specs=[pl.BlockSpec((N_EXP, N_CTX), lambda s: (0, s))],
        out_specs=[pl.BlockSpec((1, N_EXP, CAP), lambda s: (s, 0, 0)),
                   pl.BlockSpec((N_EXP, CAP, 1), lambda s: (0, s, 0))],
        out_shape=[jax.ShapeDtypeStruct((N_SETS, N_EXP, CAP), jnp.int32),
                   jax.ShapeDtypeStruct((N_EXP, N_SETS * CAP, 1), F32)],
        scratch_shapes=[pltpu.VMEM((N_EXP, N_CTX), F32)],
        compiler_params=_params(("arbitrary",)),
        name="select",
    )(logits_t)


MOE_ROWS = N_SETS * CAP
MOE_TF = 256
MOE_NF = D_EXP // MOE_TF
MOE_TN = 512
N_FILL = N_TOK // MOE_ROWS
assert MOE_NF == 4


def _moe_kernel(rows_ref, h_hbm, gate_ref, wg_ref, wu_ref, wd_ref, y_hbm,
                xf_ref, xb_ref, acc_ref, rmw_ref, sem_x, sem_put, sem_get, sem_fill):
    e = pl.program_id(0)
    f = pl.program_id(1)
    e_next = jnp.minimum(e + 1, N_EXP - 1)

    def tok_row(hbm_ref, ex, k):
        return hbm_ref.at[pl.ds(rows_ref[ex * MOE_ROWS + k], 1), :]

    def x_copy(ex, k):
        return pltpu.make_async_copy(tok_row(h_hbm, ex, k), xf_ref.at[pl.ds(k, 1), :], sem_x)

    def put_copy(ex, k):
        return pltpu.make_async_copy(rmw_ref.at[pl.ds(k, 1), :], tok_row(y_hbm, ex, k), sem_put)

    def get_copy(ex, k):
        return pltpu.make_async_copy(tok_row(y_hbm, ex, k), rmw_ref.at[pl.ds(k, 1), :], sem_get)

    def fill_copy(c):
        return pltpu.make_async_copy(rmw_ref, y_hbm.at[pl.ds(c * MOE_ROWS, MOE_ROWS), :], sem_fill)

    def start_rows(copy, ex):
        for k in range(MOE_ROWS):
            copy(ex, k).start(priority=k % 2)

    def wait_rows(sem, src, dst):
        pltpu.make_async_copy(src, dst, sem).wait()

    y_rows = y_hbm.at[pl.ds(0, MOE_ROWS), :]
    h_rows = h_hbm.at[pl.ds(0, MOE_ROWS), :]

    def ffn_step(first=False):
        x = xb_ref[...]
        g = jnp.dot(x, wg_ref[...].astype(BF16), preferred_element_type=F32)
        up = jnp.dot(x, wu_ref[...].astype(BF16), preferred_element_type=F32)
        hid = (_silu(g) * up).astype(BF16)
        wd = wd_ref[...].astype(BF16)
        for c in range(D // MOE_TN):
            cols = slice(c * MOE_TN, (c + 1) * MOE_TN)
            part = jnp.dot(hid, wd[:, cols], preferred_element_type=F32)
            if first:
                acc_ref[:, cols] = part
            else:
                acc_ref[:, cols] += part

    @pl.when(jnp.logical_and(e == 0, f == 0))
    def _():
        rmw_ref[...] = jnp.zeros(rmw_ref.shape, F32)
        for c in range(N_FILL):
            fill_copy(c).start()

        def issue(k, c):
            x_copy(0, k).start()
            return c
        lax.fori_loop(0, MOE_ROWS, issue, 0)

    @pl.when(f == 0)
    def _():
        wait_rows(sem_x, h_rows, xf_ref)
        xb_ref[...] = xf_ref[...].astype(BF16)

    @pl.when(jnp.logical_and(e > 0, f == 0))
    def _():
        start_rows(put_copy, e - 1)
        ffn_step(first=True)

    @pl.when(jnp.logical_and(e == 0, f == 0))
    def _():
        ffn_step(first=True)

    @pl.when(f == 1)
    def _():
        start_rows(x_copy, e_next)
        ffn_step()

    @pl.when(jnp.logical_and(e == 0, f == 2))
    def _():
        for c in range(N_FILL):
            fill_copy(c).wait()

    @pl.when(jnp.logical_and(e > 0, f == 2))
    def _():
        wait_rows(sem_put, rmw_ref, y_rows)

    @pl.when(f == 2)
    def _():
        start_rows(get_copy, e)
        ffn_step()

    @pl.when(f == 3)
    def _():
        ffn_step()
        wait_rows(sem_get, y_rows, rmw_ref)
        rmw_ref[...] = rmw_ref[...] + gate_ref[...] * acc_ref[...]

    @pl.when(jnp.logical_and(e == N_EXP - 1, f == MOE_NF - 1))
    def _():
        def issue(k, c):
            put_copy(N_EXP - 1, k).start()
            return c
        lax.fori_loop(0, MOE_ROWS, issue, 0)
        wait_rows(sem_put, rmw_ref, y_rows)
        wait_rows(sem_x, h_rows, xf_ref)


def _moe(rows, h2, gates, w_g, w_u, w_d):
    return pl.pallas_call(
        _moe_kernel,
        grid_spec=pltpu.PrefetchScalarGridSpec(
            num_scalar_prefetch=1,
            grid=(N_EXP, MOE_NF),
            in_specs=[pl.BlockSpec(memory_space=pl.ANY),
                      pl.BlockSpec((None, MOE_ROWS, 1), lambda e, f, s: (e, 0, 0)),
                      pl.BlockSpec((None, D, MOE_TF), lambda e, f, s: (e, 0, f)),
                      pl.BlockSpec((None, D, MOE_TF), lambda e, f, s: (e, 0, f)),
                      pl.BlockSpec((None, MOE_TF, D), lambda e, f, s: (e, f, 0))],
            out_specs=pl.BlockSpec(memory_space=pl.ANY),
            scratch_shapes=[pltpu.VMEM((MOE_ROWS, D), F32),
                            pltpu.VMEM((MOE_ROWS, D), BF16),
                            pltpu.VMEM((MOE_ROWS, D), F32),
                            pltpu.VMEM((MOE_ROWS, D), F32),
                            pltpu.SemaphoreType.DMA(()),
                            pltpu.SemaphoreType.DMA(()),
                            pltpu.SemaphoreType.DMA(()),
                            pltpu.SemaphoreType.DMA(())]),
        out_shape=jax.ShapeDtypeStruct((N_TOK, D), F32),
        compiler_params=_params(("arbitrary", "arbitrary")),
        name="moe",
    )(rows, h2, gates, w_g, w_u, w_d)


def _final_kernel(x1_ref, y_ref, mod_ref, g_ref, o_ref):
    o_ref[...] = _rms(x1_ref[...] + mod_ref[0][5:6, :] * y_ref[...], g_ref[...])


def _final(x1, y, tok0, n_tok, mod, fng):
    tm = 512
    blk0 = tok0 // tm
    return pl.pallas_call(
        _final_kernel,
        grid=(n_tok // tm,),
        in_specs=[pl.BlockSpec((tm, D), lambda i: (blk0 + i, 0)),
                  pl.BlockSpec((tm, D), lambda i: (blk0 + i, 0)),
                  pl.BlockSpec((1, N_MOD, D), lambda i: (_mod_row(blk0 + i, tm), 0, 0)),
                  pl.BlockSpec((1, D), lambda i: (0, 0))],
        out_specs=pl.BlockSpec((tm, D), lambda i: (i, 0)),
        out_shape=jax.ShapeDtypeStruct((n_tok, D), F32),
        compiler_params=_params(("arbitrary",)),
        name="final",
    )(x1, y, mod, fng)


def kernel(x_prompt, x_sample, c, cache_k, cache_v, c_ctx, w_ada, b_ada, norm1_g, w_in, conv_dw, conv_dw_b, conv_ln_g, conv_ln_b, w_conv_out, attn_sink, w_attn_out, w_out, norm2_g, w_router, b_router, w_exp_gate, w_exp_up, w_exp_down, final_norm_g):
    x_ctx = x_prompt.reshape(N_CTX, D)
    x_lat = x_sample.reshape(N_LAT, D)
    cond = jnp.concatenate([c, c_ctx[None, :], jnp.zeros((MOD_ROWS - N_LAT_SEQ - 1, D), F32)], axis=0)
    mod = _ada(cond, w_ada[0], b_ada[0]).reshape(MOD_ROWS, N_MOD, D)

    u = _inproj(_prenorm(x_ctx, x_lat, mod, norm1_g), w_in[0])
    zc = _conv(u, conv_dw[0], conv_dw_b, conv_ln_g, conv_ln_b)

    sink = attn_sink[0]
    cos, sin = _rope_tables()
    att_ctx, state_k, state_v = _ctx_attn(u, sink)
    mixed = _mix(zc, att_ctx, _lat_attn(u, sink, cache_k, cache_v, cos, sin), u,
                 w_conv_out[0], w_attn_out[0])

    x1, h2, logits = _outproj(mixed, x_ctx, x_lat, mod, norm2_g, w_out[0].astype(BF16),
                              w_router[0], b_router)

    idx, gates = _select(logits.T)
    rows = (idx + (jnp.arange(N_SETS, dtype=jnp.int32) * N_CTX)[:, None, None])
    rows = rows.transpose(1, 0, 2).reshape(N_EXP * MOE_ROWS)
    ffn = _moe(rows, h2, gates, w_exp_gate[0], w_exp_up[0], w_exp_down[0])

    fng = final_norm_g.reshape(1, D)
    y_prompt = _final(x1, ffn, 0, N_CTX, mod, fng).reshape(N_CTX_SEQ, CTX_LEN, D)
    y_sample = _final(x1, ffn, N_CTX, N_LAT, mod, fng).reshape(N_LAT_SEQ, LAT_LEN, D)
    return (y_prompt, y_sample, state_k, state_v)
```

```python
import functools

import jax
import jax.numpy as jnp
from jax import lax
from jax.experimental import pallas as pl
from jax.experimental.pallas import tpu as pltpu

D = 2048
N_CTX_SEQ, CTX_LEN = 16, 256
N_LAT_SEQ, LAT_LEN = 4, 1024
N_CTX = N_CTX_SEQ * CTX_LEN
N_LAT = N_LAT_SEQ * LAT_LEN
N_TOK = N_CTX + N_LAT
PAST = 512
GRID_W = 64
N_HEADS, N_KV, HD = 16, 4, 128
Q_GROUP = N_HEADS // N_KV
WINDOW = 128
ABLK = 128
CONV_W = 1024
CONV_K = 31
CONV_PAD = CONV_K // 2
N_EXP = 16
D_EXP = 1024
CAP = 2 * N_CTX // N_EXP
N_SETS = 2
ROPE_BASE = 10000.0
EPS = 1e-6
N_MOD = 6
NEG = -1e30
IN_COLS = 2 * CONV_W + N_HEADS * HD + 2 * N_KV * HD + 2 * D
COL_Q = 2 * CONV_W
COL_K = COL_Q + N_HEADS * HD
COL_V = COL_K + N_KV * HD
COL_GA = COL_V + N_KV * HD
COL_GB = COL_GA + D
MOD_ROWS = 8
CTX_MOD_ROW = N_LAT_SEQ

VMEM_LIMIT = 58 * 1024 * 1024
F32 = jnp.float32
BF16 = jnp.bfloat16


def _params(sem, vmem=VMEM_LIMIT):
    return pltpu.CompilerParams(dimension_semantics=sem, vmem_limit_bytes=vmem)


def _mod_row(tok_block, block_tokens):
    ctx_blocks = N_CTX // block_tokens
    per_seq = LAT_LEN // block_tokens
    return jnp.where(tok_block < ctx_blocks, CTX_MOD_ROW, (tok_block - ctx_blocks) // per_seq)


def _silu(x):
    return x * jax.nn.sigmoid(x)


def _rms(x, g):
    return x * lax.rsqrt(jnp.mean(x * x, axis=-1, keepdims=True) + EPS) * g


def _ada_kernel(cond_ref, w_ref, b_ref, o_ref):
    a = _silu(cond_ref[...]).astype(BF16)
    o_ref[...] = jnp.dot(a, w_ref[...].astype(BF16), preferred_element_type=F32) + b_ref[...]


def _ada(cond, w_ada, b_ada):
    tn = 1024
    n = N_MOD * D
    return pl.pallas_call(
        _ada_kernel,
        grid=(n // tn,),
        in_specs=[pl.BlockSpec((MOD_ROWS, D), lambda j: (0, 0)),
                  pl.BlockSpec((D, tn), lambda j: (0, j)),
                  pl.BlockSpec((1, tn), lambda j: (0, j))],
        out_specs=pl.BlockSpec((MOD_ROWS, tn), lambda j: (0, j)),
        out_shape=jax.ShapeDtypeStruct((MOD_ROWS, n), F32),
        compiler_params=_params(("arbitrary",)),
        name="ada",
    )(cond, w_ada, b_ada.reshape(1, n))


def _set_specs(tm, buffers):
    nc = N_CTX // tm
    mode = pl.Buffered(buffers)
    ctx = pl.BlockSpec((tm, D), lambda i, *_: (jnp.minimum(i, nc - 1), 0), pipeline_mode=mode)
    lat = pl.BlockSpec((tm, D), lambda i, *_: (jnp.maximum(i - nc, 0), 0), pipeline_mode=mode)
    return ctx, lat


def _prenorm_kernel(xc_ref, xl_ref, mod_ref, g_ref, h_ref):
    is_ctx = pl.program_id(0) < N_CTX // xc_ref.shape[0]
    x = jnp.where(is_ctx, xc_ref[...], xl_ref[...])
    m = mod_ref[0]
    h = _rms(x, g_ref[...]) * (1.0 + m[1:2, :]) + m[0:1, :]
    h_ref[...] = h.astype(BF16)


def _prenorm(x_ctx, x_lat, mod, n1g):
    tm = 512
    xc_spec, xl_spec = _set_specs(tm, 2)
    return pl.pallas_call(
        _prenorm_kernel,
        grid=(N_TOK // tm,),
        in_specs=[xc_spec, xl_spec,
                  pl.BlockSpec((1, N_MOD, D), lambda i: (_mod_row(i, tm), 0, 0)),
                  pl.BlockSpec((1, D), lambda i: (0, 0))],
        out_specs=pl.BlockSpec((tm, D), lambda i: (i, 0)),
        out_shape=jax.ShapeDtypeStruct((N_TOK, D), BF16),
        compiler_params=_params(("arbitrary",)),
        name="prenorm",
    )(x_ctx, x_lat, mod, n1g)


def _inproj_kernel(h_ref, w_ref, o_ref, wb_ref):
    @pl.when(pl.program_id(1) == 0)
    def _():
        wb_ref[...] = w_ref[...].astype(BF16)

    o_ref[...] = jnp.dot(h_ref[...], wb_ref[...], preferred_element_type=F32)


def _inproj(h, w_in):
    tm, tn = 1024, 1536
    return pl.pallas_call(
        _inproj_kernel,
        grid=(IN_COLS // tn, N_TOK // tm),
        in_specs=[pl.BlockSpec((tm, D), lambda j, i: (i, 0)),
                  pl.BlockSpec((D, tn), lambda j, i: (0, j))],
        out_specs=pl.BlockSpec((tm, tn), lambda j, i: (i, j)),
        out_shape=jax.ShapeDtypeStruct((N_TOK, IN_COLS), F32),
        scratch_shapes=[pltpu.VMEM((D, tn), BF16)],
        compiler_params=_params(("arbitrary", "arbitrary")),
        name="inproj",
    )(h, w_in)


CONV_T = 256
CONV_HALO = 16
CONV_ROWS = 32
SUBLANES = 8
LANES = 128
CONV_ZS = CONV_T + (CONV_K // SUBLANES) * SUBLANES


def _conv_kernel(main_ref, prev_ref, next_ref, w_ref, b_ref, lg_ref, lb_ref, o_ref,
                 z_ref, zs_ref, y_ref):
    i = pl.program_id(0)
    ctx_blocks = N_CTX // CONV_T
    per_seq = LAT_LEN // CONV_T
    pos = (i - ctx_blocks) % per_seq
    is_lat = i >= ctx_blocks
    has_prev = jnp.logical_and(is_lat, pos > 0)
    has_next = jnp.logical_and(is_lat, pos < per_seq - 1)

    def glu(ref):
        v = ref[...]
        return v[:, :CONV_W] * jax.nn.sigmoid(v[:, CONV_W:])

    z_ref[0:CONV_HALO, :] = jnp.where(has_prev, glu(prev_ref), 0.0)
    z_ref[CONV_HALO:CONV_HALO + CONV_T, :] = glu(main_ref)
    z_ref[CONV_HALO + CONV_T:, :] = jnp.where(has_next, glu(next_ref), 0.0)

    z = z_ref[...]
    zs_ref[0] = z[:CONV_ZS]
    for s in range(1, SUBLANES):
        zs_ref[s] = pltpu.roll(z, z.shape[0] - s, axis=0)[:CONV_ZS]

    def channel_group(c, carry):
        cols = pl.ds(pl.multiple_of(c * LANES, LANES), LANES)
        acc = jnp.zeros((CONV_T, LANES), F32)
        for j in range(CONV_K):
            a, s = divmod(CONV_HALO - CONV_PAD + j, SUBLANES)
            acc = acc + zs_ref[s, a * SUBLANES:a * SUBLANES + CONV_T, cols] * w_ref[j:j + 1, cols]
        y_ref[:, cols] = acc + b_ref[:, cols]
        return carry

    lax.fori_loop(0, CONV_W // LANES, channel_group, 0)

    for r in range(CONV_T // CONV_ROWS):
        acc = y_ref[r * CONV_ROWS:(r + 1) * CONV_ROWS, :]
        mu = jnp.mean(acc, axis=-1, keepdims=True)
        cen = acc - mu
        var = jnp.mean(cen * cen, axis=-1, keepdims=True)
        y = cen * lax.rsqrt(var + EPS) * lg_ref[...] + lb_ref[...]
        o_ref[r * CONV_ROWS:(r + 1) * CONV_ROWS, :] = _silu(y).astype(BF16)


def _conv(u, w_dw, b_dw, ln_g, ln_b):
    nblk = N_TOK // CONV_T
    hpb = CONV_T // CONV_HALO
    last = N_TOK // CONV_HALO - 1
    row = lambda i: (0, 0)
    return pl.pallas_call(
        _conv_kernel,
        grid=(nblk,),
        in_specs=[pl.BlockSpec((CONV_T, 2 * CONV_W), lambda i: (i, 0)),
                  pl.BlockSpec((CONV_HALO, 2 * CONV_W), lambda i: (jnp.maximum(i * hpb - 1, 0), 0)),
                  pl.BlockSpec((CONV_HALO, 2 * CONV_W), lambda i: (jnp.minimum((i + 1) * hpb, last), 0)),
                  pl.BlockSpec((CONV_K, CONV_W), row),
                  pl.BlockSpec((1, CONV_W), row),
                  pl.BlockSpec((1, CONV_W), row),
                  pl.BlockSpec((1, CONV_W), row)],
        out_specs=pl.BlockSpec((CONV_T, CONV_W), lambda i: (i, 0)),
        out_shape=jax.ShapeDtypeStruct((N_TOK, CONV_W), BF16),
        scratch_shapes=[pltpu.VMEM((CONV_T + 2 * CONV_HALO, CONV_W), F32),
                        pltpu.VMEM((SUBLANES, CONV_ZS, CONV_W), F32),
                        pltpu.VMEM((CONV_T, CONV_W), F32)],
        compiler_params=_params(("arbitrary",)),
        name="conv",
    )(u, u, u, w_dw, b_dw, ln_g, ln_b)


def _stack_heads(q):
    return jnp.concatenate([q[:, g * HD:(g + 1) * HD] for g in range(Q_GROUP)], axis=0)


def _unstack_heads(o, rows):
    return jnp.concatenate([o[g * rows:(g + 1) * rows, :] for g in range(Q_GROUP)], axis=1)


LOG2E = 1.4426950408889634
Q_SCALE = HD ** -0.5 * LOG2E
QW = Q_GROUP * HD


def _sink_column(sink_ref, kv, rows):
    rid = lax.broadcasted_iota(jnp.int32, (Q_GROUP * rows, 1), 0)
    col = jnp.full((Q_GROUP * rows, 1), sink_ref[kv * Q_GROUP], F32)
    for g in range(1, Q_GROUP):
        col = jnp.where(rid >= g * rows, sink_ref[kv * Q_GROUP + g], col)
    return col * LOG2E


def _with_ones(vals):
    return jnp.concatenate([vals, jnp.ones(vals.shape, vals.dtype)], axis=1)


def _softmax_av(s, sk, vals):
    m = jnp.maximum(jnp.max(s, axis=-1, keepdims=True), sk)
    if vals.shape[1] == 2 * HD:
        p = jnp.exp2(s - m).astype(BF16)
        o = jnp.dot(p, vals, preferred_element_type=F32)
        return o[:, :HD] / (o[:, HD:HD + 1] + jnp.exp2(sk - m))
    p = jnp.exp2(s - m)
    den = jnp.sum(p, axis=-1, keepdims=True) + jnp.exp2(sk - m)
    return jnp.dot(p.astype(BF16), vals, preferred_element_type=F32) / den


_NT = (((1,), (1,)), ((), ()))


def _ctx_attn_kernel(sink_ref, q_ref, k_ref, v_ref, o_ref, sk_ref, sv_ref):
    for kv in range(N_KV):
        k = k_ref[:, kv * HD:(kv + 1) * HD]
        v = v_ref[:, kv * HD:(kv + 1) * HD]
        sk_ref[0, 0, :, kv, :] = k
        sv_ref[0, 0, :, kv, :] = v
        q = (_stack_heads(q_ref[:, kv * QW:(kv + 1) * QW]) * Q_SCALE).astype(BF16)
        s = lax.dot_general(q, k.astype(BF16), _NT, preferred_element_type=F32)
        o = _softmax_av(s, _sink_column(sink_ref, kv, CTX_LEN), v.astype(BF16))
        o_ref[:, kv * QW:(kv + 1) * QW] = _unstack_heads(o, CTX_LEN).astype(BF16)


def _ctx_attn(u, sink):
    qcols, kcols = N_HEADS * HD, N_KV * HD
    state_spec = pl.BlockSpec((1, 1, CTX_LEN, N_KV, HD), lambda b, s: (b, 0, 0, 0, 0))
    state_shape = jax.ShapeDtypeStruct((N_CTX_SEQ, 1, CTX_LEN, N_KV, HD), F32)
    return pl.pallas_call(
        _ctx_attn_kernel,
        grid_spec=pltpu.PrefetchScalarGridSpec(
            num_scalar_prefetch=1,
            grid=(N_CTX_SEQ,),
            in_specs=[pl.BlockSpec((CTX_LEN, qcols), lambda b, s: (b, COL_Q // qcols)),
                      pl.BlockSpec((CTX_LEN, kcols), lambda b, s: (b, COL_K // kcols)),
                      pl.BlockSpec((CTX_LEN, kcols), lambda b, s: (b, COL_V // kcols))],
            out_specs=[pl.BlockSpec((CTX_LEN, qcols), lambda b, s: (b, 0)),
                       state_spec, state_spec]),
        out_shape=[jax.ShapeDtypeStruct((N_CTX, N_HEADS * HD), BF16), state_shape, state_shape],
        compiler_params=_params(("arbitrary",)),
        name="ctx_attn",
    )(sink, u, u, u)


def _rope(x, cos, sin):
    lane = lax.broadcasted_iota(jnp.int32, x.shape, 1)
    fwd = pltpu.roll(x, HD - 32, axis=1)
    bwd = pltpu.roll(x, 32, axis=1)
    partner = jnp.where((lane & 32) == 0, fwd, bwd)
    return x * cos + partner * sin


def _lat_attn_kernel(sink_ref, q_ref, k_ref, v_ref, ck_ref, cv_ref, cos_ref, sin_ref, o_ref):
    kv = pl.program_id(1)
    nblk = LAT_LEN // ABLK
    cos, sin = cos_ref[...], sin_ref[...]
    k_rot = _rope(k_ref[...], cos, sin).astype(BF16)
    v = _with_ones(v_ref[...].astype(BF16))
    ck = ck_ref[:, kv, :].astype(BF16)
    cv = _with_ones(cv_ref[:, kv, :].astype(BF16))
    sk = _sink_column(sink_ref, kv, ABLK)
    r = lax.broadcasted_iota(jnp.int32, (Q_GROUP * ABLK, ABLK), 0) & (ABLK - 1)
    c = lax.broadcasted_iota(jnp.int32, (Q_GROUP * ABLK, ABLK), 1)
    open_tile = jnp.zeros((Q_GROUP * ABLK, ABLK), F32)
    below = jnp.where(c >= r, 0.0, NEG)
    above = jnp.where(c <= r, 0.0, NEG)
    ctx_open = jnp.zeros((Q_GROUP * ABLK, PAST), F32)
    bias_first = jnp.concatenate([open_tile, above, ctx_open], axis=1)
    bias_mid = jnp.concatenate([below, open_tile, above, ctx_open], axis=1)
    bias_last = jnp.concatenate([below, open_tile, ctx_open], axis=1)
    for j in range(nblk):
        rows = slice(j * ABLK, (j + 1) * ABLK)
        cos_j, sin_j = cos[rows], sin[rows]
        qs = jnp.concatenate(
            [_rope(q_ref[rows, g * HD:(g + 1) * HD], cos_j, sin_j) * Q_SCALE for g in range(Q_GROUP)],
            axis=0).astype(BF16)
        blocks = range(max(j - 1, 0), min(j + 2, nblk))
        band = slice(blocks[0] * ABLK, (blocks[-1] + 1) * ABLK)
        keys = jnp.concatenate([k_rot[band], ck], axis=0)
        vals = jnp.concatenate([v[band], cv], axis=0)
        bias = bias_first if j == 0 else bias_last if j == nblk - 1 else bias_mid
        s = lax.dot_general(qs, keys, _NT, preferred_element_type=F32) + bias
        o = _softmax_av(s, sk, vals)
        o_ref[rows, :] = _unstack_heads(o, ABLK).astype(BF16)


def _lat_attn(u, sink, ck, cv, cos, sin):
    base = N_CTX // LAT_LEN
    cache_spec = pl.BlockSpec((None, None, PAST, N_KV, HD), lambda b, kv, s: (b, 0, 0, 0, 0))
    table_spec = pl.BlockSpec((LAT_LEN, HD), lambda b, kv, s: (0, 0))
    return pl.pallas_call(
        _lat_attn_kernel,
        grid_spec=pltpu.PrefetchScalarGridSpec(
            num_scalar_prefetch=1,
            grid=(N_LAT_SEQ, N_KV),
            in_specs=[pl.BlockSpec((LAT_LEN, QW), lambda b, kv, s: (base + b, COL_Q // QW + kv)),
                      pl.BlockSpec((LAT_LEN, HD), lambda b, kv, s: (base + b, COL_K // HD + kv)),
                      pl.BlockSpec((LAT_LEN, HD), lambda b, kv, s: (base + b, COL_V // HD + kv)),
                      cache_spec, cache_spec, table_spec, table_spec],
            out_specs=pl.BlockSpec((LAT_LEN, QW), lambda b, kv, s: (b, kv))),
        out_shape=jax.ShapeDtypeStruct((N_LAT, N_HEADS * HD), BF16),
        compiler_params=_params(("arbitrary", "arbitrary")),
        name="lat_attn",
    )(sink, u, u, u, ck, cv, cos, sin)


def _rope_tables():
    n = jnp.arange(LAT_LEN)
    row = (n // GRID_W).astype(F32)
    col = (n % GRID_W).astype(F32)
    quarter = HD // 4
    inv = ROPE_BASE ** (-jnp.arange(quarter, dtype=F32) / quarter)
    ang_r = row[:, None] * inv[None, :]
    ang_c = col[:, None] * inv[None, :]
    cos = jnp.concatenate([jnp.cos(ang_r)] * 2 + [jnp.cos(ang_c)] * 2, axis=-1)
    sin = jnp.concatenate([-jnp.sin(ang_r), jnp.sin(ang_r), -jnp.sin(ang_c), jnp.sin(ang_c)], axis=-1)
    return cos, sin


def _mix_kernel(z_ref, ac_ref, al_ref, ga_ref, gb_ref, wc_ref, wa_ref, wo_ref,
                o_ref, wob_ref, wcb_ref, wab_ref):
    @pl.when(pl.program_id(1) == 0)
    def _():
        wcb_ref[...] = wc_ref[...].astype(BF16)
        wab_ref[...] = wa_ref[...].astype(BF16)

    wob_ref[...] = wo_ref[...].astype(BF16)

    is_ctx = pl.program_id(1) < N_CTX // z_ref.shape[0]
    att = jnp.where(is_ctx, ac_ref[...], al_ref[...])
    conv_out = jnp.dot(z_ref[...], wcb_ref[...], preferred_element_type=F32)
    attn_out = jnp.dot(att, wab_ref[...], preferred_element_type=F32)
    mixed = jax.nn.sigmoid(ga_ref[...]) * conv_out + jax.nn.sigmoid(gb_ref[...]) * attn_out
    o_ref[...] = mixed.astype(BF16)


def _mix(zc, att_ctx, att_lat, u, w_conv_out, w_attn_out, w_out):
    tm, tn = 512, 1024
    nc = N_CTX // tm
    ni = N_TOK // tm
    slab = D // ((D // tn) * ni)
    slab_spec = pl.BlockSpec((slab, D), lambda j, i: (j * ni + i, 0))
    return pl.pallas_call(
        _mix_kernel,
        grid=(D // tn, N_TOK // tm),
        in_specs=[pl.BlockSpec((tm, CONV_W), lambda j, i: (i, 0)),
                  pl.BlockSpec((tm, N_HEADS * HD), lambda j, i: (jnp.minimum(i, nc - 1), 0)),
                  pl.BlockSpec((tm, N_HEADS * HD), lambda j, i: (jnp.maximum(i - nc, 0), 0)),
                  pl.BlockSpec((tm, tn), lambda j, i: (i, COL_GA // tn + j)),
                  pl.BlockSpec((tm, tn), lambda j, i: (i, COL_GB // tn + j)),
                  pl.BlockSpec((CONV_W, tn), lambda j, i: (0, j)),
                  pl.BlockSpec((N_HEADS * HD, tn), lambda j, i: (0, j)),
                  slab_spec],
        out_specs=[pl.BlockSpec((tm, tn), lambda j, i: (i, j)), slab_spec],
        out_shape=[jax.ShapeDtypeStruct((N_TOK, D), BF16), jax.ShapeDtypeStruct((D, D), BF16)],
        scratch_shapes=[pltpu.VMEM((CONV_W, tn), BF16), pltpu.VMEM((N_HEADS * HD, tn), BF16)],
        compiler_params=_params(("arbitrary", "arbitrary")),
        name="mix",
    )(zc, att_ctx, att_lat, u, u, w_conv_out, w_attn_out, w_out)


def _split_bf16(x):
    hi = x.astype(BF16)
    lo = (x - hi.astype(F32)).astype(BF16)
    return hi, lo


def _outproj_kernel(mx_ref, xc_ref, xl_ref, mod_ref, g_ref, w_ref, wr_ref, br_ref,
                    x1_ref, h2_ref, lg_ref):
    is_ctx = pl.program_id(0) < N_CTX // xc_ref.shape[0]
    x = jnp.where(is_ctx, xc_ref[...], xl_ref[...])
    m = mod_ref[0]
    x1 = x + m[2:3, :] * jnp.dot(mx_ref[...], w_ref[...], preferred_element_type=F32)
    x1_ref[...] = x1
    h2 = _rms(x1, g_ref[...]) * (1.0 + m[4:5, :]) + m[3:4, :]
    h2_ref[...] = h2
    h_hi, h_lo = _split_bf16(h2)
    w_hi, w_lo = _split_bf16(wr_ref[...])
    lg = (jnp.dot(h_hi, w_hi, preferred_element_type=F32)
          + jnp.dot(h_hi, w_lo, preferred_element_type=F32)
          + jnp.dot(h_lo, w_hi, preferred_element_type=F32))
    lg_ref[...] = lg + br_ref[...]


def _outproj(mixed, x_ctx, x_lat, mod, n2g, w_out_bf, w_router, b_router):
    tm = 512
    const = lambda i: (0, 0)
    xc_spec, xl_spec = _set_specs(tm, 2)
    return pl.pallas_call(
        _outproj_kernel,
        grid=(N_TOK // tm,),
        in_specs=[pl.BlockSpec((tm, D), lambda i: (i, 0)),
                  xc_spec, xl_spec,
                  pl.BlockSpec((1, N_MOD, D), lambda i: (_mod_row(i, tm), 0, 0)),
                  pl.BlockSpec((1, D), const),
                  pl.BlockSpec((D, D), const, pipeline_mode=pl.Buffered(1)),
                  pl.BlockSpec((D, N_EXP), const),
                  pl.BlockSpec((1, N_EXP), const)],
        out_specs=[pl.BlockSpec((tm, D), lambda i: (i, 0)),
                   pl.BlockSpec((tm, D), lambda i: (i, 0)),
                   pl.BlockSpec((tm, N_EXP), lambda i: (i, 0))],
        out_shape=[jax.ShapeDtypeStruct((N_TOK, D), F32),
                   jax.ShapeDtypeStruct((N_TOK, D), F32),
                   jax.ShapeDtypeStruct((N_TOK, N_EXP), F32)],
        compiler_params=_params(("arbitrary",)),
        name="outproj",
    )(mixed, x_ctx, x_lat, mod, n2g, w_out_bf, w_router, b_router)


SEL_CHUNK = 128


def _prefix_sum(flags, out_ref):
    n = flags.shape[1]
    ri = lax.broadcasted_iota(jnp.int32, (SEL_CHUNK, SEL_CHUNK), 0)
    ci = lax.broadcasted_iota(jnp.int32, (SEL_CHUNK, SEL_CHUNK), 1)
    tri = jnp.where(ri <= ci, 1.0, 0.0).astype(BF16)
    run = jnp.zeros((N_EXP, 1), F32)
    for ch in range(n // SEL_CHUNK):
        blk = flags[:, ch * SEL_CHUNK:(ch + 1) * SEL_CHUNK].astype(BF16)
        within = jnp.dot(blk, tri, preferred_element_type=F32) + run
        out_ref[:, ch * SEL_CHUNK:(ch + 1) * SEL_CHUNK] = within
        run = within[:, SEL_CHUNK - 1:SEL_CHUNK]


def _select_kernel(lg_ref, idx_ref, gate_ref, cs_ref):
    n = lg_ref.shape[1]
    lg = lg_ref[...]
    ex = jnp.exp(lg - jnp.max(lg, axis=0, keepdims=True))
    aff = ex / jnp.sum(ex, axis=0, keepdims=True)

    def bit_step(b, cur):
        cand = cur | (1 << (30 - b))
        cnt = jnp.sum(jnp.where(aff >= pltpu.bitcast(cand, F32), 1.0, 0.0), axis=1, keepdims=True)
        return jnp.where(cnt >= CAP, cand, cur)

    thr = pltpu.bitcast(lax.fori_loop(0, 31, bit_step, jnp.zeros((N_EXP, 1), jnp.int32)), F32)
    above = aff > thr
    tied = aff == thr
    room = CAP - jnp.sum(jnp.where(above, 1.0, 0.0), axis=1, keepdims=True)
    _prefix_sum(jnp.where(tied, 1.0, 0.0), cs_ref)
    sel = jnp.logical_or(above, jnp.logical_and(tied, cs_ref[...] <= room))
    gsel = jnp.where(sel, aff, 0.0)
    _prefix_sum(jnp.where(sel, 1.0, 0.0), cs_ref)

    slot = lax.broadcasted_iota(jnp.int32, (CAP, n), 0).astype(F32)
    g_hi = gsel.astype(BF16).astype(F32)
    rest = gsel - g_hi
    g_mid = rest.astype(BF16).astype(F32)
    g_lo = rest - g_mid
    ones = jnp.ones((1, n), F32)
    pad = jnp.zeros((4, n), F32)
    last_slot = lax.broadcasted_iota(jnp.int32, (1, CAP), 1) == CAP - 1
    on_diag = (lax.broadcasted_iota(jnp.int32, (CAP, CAP), 0)
               == lax.broadcasted_iota(jnp.int32, (CAP, CAP), 1))
    total = jnp.sum(gsel, axis=1, keepdims=True)
    for e in range(N_EXP):
        row = slice(e, e + 1)
        le = jnp.where(cs_ref[row, :] <= slot, 1.0, 0.0).astype(BF16)
        lhs = jnp.concatenate([ones, g_hi[row], g_mid[row], g_lo[row], pad], axis=0).astype(BF16)
        res = lax.dot_general(lhs, le, (((1,), (1,)), ((), ())), preferred_element_type=F32)
        idx_ref[0, row, :] = res[0:1, :].astype(jnp.int32)
        cum = res[1:2, :] + res[2:3, :] + res[3:4, :]
        gate = jnp.where(last_slot, total[row], pltpu.roll(cum, CAP - 1, axis=1)) - cum
        gate_ref[e] = jnp.sum(jnp.where(on_diag, gate, 0.0), axis=1, keepdims=True)


def _select(logits_t):
    return pl.pallas_call(
        _select_kernel,
        grid=(N_SETS,),
        in_specs=[pl.BlockSpec((N_EXP, N_CTX), lambda s: (0, s))],
        out_specs=[pl.BlockSpec((1, N_EXP, CAP), lambda s: (s, 0, 0)),
                   pl.BlockSpec((N_EXP, CAP, 1), lambda s: (0, s, 0))],
        out_shape=[jax.ShapeDtypeStruct((N_SETS, N_EXP, CAP), jnp.int32),
                   jax.ShapeDtypeStruct((N_EXP, N_SETS * CAP, 1), F32)],
        scratch_shapes=[pltpu.VMEM((N_EXP, N_CTX), F32)],
        compiler_params=_params(("arbitrary",)),
        name="select",
    )(logits_t)


MOE_ROWS = N_SETS * CAP
MOE_TF = 256
MOE_NF = D_EXP // MOE_TF
MOE_TN = 512
N_FILL = N_TOK // MOE_ROWS
assert MOE_NF == 4


def _moe_kernel(rows_ref, h_hbm, gate_ref, wg_ref, wu_ref, wd_ref, y_hbm,
                xf_ref, xb_ref, acc_ref, rmw_ref, sem_x, sem_put, sem_get, sem_fill):
    e = pl.program_id(0)
    f = pl.program_id(1)
    e_next = jnp.minimum(e + 1, N_EXP - 1)

    def tok_row(hbm_ref, ex, k):
        return hbm_ref.at[pl.ds(rows_ref[ex * MOE_ROWS + k], 1), :]

    def x_copy(ex, k):
        return pltpu.make_async_copy(tok_row(h_hbm, ex, k), xf_ref.at[pl.ds(k, 1), :], sem_x)

    def put_copy(ex, k):
        return pltpu.make_async_copy(rmw_ref.at[pl.ds(k, 1), :], tok_row(y_hbm, ex, k), sem_put)

    def get_copy(ex, k):
        return pltpu.make_async_copy(tok_row(y_hbm, ex, k), rmw_ref.at[pl.ds(k, 1), :], sem_get)

    def fill_copy(c):
        return pltpu.make_async_copy(rmw_ref, y_hbm.at[pl.ds(c * MOE_ROWS, MOE_ROWS), :], sem_fill)

    def start_rows(copy, ex):
        for k in range(MOE_ROWS):
            copy(ex, k).start(priority=k % 2)

    def wait_rows(sem, src, dst):
        pltpu.make_async_copy(src, dst, sem).wait()

    y_rows = y_hbm.at[pl.ds(0, MOE_ROWS), :]
    h_rows = h_hbm.at[pl.ds(0, MOE_ROWS), :]

    def ffn_step(first=False):
        x = xb_ref[...]
        g = jnp.dot(x, wg_ref[...].astype(BF16), preferred_element_type=F32)
        up = jnp.dot(x, wu_ref[...].astype(BF16), preferred_element_type=F32)
        hid = (_silu(g) * up).astype(BF16)
        wd = wd_ref[...].astype(BF16)
        for c in range(D // MOE_TN):
            cols = slice(c * MOE_TN, (c + 1) * MOE_TN)
            part = jnp.dot(hid, wd[:, cols], preferred_element_type=F32)
            if first:
                acc_ref[:, cols] = part
            else:
                acc_ref[:, cols] += part

    @pl.when(jnp.logical_and(e == 0, f == 0))
    def _():
        rmw_ref[...] = jnp.zeros(rmw_ref.shape, F32)
        for c in range(N_FILL):
            fill_copy(c).start()

        def issue(k, c):
            x_copy(0, k).start()
            return c
        lax.fori_loop(0, MOE_ROWS, issue, 0)

    @pl.when(f == 0)
    def _():
        wait_rows(sem_x, h_rows, xf_ref)
        xb_ref[...] = xf_ref[...].astype(BF16)

    @pl.when(jnp.logical_and(e > 0, f == 0))
    def _():
        start_rows(put_copy, e - 1)
        ffn_step(first=True)

    @pl.when(jnp.logical_and(e == 0, f == 0))
    def _():
        ffn_step(first=True)

    @pl.when(f == 1)
    def _():
        start_rows(x_copy, e_next)
        ffn_step()

    @pl.when(jnp.logical_and(e == 0, f == 2))
    def _():
        for c in range(N_FILL):
            fill_copy(c).wait()

    @pl.when(jnp.logical_and(e > 0, f == 2))
    def _():
        wait_rows(sem_put, rmw_ref, y_rows)

    @pl.when(f == 2)
    def _():
        start_rows(get_copy, e)
        ffn_step()

    @pl.when(f == 3)
    def _():
        ffn_step()
        wait_rows(sem_get, y_rows, rmw_ref)
        rmw_ref[...] = rmw_ref[...] + gate_ref[...] * acc_ref[...]

    @pl.when(jnp.logical_and(e == N_EXP - 1, f == MOE_NF - 1))
    def _():
        def issue(k, c):
            put_copy(N_EXP - 1, k).start()
            return c
        lax.fori_loop(0, MOE_ROWS, issue, 0)
        wait_rows(sem_put, rmw_ref, y_rows)
        wait_rows(sem_x, h_rows, xf_ref)


def _moe(rows, h2, gates, w_g, w_u, w_d):
    return pl.pallas_call(
        _moe_kernel,
        grid_spec=pltpu.PrefetchScalarGridSpec(
            num_scalar_prefetch=1,
            grid=(N_EXP, MOE_NF),
            in_specs=[pl.BlockSpec(memory_space=pl.ANY),
                      pl.BlockSpec((None, MOE_ROWS, 1), lambda e, f, s: (e, 0, 0)),
                      pl.BlockSpec((None, D, MOE_TF), lambda e, f, s: (e, 0, f)),
                      pl.BlockSpec((None, D, MOE_TF), lambda e, f, s: (e, 0, f)),
                      pl.BlockSpec((None, MOE_TF, D), lambda e, f, s: (e, f, 0))],
            out_specs=pl.BlockSpec(memory_space=pl.ANY),
            scratch_shapes=[pltpu.VMEM((MOE_ROWS, D), F32),
                            pltpu.VMEM((MOE_ROWS, D), BF16),
                            pltpu.VMEM((MOE_ROWS, D), F32),
                            pltpu.VMEM((MOE_ROWS, D), F32),
                            pltpu.SemaphoreType.DMA(()),
                            pltpu.SemaphoreType.DMA(()),
                            pltpu.SemaphoreType.DMA(()),
                            pltpu.SemaphoreType.DMA(())]),
        out_shape=jax.ShapeDtypeStruct((N_TOK, D), F32),
        compiler_params=_params(("arbitrary", "arbitrary")),
        name="moe",
    )(rows, h2, gates, w_g, w_u, w_d)


def _final_kernel(x1_ref, y_ref, mod_ref, g_ref, o_ref):
    o_ref[...] = _rms(x1_ref[...] + mod_ref[0][5:6, :] * y_ref[...], g_ref[...])


def _final(x1, y, tok0, n_tok, mod, fng):
    tm = 512
    blk0 = tok0 // tm
    return pl.pallas_call(
        _final_kernel,
        grid=(n_tok // tm,),
        in_specs=[pl.BlockSpec((tm, D), lambda i: (blk0 + i, 0)),
                  pl.BlockSpec((tm, D), lambda i: (blk0 + i, 0)),
                  pl.BlockSpec((1, N_MOD, D), lambda i: (_mod_row(blk0 + i, tm), 0, 0)),
                  pl.BlockSpec((1, D), lambda i: (0, 0))],
        out_specs=pl.BlockSpec((tm, D), lambda i: (i, 0)),
        out_shape=jax.ShapeDtypeStruct((n_tok, D), F32),
        compiler_params=_params(("arbitrary",)),
        name="final",
    )(x1, y, mod, fng)


def kernel(x_prompt, x_sample, c, cache_k, cache_v, c_ctx, w_ada, b_ada, norm1_g, w_in, conv_dw, conv_dw_b, conv_ln_g, conv_ln_b, w_conv_out, attn_sink, w_attn_out, w_out, norm2_g, w_router, b_router, w_exp_gate, w_exp_up, w_exp_down, final_norm_g):
    x_ctx = x_prompt.reshape(N_CTX, D)
    x_lat = x_sample.reshape(N_LAT, D)
    cond = jnp.concatenate([c, c_ctx[None, :], jnp.zeros((MOD_ROWS - N_LAT_SEQ - 1, D), F32)], axis=0)
    mod = _ada(cond, w_ada[0], b_ada[0]).reshape(MOD_ROWS, N_MOD, D)

    u = _inproj(_prenorm(x_ctx, x_lat, mod, norm1_g), w_in[0])
    zc = _conv(u, conv_dw[0], conv_dw_b, conv_ln_g, conv_ln_b)

    sink = attn_sink[0]
    cos, sin = _rope_tables()
    att_ctx, state_k, state_v = _ctx_attn(u, sink)
    mixed, w_out_bf = _mix(zc, att_ctx, _lat_attn(u, sink, cache_k, cache_v, cos, sin), u,
                           w_conv_out[0], w_attn_out[0], w_out[0])

    x1, h2, logits = _outproj(mixed, x_ctx, x_lat, mod, norm2_g, w_out_bf, w_router[0], b_router)

    idx, gates = _select(logits.T)
    rows = (idx + (jnp.arange(N_SETS, dtype=jnp.int32) * N_CTX)[:, None, None])
    rows = rows.transpose(1, 0, 2).reshape(N_EXP * MOE_ROWS)
    ffn = _moe(rows, h2, gates, w_exp_gate[0], w_exp_up[0], w_exp_down[0])

    fng = final_norm_g.reshape(1, D)
    y_prompt = _final(x1, ffn, 0, N_CTX, mod, fng).reshape(N_CTX_SEQ, CTX_LEN, D)
    y_sample = _final(x1, ffn, N_CTX, N_LAT, mod, fng).reshape(N_LAT_SEQ, LAT_LEN, D)
    return (y_prompt, y_sample, state_k, state_v)
```

```python
import functools

import jax
import jax.numpy as jnp
from jax import lax
from jax.experimental import pallas as pl
from jax.experimental.pallas import tpu as pltpu

D = 2048
N_CTX_SEQ, CTX_LEN = 16, 256
N_LAT_SEQ, LAT_LEN = 4, 1024
N_CTX = N_CTX_SEQ * CTX_LEN
N_LAT = N_LAT_SEQ * LAT_LEN
N_TOK = N_CTX + N_LAT
PAST = 512
GRID_W = 64
N_HEADS, N_KV, HD = 16, 4, 128
Q_GROUP = N_HEADS // N_KV
WINDOW = 128
ABLK = 128
CONV_W = 1024
CONV_K = 31
CONV_PAD = CONV_K // 2
N_EXP = 16
D_EXP = 1024
CAP = 2 * N_CTX // N_EXP
N_SETS = 2
ROPE_BASE = 10000.0
EPS = 1e-6
N_MOD = 6
NEG = -1e30
IN_COLS = 2 * CONV_W + N_HEADS * HD + 2 * N_KV * HD + 2 * D
COL_Q = 2 * CONV_W
COL_K = COL_Q + N_HEADS * HD
COL_V = COL_K + N_KV * HD
COL_GA = COL_V + N_KV * HD
COL_GB = COL_GA + D
MOD_ROWS = 8
CTX_MOD_ROW = N_LAT_SEQ

VMEM_LIMIT = 58 * 1024 * 1024
F32 = jnp.float32
BF16 = jnp.bfloat16


def _params(sem, vmem=VMEM_LIMIT):
    return pltpu.CompilerParams(dimension_semantics=sem, vmem_limit_bytes=vmem)


def _mod_row(tok_block, block_tokens):
    ctx_blocks = N_CTX // block_tokens
    per_seq = LAT_LEN // block_tokens
    return jnp.where(tok_block < ctx_blocks, CTX_MOD_ROW, (tok_block - ctx_blocks) // per_seq)


def _silu(x):
    return x * jax.nn.sigmoid(x)


def _rms(x, g):
    return x * lax.rsqrt(jnp.mean(x * x, axis=-1, keepdims=True) + EPS) * g


def _ada_kernel(cond_ref, w_ref, b_ref, o_ref):
    a = _silu(cond_ref[...]).astype(BF16)
    o_ref[...] = jnp.dot(a, w_ref[...].astype(BF16), preferred_element_type=F32) + b_ref[...]


def _ada(cond, w_ada, b_ada):
    tn = 1024
    n = N_MOD * D
    return pl.pallas_call(
        _ada_kernel,
        grid=(n // tn,),
        in_specs=[pl.BlockSpec((MOD_ROWS, D), lambda j: (0, 0)),
                  pl.BlockSpec((D, tn), lambda j: (0, j)),
                  pl.BlockSpec((1, tn), lambda j: (0, j))],
        out_specs=pl.BlockSpec((MOD_ROWS, tn), lambda j: (0, j)),
        out_shape=jax.ShapeDtypeStruct((MOD_ROWS, n), F32),
        compiler_params=_params(("arbitrary",)),
        name="ada",
    )(cond, w_ada, b_ada.reshape(1, n))


def _set_specs(tm, buffers):
    nc = N_CTX // tm
    mode = pl.Buffered(buffers)
    ctx = pl.BlockSpec((tm, D), lambda i, *_: (jnp.minimum(i, nc - 1), 0), pipeline_mode=mode)
    lat = pl.BlockSpec((tm, D), lambda i, *_: (jnp.maximum(i - nc, 0), 0), pipeline_mode=mode)
    return ctx, lat


def _prenorm_kernel(xc_ref, xl_ref, mod_ref, g_ref, h_ref):
    is_ctx = pl.program_id(0) < N_CTX // xc_ref.shape[0]
    x = jnp.where(is_ctx, xc_ref[...], xl_ref[...])
    m = mod_ref[0]
    h = _rms(x, g_ref[...]) * (1.0 + m[1:2, :]) + m[0:1, :]
    h_ref[...] = h.astype(BF16)


def _prenorm(x_ctx, x_lat, mod, n1g):
    tm = 512
    xc_spec, xl_spec = _set_specs(tm, 2)
    return pl.pallas_call(
        _prenorm_kernel,
        grid=(N_TOK // tm,),
        in_specs=[xc_spec, xl_spec,
                  pl.BlockSpec((1, N_MOD, D), lambda i: (_mod_row(i, tm), 0, 0)),
                  pl.BlockSpec((1, D), lambda i: (0, 0))],
        out_specs=pl.BlockSpec((tm, D), lambda i: (i, 0)),
        out_shape=jax.ShapeDtypeStruct((N_TOK, D), BF16),
        compiler_params=_params(("arbitrary",)),
        name="prenorm",
    )(x_ctx, x_lat, mod, n1g)


def _inproj_kernel(h_ref, w_ref, o_ref, wb_ref):
    @pl.when(pl.program_id(1) == 0)
    def _():
        wb_ref[...] = w_ref[...].astype(BF16)

    o_ref[...] = jnp.dot(h_ref[...], wb_ref[...], preferred_element_type=F32)


def _inproj(h, w_in):
    tm, tn = 1024, 1536
    return pl.pallas_call(
        _inproj_kernel,
        grid=(IN_COLS // tn, N_TOK // tm),
        in_specs=[pl.BlockSpec((tm, D), lambda j, i: (i, 0)),
                  pl.BlockSpec((D, tn), lambda j, i: (0, j))],
        out_specs=pl.BlockSpec((tm, tn), lambda j, i: (i, j)),
        out_shape=jax.ShapeDtypeStruct((N_TOK, IN_COLS), F32),
        scratch_shapes=[pltpu.VMEM((D, tn), BF16)],
        compiler_params=_params(("arbitrary", "arbitrary")),
        name="inproj",
    )(h, w_in)


CONV_T = 256
CONV_HALO = 16
CONV_ROWS = 32
SUBLANES = 8
LANES = 128
CONV_ZS = CONV_T + (CONV_K // SUBLANES) * SUBLANES


def _conv_kernel(main_ref, prev_ref, next_ref, w_ref, b_ref, lg_ref, lb_ref, o_ref,
                 z_ref, zs_ref, y_ref):
    i = pl.program_id(0)
    ctx_blocks = N_CTX // CONV_T
    per_seq = LAT_LEN // CONV_T
    pos = (i - ctx_blocks) % per_seq
    is_lat = i >= ctx_blocks
    has_prev = jnp.logical_and(is_lat, pos > 0)
    has_next = jnp.logical_and(is_lat, pos < per_seq - 1)

    def glu(ref):
        v = ref[...]
        return v[:, :CONV_W] * jax.nn.sigmoid(v[:, CONV_W:])

    z_ref[0:CONV_HALO, :] = jnp.where(has_prev, glu(prev_ref), 0.0)
    z_ref[CONV_HALO:CONV_HALO + CONV_T, :] = glu(main_ref)
    z_ref[CONV_HALO + CONV_T:, :] = jnp.where(has_next, glu(next_ref), 0.0)

    z = z_ref[...]
    zs_ref[0] = z[:CONV_ZS]
    for s in range(1, SUBLANES):
        zs_ref[s] = pltpu.roll(z, z.shape[0] - s, axis=0)[:CONV_ZS]

    def channel_group(c, carry):
        cols = pl.ds(pl.multiple_of(c * LANES, LANES), LANES)
        acc = jnp.zeros((CONV_T, LANES), F32)
        for j in range(CONV_K):
            a, s = divmod(CONV_HALO - CONV_PAD + j, SUBLANES)
            acc = acc + zs_ref[s, a * SUBLANES:a * SUBLANES + CONV_T, cols] * w_ref[j:j + 1, cols]
        y_ref[:, cols] = acc + b_ref[:, cols]
        return carry

    lax.fori_loop(0, CONV_W // LANES, channel_group, 0)

    for r in range(CONV_T // CONV_ROWS):
        acc = y_ref[r * CONV_ROWS:(r + 1) * CONV_ROWS, :]
        mu = jnp.mean(acc, axis=-1, keepdims=True)
        cen = acc - mu
        var = jnp.mean(cen * cen, axis=-1, keepdims=True)
        y = cen * lax.rsqrt(var + EPS) * lg_ref[...] + lb_ref[...]
        o_ref[r * CONV_ROWS:(r + 1) * CONV_ROWS, :] = _silu(y).astype(BF16)


def _conv(u, w_dw, b_dw, ln_g, ln_b):
    nblk = N_TOK // CONV_T
    hpb = CONV_T // CONV_HALO
    last = N_TOK // CONV_HALO - 1
    row = lambda i: (0, 0)
    return pl.pallas_call(
        _conv_kernel,
        grid=(nblk,),
        in_specs=[pl.BlockSpec((CONV_T, 2 * CONV_W), lambda i: (i, 0)),
                  pl.BlockSpec((CONV_HALO, 2 * CONV_W), lambda i: (jnp.maximum(i * hpb - 1, 0), 0)),
                  pl.BlockSpec((CONV_HALO, 2 * CONV_W), lambda i: (jnp.minimum((i + 1) * hpb, last), 0)),
                  pl.BlockSpec((CONV_K, CONV_W), row),
                  pl.BlockSpec((1, CONV_W), row),
                  pl.BlockSpec((1, CONV_W), row),
                  pl.BlockSpec((1, CONV_W), row)],
        out_specs=pl.BlockSpec((CONV_T, CONV_W), lambda i: (i, 0)),
        out_shape=jax.ShapeDtypeStruct((N_TOK, CONV_W), BF16),
        scratch_shapes=[pltpu.VMEM((CONV_T + 2 * CONV_HALO, CONV_W), F32),
                        pltpu.VMEM((SUBLANES, CONV_ZS, CONV_W), F32),
                        pltpu.VMEM((CONV_T, CONV_W), F32)],
        compiler_params=_params(("arbitrary",)),
        name="conv",
    )(u, u, u, w_dw, b_dw, ln_g, ln_b)


def _stack_heads(q):
    return jnp.concatenate([q[:, g * HD:(g + 1) * HD] for g in range(Q_GROUP)], axis=0)


def _unstack_heads(o, rows):
    return jnp.concatenate([o[g * rows:(g + 1) * rows, :] for g in range(Q_GROUP)], axis=1)


LOG2E = 1.4426950408889634
Q_SCALE = HD ** -0.5 * LOG2E
QW = Q_GROUP * HD


def _sink_column(sink_ref, kv, rows):
    rid = lax.broadcasted_iota(jnp.int32, (Q_GROUP * rows, 1), 0)
    col = jnp.full((Q_GROUP * rows, 1), sink_ref[kv * Q_GROUP], F32)
    for g in range(1, Q_GROUP):
        col = jnp.where(rid >= g * rows, sink_ref[kv * Q_GROUP + g], col)
    return col * LOG2E


def _with_ones(vals):
    return jnp.concatenate([vals, jnp.ones(vals.shape, vals.dtype)], axis=1)


def _softmax_av(s, sk, vals):
    m = jnp.maximum(jnp.max(s, axis=-1, keepdims=True), sk)
    if vals.shape[1] == 2 * HD:
        p = jnp.exp2(s - m).astype(BF16)
        o = jnp.dot(p, vals, preferred_element_type=F32)
        return o[:, :HD] / (o[:, HD:HD + 1] + jnp.exp2(sk - m))
    p = jnp.exp2(s - m)
    den = jnp.sum(p, axis=-1, keepdims=True) + jnp.exp2(sk - m)
    return jnp.dot(p.astype(BF16), vals, preferred_element_type=F32) / den


_NT = (((1,), (1,)), ((), ()))


def _ctx_attn_kernel(sink_ref, q_ref, k_ref, v_ref, o_ref, sk_ref, sv_ref):
    for kv in range(N_KV):
        k = k_ref[:, kv * HD:(kv + 1) * HD]
        v = v_ref[:, kv * HD:(kv + 1) * HD]
        sk_ref[0, 0, :, kv, :] = k
        sv_ref[0, 0, :, kv, :] = v
        q = (_stack_heads(q_ref[:, kv * QW:(kv + 1) * QW]) * Q_SCALE).astype(BF16)
        s = lax.dot_general(q, k.astype(BF16), _NT, preferred_element_type=F32)
        o = _softmax_av(s, _sink_column(sink_ref, kv, CTX_LEN), v.astype(BF16))
        o_ref[:, kv * QW:(kv + 1) * QW] = _unstack_heads(o, CTX_LEN).astype(BF16)


def _ctx_attn(u, sink):
    qcols, kcols = N_HEADS * HD, N_KV * HD
    state_spec = pl.BlockSpec((1, 1, CTX_LEN, N_KV, HD), lambda b, s: (b, 0, 0, 0, 0))
    state_shape = jax.ShapeDtypeStruct((N_CTX_SEQ, 1, CTX_LEN, N_KV, HD), F32)
    return pl.pallas_call(
        _ctx_attn_kernel,
        grid_spec=pltpu.PrefetchScalarGridSpec(
            num_scalar_prefetch=1,
            grid=(N_CTX_SEQ,),
            in_specs=[pl.BlockSpec((CTX_LEN, qcols), lambda b, s: (b, COL_Q // qcols)),
                      pl.BlockSpec((CTX_LEN, kcols), lambda b, s: (b, COL_K // kcols)),
                      pl.BlockSpec((CTX_LEN, kcols), lambda b, s: (b, COL_V // kcols))],
            out_specs=[pl.BlockSpec((CTX_LEN, qcols), lambda b, s: (b, 0)),
                       state_spec, state_spec]),
        out_shape=[jax.ShapeDtypeStruct((N_CTX, N_HEADS * HD), BF16), state_shape, state_shape],
        compiler_params=_params(("arbitrary",)),
        name="ctx_attn",
    )(sink, u, u, u)


def _rope(x, cos, sin):
    lane = lax.broadcasted_iota(jnp.int32, x.shape, 1)
    fwd = pltpu.roll(x, HD - 32, axis=1)
    bwd = pltpu.roll(x, 32, axis=1)
    partner = jnp.where((lane & 32) == 0, fwd, bwd)
    return x * cos + partner * sin


def _lat_attn_kernel(sink_ref, q_ref, k_ref, v_ref, ck_ref, cv_ref, cos_ref, sin_ref, o_ref):
    kv = pl.program_id(1)
    nblk = LAT_LEN // ABLK
    cos, sin = cos_ref[...], sin_ref[...]
    k_rot = _rope(k_ref[...], cos, sin).astype(BF16)
    v = _with_ones(v_ref[...].astype(BF16))
    ck = ck_ref[:, kv, :].astype(BF16)
    cv = _with_ones(cv_ref[:, kv, :].astype(BF16))
    sk = _sink_column(sink_ref, kv, ABLK)
    r = lax.broadcasted_iota(jnp.int32, (Q_GROUP * ABLK, ABLK), 0) & (ABLK - 1)
    c = lax.broadcasted_iota(jnp.int32, (Q_GROUP * ABLK, ABLK), 1)
    open_tile = jnp.zeros((Q_GROUP * ABLK, ABLK), F32)
    below = jnp.where(c >= r, 0.0, NEG)
    above = jnp.where(c <= r, 0.0, NEG)
    ctx_open = jnp.zeros((Q_GROUP * ABLK, PAST), F32)
    bias_first = jnp.concatenate([open_tile, above, ctx_open], axis=1)
    bias_mid = jnp.concatenate([below, open_tile, above, ctx_open], axis=1)
    bias_last = jnp.concatenate([below, open_tile, ctx_open], axis=1)
    for j in range(nblk):
        rows = slice(j * ABLK, (j + 1) * ABLK)
        cos_j, sin_j = cos[rows], sin[rows]
        qs = jnp.concatenate(
            [_rope(q_ref[rows, g * HD:(g + 1) * HD], cos_j, sin_j) * Q_SCALE for g in range(Q_GROUP)],
            axis=0).astype(BF16)
        blocks = range(max(j - 1, 0), min(j + 2, nblk))
        band = slice(blocks[0] * ABLK, (blocks[-1] + 1) * ABLK)
        keys = jnp.concatenate([k_rot[band], ck], axis=0)
        vals = jnp.concatenate([v[band], cv], axis=0)
        bias = bias_first if j == 0 else bias_last if j == nblk - 1 else bias_mid
        s = lax.dot_general(qs, keys, _NT, preferred_element_type=F32) + bias
        o = _softmax_av(s, sk, vals)
        o_ref[rows, :] = _unstack_heads(o, ABLK).astype(BF16)


def _lat_attn(u, sink, ck, cv, cos, sin):
    base = N_CTX // LAT_LEN
    cache_spec = pl.BlockSpec((None, None, PAST, N_KV, HD), lambda b, kv, s: (b, 0, 0, 0, 0))
    table_spec = pl.BlockSpec((LAT_LEN, HD), lambda b, kv, s: (0, 0))
    return pl.pallas_call(
        _lat_attn_kernel,
        grid_spec=pltpu.PrefetchScalarGridSpec(
            num_scalar_prefetch=1,
            grid=(N_LAT_SEQ, N_KV),
            in_specs=[pl.BlockSpec((LAT_LEN, QW), lambda b, kv, s: (base + b, COL_Q // QW + kv)),
                      pl.BlockSpec((LAT_LEN, HD), lambda b, kv, s: (base + b, COL_K // HD + kv)),
                      pl.BlockSpec((LAT_LEN, HD), lambda b, kv, s: (base + b, COL_V // HD + kv)),
                      cache_spec, cache_spec, table_spec, table_spec],
            out_specs=pl.BlockSpec((LAT_LEN, QW), lambda b, kv, s: (b, kv))),
        out_shape=jax.ShapeDtypeStruct((N_LAT, N_HEADS * HD), BF16),
        compiler_params=_params(("arbitrary", "arbitrary")),
        name="lat_attn",
    )(sink, u, u, u, ck, cv, cos, sin)


def _rope_tables():
    n = jnp.arange(LAT_LEN)
    row = (n // GRID_W).astype(F32)
    col = (n % GRID_W).astype(F32)
    quarter = HD // 4
    inv = ROPE_BASE ** (-jnp.arange(quarter, dtype=F32) / quarter)
    ang_r = row[:, None] * inv[None, :]
    ang_c = col[:, None] * inv[None, :]
    cos = jnp.concatenate([jnp.cos(ang_r)] * 2 + [jnp.cos(ang_c)] * 2, axis=-1)
    sin = jnp.concatenate([-jnp.sin(ang_r), jnp.sin(ang_r), -jnp.sin(ang_c), jnp.sin(ang_c)], axis=-1)
    return cos, sin


def _mix_kernel(z_ref, ac_ref, al_ref, ga_ref, gb_ref, wc_ref, wa_ref, wo_ref,
                o_ref, wob_ref, wcb_ref, wab_ref):
    @pl.when(pl.program_id(1) == 0)
    def _():
        wcb_ref[...] = wc_ref[...].astype(BF16)
        wab_ref[...] = wa_ref[...].astype(BF16)

    wob_ref[...] = wo_ref[...].astype(BF16)

    is_ctx = pl.program_id(1) < N_CTX // z_ref.shape[0]
    att = jnp.where(is_ctx, ac_ref[...], al_ref[...])
    conv_out = jnp.dot(z_ref[...], wcb_ref[...], preferred_element_type=F32)
    attn_out = jnp.dot(att, wab_ref[...], preferred_element_type=F32)
    mixed = jax.nn.sigmoid(ga_ref[...]) * conv_out + jax.nn.sigmoid(gb_ref[...]) * attn_out
    o_ref[...] = mixed.astype(BF16)


def _mix(zc, att_ctx, att_lat, u, w_conv_out, w_attn_out, w_out):
    tm, tn = 512, 1024
    nc = N_CTX // tm
    ni = N_TOK // tm
    slab = D // ((D // tn) * ni)
    slab_spec = pl.BlockSpec((slab, D), lambda j, i: (j * ni + i, 0))
    return pl.pallas_call(
        _mix_kernel,
        grid=(D // tn, N_TOK // tm),
        in_specs=[pl.BlockSpec((tm, CONV_W), lambda j, i: (i, 0)),
                  pl.BlockSpec((tm, N_HEADS * HD), lambda j, i: (jnp.minimum(i, nc - 1), 0)),
                  pl.BlockSpec((tm, N_HEADS * HD), lambda j, i: (jnp.maximum(i - nc, 0), 0)),
                  pl.BlockSpec((tm, tn), lambda j, i: (i, COL_GA // tn + j)),
                  pl.BlockSpec((tm, tn), lambda j, i: (i, COL_GB // tn + j)),
                  pl.BlockSpec((CONV_W, tn), lambda j, i: (0, j)),
                  pl.BlockSpec((N_HEADS * HD, tn), lambda j, i: (0, j)),
                  slab_spec],
        out_specs=[pl.BlockSpec((tm, tn), lambda j, i: (i, j)), slab_spec],
        out_shape=[jax.ShapeDtypeStruct((N_TOK, D), BF16), jax.ShapeDtypeStruct((D, D), BF16)],
        scratch_shapes=[pltpu.VMEM((CONV_W, tn), BF16), pltpu.VMEM((N_HEADS * HD, tn), BF16)],
        compiler_params=_params(("arbitrary", "arbitrary")),
        name="mix",
    )(zc, att_ctx, att_lat, u, u, w_conv_out, w_attn_out, w_out)


def _split_bf16(x):
    hi = x.astype(BF16)
    lo = (x - hi.astype(F32)).astype(BF16)
    return hi, lo


def _outproj_kernel(mx_ref, xc_ref, xl_ref, mod_ref, g_ref, w_ref, wr_ref, br_ref,
                    x1_ref, h2_ref, lg_ref):
    is_ctx = pl.program_id(0) < N_CTX // xc_ref.shape[0]
    x = jnp.where(is_ctx, xc_ref[...], xl_ref[...])
    m = mod_ref[0]
    x1 = x + m[2:3, :] * jnp.dot(mx_ref[...], w_ref[...], preferred_element_type=F32)
    x1_ref[...] = x1
    h2 = _rms(x1, g_ref[...]) * (1.0 + m[4:5, :]) + m[3:4, :]
    h2_ref[...] = h2
    h_hi, h_lo = _split_bf16(h2)
    w_hi, w_lo = _split_bf16(wr_ref[...])
    lg = (jnp.dot(h_hi, w_hi, preferred_element_type=F32)
          + jnp.dot(h_hi, w_lo, preferred_element_type=F32)
          + jnp.dot(h_lo, w_hi, preferred_element_type=F32))
    lg_ref[...] = lg + br_ref[...]


def _outproj(mixed, x_ctx, x_lat, mod, n2g, w_out_bf, w_router, b_router):
    tm = 512
    const = lambda i: (0, 0)
    xc_spec, xl_spec = _set_specs(tm, 2)
    return pl.pallas_call(
        _outproj_kernel,
        grid=(N_TOK // tm,),
        in_specs=[pl.BlockSpec((tm, D), lambda i: (i, 0)),
                  xc_spec, xl_spec,
                  pl.BlockSpec((1, N_MOD, D), lambda i: (_mod_row(i, tm), 0, 0)),
                  pl.BlockSpec((1, D), const),
                  pl.BlockSpec((D, D), const, pipeline_mode=pl.Buffered(1)),
                  pl.BlockSpec((D, N_EXP), const),
                  pl.BlockSpec((1, N_EXP), const)],
        out_specs=[pl.BlockSpec((tm, D), lambda i: (i, 0)),
                   pl.BlockSpec((tm, D), lambda i: (i, 0)),
                   pl.BlockSpec((tm, N_EXP), lambda i: (i, 0))],
        out_shape=[jax.ShapeDtypeStruct((N_TOK, D), F32),
                   jax.ShapeDtypeStruct((N_TOK, D), F32),
                   jax.ShapeDtypeStruct((N_TOK, N_EXP), F32)],
        compiler_params=_params(("arbitrary",)),
        name="outproj",
    )(mixed, x_ctx, x_lat, mod, n2g, w_out_bf, w_router, b_router)


SEL_CHUNK = 128


def _prefix_sum(flags, out_ref):
    n = flags.shape[1]
    ri = lax.broadcasted_iota(jnp.int32, (SEL_CHUNK, SEL_CHUNK), 0)
    ci = lax.broadcasted_iota(jnp.int32, (SEL_CHUNK, SEL_CHUNK), 1)
    tri = jnp.where(ri <= ci, 1.0, 0.0).astype(BF16)
    run = jnp.zeros((N_EXP, 1), F32)
    for ch in range(n // SEL_CHUNK):
        blk = flags[:, ch * SEL_CHUNK:(ch + 1) * SEL_CHUNK].astype(BF16)
        within = jnp.dot(blk, tri, preferred_element_type=F32) + run
        out_ref[:, ch * SEL_CHUNK:(ch + 1) * SEL_CHUNK] = within
        run = within[:, SEL_CHUNK - 1:SEL_CHUNK]


def _select_kernel(lg_ref, idx_ref, gate_ref, cs_ref):
    n = lg_ref.shape[1]
    lg = lg_ref[...]
    ex = jnp.exp(lg - jnp.max(lg, axis=0, keepdims=True))
    aff = ex / jnp.sum(ex, axis=0, keepdims=True)

    def bit_step(b, cur):
        cand = cur | (1 << (30 - b))
        cnt = jnp.sum(jnp.where(aff >= pltpu.bitcast(cand, F32), 1.0, 0.0), axis=1, keepdims=True)
        return jnp.where(cnt >= CAP, cand, cur)

    thr = pltpu.bitcast(lax.fori_loop(0, 31, bit_step, jnp.zeros((N_EXP, 1), jnp.int32)), F32)
    above = aff > thr
    tied = aff == thr
    room = CAP - jnp.sum(jnp.where(above, 1.0, 0.0), axis=1, keepdims=True)
    _prefix_sum(jnp.where(tied, 1.0, 0.0), cs_ref)
    sel = jnp.logical_or(above, jnp.logical_and(tied, cs_ref[...] <= room))
    gsel = jnp.where(sel, aff, 0.0)
    _prefix_sum(jnp.where(sel, 1.0, 0.0), cs_ref)

    slot = lax.broadcasted_iota(jnp.int32, (CAP, n), 0).astype(F32)
    g_hi = gsel.astype(BF16).astype(F32)
    rest = gsel - g_hi
    g_mid = rest.astype(BF16).astype(F32)
    g_lo = rest - g_mid
    ones = jnp.ones((1, n), F32)
    pad = jnp.zeros((4, n), F32)
    last_slot = lax.broadcasted_iota(jnp.int32, (1, CAP), 1) == CAP - 1
    on_diag = (lax.broadcasted_iota(jnp.int32, (CAP, CAP), 0)
               == lax.broadcasted_iota(jnp.int32, (CAP, CAP), 1))
    total = jnp.sum(gsel, axis=1, keepdims=True)
    for e in range(N_EXP):
        row = slice(e, e + 1)
        le = jnp.where(cs_ref[row, :] <= slot, 1.0, 0.0).astype(BF16)
        lhs = jnp.concatenate([ones, g_hi[row], g_mid[row], g_lo[row], pad], axis=0).astype(BF16)
        res = lax.dot_general(lhs, le, (((1,), (1,)), ((), ())), preferred_element_type=F32)
        idx_ref[0, row, :] = res[0:1, :].astype(jnp.int32)
        cum = res[1:2, :] + res[2:3, :] + res[3:4, :]
        gate = jnp.where(last_slot, total[row], pltpu.roll(cum, CAP - 1, axis=1)) - cum
        gate_ref[e] = jnp.sum(jnp.where(on_diag, gate, 0.0), axis=1, keepdims=True)


def _select(logits_t):
    return pl.pallas_call(
        _select_kernel,
        grid=(N_SETS,),
        in_specs=[pl.BlockSpec((N_EXP, N_CTX), lambda s: (0, s))],
        out_specs=[pl.BlockSpec((1, N_EXP, CAP), lambda s: (s, 0, 0)),
                   pl.BlockSpec((N_EXP, CAP, 1), lambda s: (0, s, 0))],
        out_shape=[jax.ShapeDtypeStruct((N_SETS, N_EXP, CAP), jnp.int32),
                   jax.ShapeDtypeStruct((N_EXP, N_SETS * CAP, 1), F32)],
        scratch_shapes=[pltpu.VMEM((N_EXP, N_CTX), F32)],
        compiler_params=_params(("arbitrary",)),
        name="select",
    )(logits_t)


MOE_ROWS = N_SETS * CAP
MOE_TF = 256
MOE_NF = D_EXP // MOE_TF
MOE_TN = 512
N_FILL = N_TOK // MOE_ROWS
assert MOE_NF == 4


def _moe_kernel(rows_ref, h_hbm, gate_ref, wg_ref, wu_ref, wd_ref, y_hbm,
                xf_ref, xb_ref, acc_ref, rmw_ref, sem_x, sem_put, sem_get, sem_fill):
    e = pl.program_id(0)
    f = pl.program_id(1)
    e_next = jnp.minimum(e + 1, N_EXP - 1)

    def tok_row(hbm_ref, ex, k):
        return hbm_ref.at[pl.ds(rows_ref[ex * MOE_ROWS + k], 1), :]

    def x_copy(ex, k):
        return pltpu.make_async_copy(tok_row(h_hbm, ex, k), xf_ref.at[pl.ds(k, 1), :], sem_x)

    def put_copy(ex, k):
        return pltpu.make_async_copy(rmw_ref.at[pl.ds(k, 1), :], tok_row(y_hbm, ex, k), sem_put)

    def get_copy(ex, k):
        return pltpu.make_async_copy(tok_row(y_hbm, ex, k), rmw_ref.at[pl.ds(k, 1), :], sem_get)

    def fill_copy(c):
        return pltpu.make_async_copy(rmw_ref, y_hbm.at[pl.ds(c * MOE_ROWS, MOE_ROWS), :], sem_fill)

    def start_rows(copy, ex):
        for k in range(MOE_ROWS):
            copy(ex, k).start(priority=k % 2)

    def wait_rows(sem, src, dst):
        pltpu.make_async_copy(src, dst, sem).wait()

    y_rows = y_hbm.at[pl.ds(0, MOE_ROWS), :]
    h_rows = h_hbm.at[pl.ds(0, MOE_ROWS), :]

    def ffn_step(first=False):
        x = xb_ref[...]
        g = jnp.dot(x, wg_ref[...].astype(BF16), preferred_element_type=F32)
        up = jnp.dot(x, wu_ref[...].astype(BF16), preferred_element_type=F32)
        hid = (_silu(g) * up).astype(BF16)
        wd = wd_ref[...].astype(BF16)
        for c in range(D // MOE_TN):
            cols = slice(c * MOE_TN, (c + 1) * MOE_TN)
            part = jnp.dot(hid, wd[:, cols], preferred_element_type=F32)
            if first:
                acc_ref[:, cols] = part
            else:
                acc_ref[:, cols] += part

    @pl.when(jnp.logical_and(e == 0, f == 0))
    def _():
        rmw_ref[...] = jnp.zeros(rmw_ref.shape, F32)
        for c in range(N_FILL):
            fill_copy(c).start()

        def issue(k, c):
            x_copy(0, k).start()
            return c
        lax.fori_loop(0, MOE_ROWS, issue, 0)

    @pl.when(f == 0)
    def _():
        wait_rows(sem_x, h_rows, xf_ref)
        xb_ref[...] = xf_ref[...].astype(BF16)

    @pl.when(jnp.logical_and(e > 0, f == 0))
    def _():
        start_rows(put_copy, e - 1)
        ffn_step(first=True)

    @pl.when(jnp.logical_and(e == 0, f == 0))
    def _():
        ffn_step(first=True)

    @pl.when(f == 1)
    def _():
        start_rows(x_copy, e_next)
        ffn_step()

    @pl.when(jnp.logical_and(e == 0, f == 2))
    def _():
        for c in range(N_FILL):
            fill_copy(c).wait()

    @pl.when(jnp.logical_and(e > 0, f == 2))
    def _():
        wait_rows(sem_put, rmw_ref, y_rows)

    @pl.when(f == 2)
    def _():
        start_rows(get_copy, e)
        ffn_step()

    @pl.when(f == 3)
    def _():
        ffn_step()
        wait_rows(sem_get, y_rows, rmw_ref)
        rmw_ref[...] = rmw_ref[...] + gate_ref[...] * acc_ref[...]

    @pl.when(jnp.logical_and(e == N_EXP - 1, f == MOE_NF - 1))
    def _():
        def issue(k, c):
            put_copy(N_EXP - 1, k).start()
            return c
        lax.fori_loop(0, MOE_ROWS, issue, 0)
        wait_rows(sem_put, rmw_ref, y_rows)
        wait_rows(sem_x, h_rows, xf_ref)


def _moe(rows, h2, gates, w_g, w_u, w_d):
    return pl.pallas_call(
        _moe_kernel,
        grid_spec=pltpu.PrefetchScalarGridSpec(
            num_scalar_prefetch=1,
            grid=(N_EXP, MOE_NF),
            in_specs=[pl.BlockSpec(memory_space=pl.ANY),
                      pl.BlockSpec((None, MOE_ROWS, 1), lambda e, f, s: (e, 0, 0)),
                      pl.BlockSpec((None, D, MOE_TF), lambda e, f, s: (e, 0, f)),
                      pl.BlockSpec((None, D, MOE_TF), lambda e, f, s: (e, 0, f)),
                      pl.BlockSpec((None, MOE_TF, D), lambda e, f, s: (e, f, 0))],
            out_specs=pl.BlockSpec(memory_space=pl.ANY),
            scratch_shapes=[pltpu.VMEM((MOE_ROWS, D), F32),
                            pltpu.VMEM((MOE_ROWS, D), BF16),
                            pltpu.VMEM((MOE_ROWS, D), F32),
                            pltpu.VMEM((MOE_ROWS, D), F32),
                            pltpu.SemaphoreType.DMA(()),
                            pltpu.SemaphoreType.DMA(()),
                            pltpu.SemaphoreType.DMA(()),
                            pltpu.SemaphoreType.DMA(())]),
        out_shape=jax.ShapeDtypeStruct((N_TOK, D), F32),
        compiler_params=_params(("arbitrary", "arbitrary")),
        name="moe",
    )(rows, h2, gates, w_g, w_u, w_d)


FINAL_SLOTS = 3


def _final_kernel(x1_hbm, y_hbm, mod_ref, g_ref, o_ref, xbuf, ybuf, sem_x, sem_y, *, blk0, nblk, tm):
    i = pl.program_id(0)

    def copies(b):
        rows = pl.ds(pl.multiple_of((blk0 + b) * tm, tm), tm)
        slot = b % FINAL_SLOTS
        return (pltpu.make_async_copy(x1_hbm.at[rows, :], xbuf.at[slot], sem_x.at[slot]),
                pltpu.make_async_copy(y_hbm.at[rows, :], ybuf.at[slot], sem_y.at[slot]))

    def start(b):
        for cp in copies(b):
            cp.start()

    @pl.when(i == 0)
    def _():
        for b in range(FINAL_SLOTS - 1):
            start(b)

    @pl.when(i + FINAL_SLOTS - 1 < nblk)
    def _():
        start(i + FINAL_SLOTS - 1)

    for cp in copies(i):
        cp.wait()
    slot = i % FINAL_SLOTS
    o_ref[...] = _rms(xbuf[slot] + mod_ref[0][5:6, :] * ybuf[slot], g_ref[...])


def _final(x1, y, tok0, n_tok, mod, fng):
    tm = 512
    blk0 = tok0 // tm
    nblk = n_tok // tm
    assert nblk >= FINAL_SLOTS - 1
    ring = pltpu.VMEM((FINAL_SLOTS, tm, D), F32)
    sems = pltpu.SemaphoreType.DMA((FINAL_SLOTS,))
    return pl.pallas_call(
        functools.partial(_final_kernel, blk0=blk0, nblk=nblk, tm=tm),
        grid=(nblk,),
        in_specs=[pl.BlockSpec(memory_space=pl.ANY),
                  pl.BlockSpec(memory_space=pl.ANY),
                  pl.BlockSpec((1, N_MOD, D), lambda i: (_mod_row(blk0 + i, tm), 0, 0)),
                  pl.BlockSpec((1, D), lambda i: (0, 0))],
        out_specs=pl.BlockSpec((tm, D), lambda i: (i, 0)),
        out_shape=jax.ShapeDtypeStruct((n_tok, D), F32),
        scratch_shapes=[ring, ring, sems, sems],
        compiler_params=_params(("arbitrary",)),
        name="final",
    )(x1, y, mod, fng)


def kernel(x_prompt, x_sample, c, cache_k, cache_v, c_ctx, w_ada, b_ada, norm1_g, w_in, conv_dw, conv_dw_b, conv_ln_g, conv_ln_b, w_conv_out, attn_sink, w_attn_out, w_out, norm2_g, w_router, b_router, w_exp_gate, w_exp_up, w_exp_down, final_norm_g):
    x_ctx = x_prompt.reshape(N_CTX, D)
    x_lat = x_sample.reshape(N_LAT, D)
    cond = jnp.concatenate([c, c_ctx[None, :], jnp.zeros((MOD_ROWS - N_LAT_SEQ - 1, D), F32)], axis=0)
    mod = _ada(cond, w_ada[0], b_ada[0]).reshape(MOD_ROWS, N_MOD, D)

    u = _inproj(_prenorm(x_ctx, x_lat, mod, norm1_g), w_in[0])
    zc = _conv(u, conv_dw[0], conv_dw_b, conv_ln_g, conv_ln_b)

    sink = attn_sink[0]
    cos, sin = _rope_tables()
    att_ctx, state_k, state_v = _ctx_attn(u, sink)
    mixed, w_out_bf = _mix(zc, att_ctx, _lat_attn(u, sink, cache_k, cache_v, cos, sin), u,
                           w_conv_out[0], w_attn_out[0], w_out[0])

    x1, h2, logits = _outproj(mixed, x_ctx, x_lat, mod, norm2_g, w_out_bf, w_router[0], b_router)

    idx, gates = _select(logits.T)
    rows = (idx + (jnp.arange(N_SETS, dtype=jnp.int32) * N_CTX)[:, None, None])
    rows = rows.transpose(1, 0, 2).reshape(N_EXP * MOE_ROWS)
    ffn = _moe(rows, h2, gates, w_exp_gate[0], w_exp_up[0], w_exp_down[0])

    fng = final_norm_g.reshape(1, D)
    y_prompt = _final(x1, ffn, 0, N_CTX, mod, fng).reshape(N_CTX_SEQ, CTX_LEN, D)
    y_sample = _final(x1, ffn, N_CTX, N_LAT, mod, fng).reshape(N_LAT_SEQ, LAT_LEN, D)
    return (y_prompt, y_sample, state_k, state_v)
```
